```python
import math
import jax, jax.numpy as jnp
from jax import lax
import numpy as np

D_MODEL = 1024
BATCH = 8
SEQ = 4096
DEPTH = 4
DEC_BATCH = 32
DEC_SEQ = 32
PAST_LEN = 4096

CHUNK = 64
N_EVEN = (DEPTH + 1) // 2
N_ODD = DEPTH // 2
EPS = 1e-6

H_A = 4
DK_A = 64
DV_A = 128
GATE_RANK = 16
GATE_TAU = 16.0

H_B = 4
Q_LORA = 384
KV_LORA = 256
NOPE = 128
ROPE = 64
V_B = 128
ROPE_BASE = 10000.0
Q_BLOCK = 128
MLA_SCALE = (NOPE + ROPE) ** -0.5

H_C = 8
DK_C = 128
DV_C = D_MODEL // H_C

N_EXPERTS = 32
TOP_K = 4
D_FF = D_MODEL
SWIGLU_LIMIT = 7.0
SWIGLU_ALPHA = 1.702
MOE_BLOCK = 256

A_QK = H_A * DK_A
A_V = H_A * DV_A
EVEN_SPLITS = (A_QK, 2 * A_QK, 2 * A_QK + A_V, 2 * A_QK + 2 * A_V,
               2 * A_QK + 2 * A_V + GATE_RANK,
               2 * A_QK + 2 * A_V + GATE_RANK + Q_LORA,
               2 * A_QK + 2 * A_V + GATE_RANK + Q_LORA + KV_LORA)
EVEN_IN = 2 * A_QK + 2 * A_V + GATE_RANK + Q_LORA + KV_LORA + ROPE
EVEN_MIX = H_A * DV_A + H_B * V_B

C_K = H_C * DK_C
C_V = H_C * DV_C
ODD_SPLITS = (C_K, 2 * C_K, 2 * C_K + C_V)
ODD_IN = 2 * C_K + 2 * C_V
ODD_MIX = C_V

kernel_name = 'chunk_streaming_gla_mla_hgrn2_moe_step'


def rmsnorm(x, g):
    xf = x.astype(jnp.float32)
    y = xf * lax.rsqrt(jnp.mean(xf * xf, axis=-1, keepdims=True) + EPS)
    return (y * g.astype(jnp.float32)).astype(x.dtype)


def rope(x, pos):
    half = ROPE // 2
    inv_freq = jnp.exp(-math.log(ROPE_BASE) * jnp.arange(half, dtype=jnp.float32) / half)
    ang = pos.astype(jnp.float32)[:, None] * inv_freq[None, :]
    cos = jnp.cos(ang)[None, :, None, :]
    sin = jnp.sin(ang)[None, :, None, :]
    xf = x.astype(jnp.float32)
    x1, x2 = xf[..., :half], xf[..., half:]
    return jnp.concatenate([x1 * cos - x2 * sin, x2 * cos + x1 * sin], axis=-1).astype(x.dtype)


def gated_linear_recurrence(q, k, v, log_f, s0, chunk):
    B, T, H, K = q.shape
    V = v.shape[-1]
    n = T // chunk

    def to_chunks(a):
        return a.astype(jnp.float32).reshape(B, n, chunk, H, a.shape[-1]).transpose(1, 0, 3, 2, 4)

    causal = jnp.tril(jnp.ones((chunk, chunk), dtype=bool))[None, None, :, :, None]

    def step(S, inp):
        qb, kb, vb, gb = inp
        b = jnp.cumsum(gb, axis=2)
        diff = b[:, :, :, None, :] - b[:, :, None, :, :]
        decay = jnp.exp(jnp.where(causal, diff, -jnp.inf))
        A = jnp.einsum('bhtd,bhsd,bhtsd->bhts', qb, kb, decay)
        o = jnp.einsum('bhts,bhsv->bhtv', A, vb) + jnp.einsum('bhtk,bhkv->bhtv', qb * jnp.exp(b), S)
        b_last = b[:, :, -1:, :]
        S_new = jnp.exp(b_last[:, :, 0, :])[..., None] * S + jnp.einsum('bhsk,bhsv->bhkv', kb * jnp.exp(b_last - b), vb)
        return S_new, o

    S_T, o = lax.scan(step, s0.astype(jnp.float32), (to_chunks(q), to_chunks(k), to_chunks(v), to_chunks(log_f)))
    o = o.transpose(1, 0, 3, 2, 4).reshape(B, T, H, V)
    return o.astype(v.dtype), S_T.astype(v.dtype)


def mla_attend(q_lat, q_pe, lat, kpe, mask):
    s = jnp.einsum('bqhc,bsc->bhqs', q_lat, lat) + jnp.einsum('bqhr,bsr->bhqs', q_pe, kpe)
    s = s.astype(jnp.float32) * MLA_SCALE
    if mask is not None:
        s = jnp.where(mask[None, None], s, -jnp.inf)
    w = jax.nn.softmax(s, axis=-1).astype(lat.dtype)
    return jnp.einsum('bhqs,bsc->bqhc', w, lat)


def mla_prompt(q_lat, q_pe, lat, kpe):
    B, T, H, C = q_lat.shape
    nb = T // Q_BLOCK
    key_chunk = jnp.arange(T) // CHUNK
    ql = q_lat.reshape(B, nb, Q_BLOCK, H, C).transpose(1, 0, 2, 3, 4)
    qp = q_pe.reshape(B, nb, Q_BLOCK, H, ROPE).transpose(1, 0, 2, 3, 4)

    def one_block(args):
        i, qlb, qpb = args
        q_chunk = (i * Q_BLOCK + jnp.arange(Q_BLOCK)) // CHUNK
        mask = key_chunk[None, :] <= q_chunk[:, None]
        return mla_attend(qlb, qpb, lat, kpe, mask)

    o = lax.map(one_block, (jnp.arange(nb), ql, qp))
    return o.transpose(1, 0, 2, 3, 4).reshape(B, T, H, C)


def even_mixer(h, pos, j, p, past, s0, chunk):
    B, T, _ = h.shape
    z = jnp.einsum('btd,de->bte', h, p['w_in_even'][j])
    q_a, k_a, v_a, r_a, a_lr, cq, ckv, kpe = jnp.split(z, EVEN_SPLITS, axis=-1)
    q_a = q_a.reshape(B, T, H_A, DK_A) * (DK_A ** -0.5)
    k_a = k_a.reshape(B, T, H_A, DK_A)
    v_a = v_a.reshape(B, T, H_A, DV_A)
    a_logit = jnp.einsum('btr,rk->btk', a_lr, p['w_gla_a2'][j]) + p['b_gla_a'][j]
    log_alpha = (jax.nn.log_sigmoid(a_logit.astype(jnp.float32)) / GATE_TAU).reshape(B, T, H_A, DK_A)
    o_a, s_a = gated_linear_recurrence(q_a, k_a, v_a, log_alpha, s0, chunk)
    o_a = rmsnorm(o_a, p['gla_norm_g'][j]) * jax.nn.silu(r_a).reshape(B, T, H_A, DV_A)
    o_a = o_a.reshape(B, T, H_A * DV_A)
    cq = rmsnorm(cq, p['mla_q_norm_g'][j])
    qf = jnp.einsum('btr,re->bte', cq, p['w_mla_uq'][j]).reshape(B, T, H_B, NOPE + ROPE)
    q_nope, q_pe = qf[..., :NOPE], rope(qf[..., NOPE:], pos)
    lat = rmsnorm(ckv, p['mla_kv_norm_g'][j])
    kpe = rope(kpe[:, :, None, :], pos)[:, :, 0, :]
    q_lat = jnp.einsum('bthn,chn->bthc', q_nope, p['w_mla_uk'][j])
    if past is None:
        o_lat = mla_prompt(q_lat, q_pe, lat, kpe)
    else:
        lat_all = jnp.concatenate([past[0], lat], axis=1)
        kpe_all = jnp.concatenate([past[1], kpe], axis=1)
        o_lat = mla_attend(q_lat, q_pe, lat_all, kpe_all, None)
    o_b = jnp.einsum('bthc,chv->bthv', o_lat, p['w_mla_uv'][j]).reshape(B, T, H_B * V_B)
    mix = jnp.concatenate([o_a, o_b], axis=-1)
    out = jnp.einsum('bte,ed->btd', mix, p['w_out_even'][j])
    return out, lat, kpe, s_a


def odd_mixer(h, l, j, p, s0, chunk):
    B, T, _ = h.shape
    z = jnp.einsum('btd,de->bte', h, p['w_in_odd'][j])
    q, f, i, g = jnp.split(z, ODD_SPLITS, axis=-1)
    q = jax.nn.silu(q).reshape(B, T, H_C, DK_C)
    lb_soft = jax.nn.softmax(p['hgrn_lb'].astype(jnp.float32), axis=0)
    lb = (jnp.cumsum(lb_soft, axis=0) - lb_soft[0])[l].reshape(H_C, DK_C)
    fg = lb + (1.0 - lb) * jax.nn.sigmoid(f.astype(jnp.float32).reshape(B, T, H_C, DK_C))
    o, s = gated_linear_recurrence(q, 1.0 - fg, i.reshape(B, T, H_C, DV_C), jnp.log(fg), s0, chunk)
    o = rmsnorm(o, p['hgrn_norm_g'][j]) * jax.nn.silu(g).reshape(B, T, H_C, DV_C)
    return jnp.einsum('bte,ed->btd', o.reshape(B, T, ODD_MIX), p['w_out_odd'][j]), s


def moe(h, l, p):
    B, T, D = h.shape
    x = h.reshape(B * T, D)
    N = B * T
    logits = (x @ p['w_router'][l] + p['b_router'][l]).astype(jnp.float32)
    top_v, top_i = lax.top_k(logits, TOP_K)
    gate = jax.nn.softmax(top_v, axis=-1)
    NK = N * TOP_K
    e_flat = top_i.reshape(NK)
    t_flat = jnp.repeat(jnp.arange(N, dtype=jnp.int32), TOP_K)
    g_flat = gate.reshape(NK)
    order = jnp.argsort(e_flat)
    e_s, t_s, g_s = e_flat[order], t_flat[order], g_flat[order]
    counts = jnp.bincount(e_flat, length=N_EXPERTS)
    starts = jnp.cumsum(counts) - counts
    padded = (counts + MOE_BLOCK - 1) // MOE_BLOCK * MOE_BLOCK
    pends = jnp.cumsum(padded)
    pstarts = pends - padded
    rows = (-(-NK // MOE_BLOCK)) * MOE_BLOCK + N_EXPERTS * MOE_BLOCK
    n_blocks = rows // MOE_BLOCK
    dest = pstarts[e_s] + jnp.arange(NK) - starts[e_s]
    tok = jnp.full((rows,), N, dtype=jnp.int32).at[dest].set(t_s)
    gbuf = jnp.zeros((rows,), jnp.float32).at[dest].set(g_s)
    block_expert = jnp.clip(jnp.searchsorted(pends, jnp.arange(n_blocks) * MOE_BLOCK, side='right'), 0, N_EXPERTS - 1)
    x_pad = jnp.concatenate([x, jnp.zeros((1, D), x.dtype)], axis=0)
    w1, b1, w2, b2 = p['w_e1'][l], p['b_e1'][l], p['w_e2'][l], p['b_e2'][l]

    def expert_block(args):
        tb, gb, e = args
        xe = x_pad[tb]
        hc = xe @ w1[e] + b1[e]
        gt = jnp.minimum(hc[:, :D_FF], SWIGLU_LIMIT)
        up = jnp.clip(hc[:, D_FF:], -SWIGLU_LIMIT, SWIGLU_LIMIT)
        act = gt * jax.nn.sigmoid(gt * SWIGLU_ALPHA) * (up + 1.0)
        return (act @ w2[e] + b2[e]) * gb[:, None].astype(xe.dtype)

    yb = lax.map(expert_block, (tok.reshape(n_blocks, MOE_BLOCK), gbuf.reshape(n_blocks, MOE_BLOCK), block_expert))
    out = jax.ops.segment_sum(yb.reshape(rows, D), tok, num_segments=N + 1)[:N]
    return out.reshape(B, T, D)


def run_trunk(x, c, pos0, past_lat, past_kpe, gla_s0, hgrn_s0, p):
    T = x.shape[1]
    chunk = CHUNK if T % CHUNK == 0 else T
    pos = jnp.arange(T) + pos0
    lats, kpes, glas, hgrns = [], [], [], []
    for l in range(DEPTH):
        mod = jnp.einsum('bd,de->be', jax.nn.silu(c), p['w_ada'][l]) + p['b_ada'][l]
        sh1, sc1, g1, sh2, sc2, g2 = [m[:, None, :] for m in jnp.split(mod, 6, axis=-1)]
        h = rmsnorm(x, p['norm1_g'][l]) * (1.0 + sc1) + sh1
        j = l // 2
        if l % 2 == 0:
            past = None if past_lat is None else (past_lat[j], past_kpe[j])
            out, lat, kpe, s_a = even_mixer(h, pos, j, p, past, gla_s0[j], chunk)
            lats.append(lat)
            kpes.append(kpe)
            glas.append(s_a)
        else:
            out, s_c = odd_mixer(h, l, j, p, hgrn_s0[j], chunk)
            hgrns.append(s_c)
        x = x + g1 * out
        h = rmsnorm(x, p['norm2_g'][l]) * (1.0 + sc2) + sh2
        x = x + g2 * moe(h, l, p)
    y = rmsnorm(x, p['final_norm_g'])
    return y, jnp.stack(lats), jnp.stack(kpes), jnp.stack(glas), jnp.stack(hgrns)


def setup_inputs(seed: int = 0) -> dict:
    key = jax.random.key(seed)
    ks = jax.random.split(key, 34)
    f32 = jnp.float32

    def nrm(k, shape, s):
        return jax.random.normal(k, shape, f32) * s

    def gain(k, shape):
        return 1.0 + 0.01 * jax.random.normal(k, shape, f32)

    return {
        'x_prompt': nrm(ks[0], (BATCH, SEQ, D_MODEL), 1.0),
        'x_sample': nrm(ks[1], (DEC_BATCH, DEC_SEQ, D_MODEL), 1.0),
        'cache_mla_latent': nrm(ks[2], (N_EVEN, DEC_BATCH, PAST_LEN, KV_LORA), 1.0),
        'cache_mla_krope': nrm(ks[3], (N_EVEN, DEC_BATCH, PAST_LEN, ROPE), 1.0),
        'state_gla': nrm(ks[4], (N_EVEN, DEC_BATCH, H_A, DK_A, DV_A), 0.3),
        'state_hgrn': nrm(ks[5], (N_ODD, DEC_BATCH, H_C, DK_C, DV_C), 0.3),
        'c_prompt': nrm(ks[6], (BATCH, D_MODEL), 1.0),
        'c_sample': nrm(ks[7], (DEC_BATCH, D_MODEL), 1.0),
        'w_ada': nrm(ks[8], (DEPTH, D_MODEL, 6 * D_MODEL), 0.5 * D_MODEL ** -0.5),
        'b_ada': nrm(ks[9], (DEPTH, 6 * D_MODEL), 0.01),
        'norm1_g': gain(ks[10], (DEPTH, D_MODEL)),
        'norm2_g': gain(ks[11], (DEPTH, D_MODEL)),
        'w_in_even': nrm(ks[12], (N_EVEN, D_MODEL, EVEN_IN), D_MODEL ** -0.5),
        'w_gla_a2': nrm(ks[13], (N_EVEN, GATE_RANK, A_QK), GATE_RANK ** -0.5),
        'b_gla_a': nrm(ks[14], (N_EVEN, A_QK), 0.5),
        'gla_norm_g': gain(ks[15], (N_EVEN, DV_A)),
        'mla_q_norm_g': gain(ks[16], (N_EVEN, Q_LORA)),
        'w_mla_uq': nrm(ks[17], (N_EVEN, Q_LORA, H_B * (NOPE + ROPE)), Q_LORA ** -0.5),
        'mla_kv_norm_g': gain(ks[18], (N_EVEN, KV_LORA)),
        'w_mla_uk': nrm(ks[19], (N_EVEN, KV_LORA, H_B, NOPE), KV_LORA ** -0.5),
        'w_mla_uv': nrm(ks[20], (N_EVEN, KV_LORA, H_B, V_B), KV_LORA ** -0.5),
        'w_out_even': nrm(ks[21], (N_EVEN, EVEN_MIX, D_MODEL), EVEN_MIX ** -0.5),
        'w_in_odd': nrm(ks[22], (N_ODD, D_MODEL, ODD_IN), D_MODEL ** -0.5),
        'hgrn_lb': nrm(ks[23], (DEPTH, C_K), 0.1),
        'hgrn_norm_g': gain(ks[24], (N_ODD, DV_C)),
        'w_out_odd': nrm(ks[25], (N_ODD, ODD_MIX, D_MODEL), ODD_MIX ** -0.5),
        'w_router': nrm(ks[26], (DEPTH, D_MODEL, N_EXPERTS), D_MODEL ** -0.5),
        'b_router': nrm(ks[27], (DEPTH, N_EXPERTS), 0.01),
        'w_e1': nrm(ks[28], (DEPTH, N_EXPERTS, D_MODEL, 2 * D_FF), D_MODEL ** -0.5),
        'b_e1': nrm(ks[29], (DEPTH, N_EXPERTS, 2 * D_FF), 0.01),
        'w_e2': nrm(ks[30], (DEPTH, N_EXPERTS, D_FF, D_MODEL), D_FF ** -0.5),
        'b_e2': nrm(ks[31], (DEPTH, N_EXPERTS, D_MODEL), 0.01),
        'final_norm_g': gain(ks[32], (D_MODEL,)),
    }


def reference(x_prompt, x_sample, cache_mla_latent, cache_mla_krope, state_gla, state_hgrn, c_prompt, c_sample,
              w_ada, b_ada, norm1_g, norm2_g, w_in_even, w_gla_a2, b_gla_a, gla_norm_g, mla_q_norm_g, w_mla_uq,
              mla_kv_norm_g, w_mla_uk, w_mla_uv, w_out_even, w_in_odd, hgrn_lb, hgrn_norm_g, w_out_odd,
              w_router, b_router, w_e1, b_e1, w_e2, b_e2, final_norm_g):
    p = {'w_ada': w_ada, 'b_ada': b_ada, 'norm1_g': norm1_g, 'norm2_g': norm2_g,
         'w_in_even': w_in_even, 'w_gla_a2': w_gla_a2, 'b_gla_a': b_gla_a, 'gla_norm_g': gla_norm_g,
         'mla_q_norm_g': mla_q_norm_g, 'w_mla_uq': w_mla_uq, 'mla_kv_norm_g': mla_kv_norm_g,
         'w_mla_uk': w_mla_uk, 'w_mla_uv': w_mla_uv, 'w_out_even': w_out_even,
         'w_in_odd': w_in_odd, 'hgrn_lb': hgrn_lb, 'hgrn_norm_g': hgrn_norm_g, 'w_out_odd': w_out_odd,
         'w_router': w_router, 'b_router': b_router, 'w_e1': w_e1, 'b_e1': b_e1, 'w_e2': w_e2, 'b_e2': b_e2,
         'final_norm_g': final_norm_g}
    Bp = x_prompt.shape[0]
    gla0 = jnp.zeros((N_EVEN, Bp, H_A, DK_A, DV_A), x_prompt.dtype)
    hgrn0 = jnp.zeros((N_ODD, Bp, H_C, DK_C, DV_C), x_prompt.dtype)
    y_prompt, lat_p, kpe_p, gla_p, hgrn_p = run_trunk(x_prompt, c_prompt, 0, None, None, gla0, hgrn0, p)
    y_sample, lat_s, kpe_s, gla_s, hgrn_s = run_trunk(x_sample, c_sample, PAST_LEN, cache_mla_latent,
                                                      cache_mla_krope, state_gla, state_hgrn, p)
    return (y_prompt, y_sample, lat_p, kpe_p, gla_p, hgrn_p, lat_s, kpe_s, gla_s, hgrn_s)
```

```python
import functools
import math

import jax
import jax.numpy as jnp
from jax import lax
from jax.experimental import pallas as pl
from jax.experimental.pallas import tpu as pltpu

F32 = jnp.float32
BF16 = jnp.bfloat16
I32 = jnp.int32

EPS = 1e-6
CHUNK = 64
LANE = 128

H_A, DK_A, DV_A, GATE_RANK, GATE_TAU = 4, 64, 128, 16, 16.0
H_B, Q_LORA, KV_LORA, NOPE, ROPE, V_B = 4, 384, 256, 128, 64, 128
ROPE_BASE = 10000.0
MLA_SCALE = (NOPE + ROPE) ** -0.5
KCAT = KV_LORA + LANE
H_C, DK_C, DV_C = 8, 128, 128
TOP_K = 4
SWIGLU_LIMIT = 7.0
SWIGLU_ALPHA = 1.702
MOE_BLOCK = 256
EXP_CLAMP = 80.0

VMEM_LIMIT = 56 * 1024 * 1024


def _cparams(sem, vmem=None):
    return pltpu.CompilerParams(dimension_semantics=sem, vmem_limit_bytes=vmem)


def _pick_tile(n, pref):
    t = pref
    while n % t:
        t //= 2
    return t


def _rms(x, g):
    return x * lax.rsqrt(jnp.mean(x * x, axis=-1, keepdims=True) + EPS) * g


def _group_affine(x, scale, shift, group):
    tm, d = x.shape
    x3 = x.reshape(tm // group, group, d)
    if scale is not None:
        x3 = x3 * scale
    if shift is not None:
        x3 = x3 + shift
    return x3.reshape(tm, d)


def _split3(x):
    hi = x.astype(BF16)
    r1 = x - hi.astype(F32)
    mid = r1.astype(BF16)
    lo = (r1 - mid.astype(F32)).astype(BF16)
    return hi, mid, lo


def _dot(a, b):
    return jnp.dot(a, b, preferred_element_type=F32)


def _dot_nt(a, b):
    return lax.dot_general(a, b, (((1,), (1,)), ((), ())), preferred_element_type=F32)


def _dot_tn(a, b):
    return lax.dot_general(a, b, (((0,), (0,)), ((), ())), preferred_element_type=F32)


def _ada_kernel(c_ref, w_ref, b_ref, o_ref):
    c = c_ref[...]
    a = (c * jax.nn.sigmoid(c)).astype(BF16)
    o_ref[...] = _dot(a, w_ref[...].astype(BF16)) + b_ref[...]


def _ada(c_all, w_ada, b_ada):
    depth, d, n6 = w_ada.shape
    s = c_all.shape[0]
    tn = _pick_tile(n6, 1536)
    return pl.pallas_call(
        _ada_kernel,
        grid=(depth, n6 // tn),
        in_specs=[
            pl.BlockSpec((s, d), lambda l, j: (0, 0)),
            pl.BlockSpec((None, d, tn), lambda l, j: (l, 0, j)),
            pl.BlockSpec((None, 1, tn), lambda l, j: (l, 0, j)),
        ],
        out_specs=pl.BlockSpec((None, s, tn), lambda l, j: (l, 0, j)),
        out_shape=jax.ShapeDtypeStruct((depth, s, n6), F32),
        compiler_params=_cparams(("arbitrary", "arbitrary")),
        name="ada",
    )(c_all, w_ada, b_ada.reshape(depth, 1, n6))


def _inproj_kernel(x_ref, g_ref, sc_ref, sh_ref, w_ref, *out_refs, splits, group):
    h = _rms(x_ref[...], g_ref[...])
    h = _group_affine(h, 1.0 + sc_ref[...], sh_ref[...], group).astype(BF16)
    for o_ref, (c0, c1) in zip(out_refs, splits):
        o_ref[...] = _dot(h, w_ref[:, c0:c1]).astype(o_ref.dtype)


def _inproj(x, gain, modg, l, w, widths, group):
    n, d = x.shape
    tm = _pick_tile(n, 512)
    tg = tm // group
    splits, c = [], 0
    for wd in widths:
        splits.append((c, c + wd))
        c += wd
    mod_spec = lambda comp: pl.BlockSpec((None, None, tg, 1, d), lambda i: (l, comp, i, 0, 0))
    return pl.pallas_call(
        functools.partial(_inproj_kernel, splits=tuple(splits), group=group),
        grid=(n // tm,),
        in_specs=[
            pl.BlockSpec((tm, d), lambda i: (i, 0)),
            pl.BlockSpec((1, d), lambda i: (0, 0)),
            mod_spec(1),
            mod_spec(0),
            pl.BlockSpec((d, c), lambda i: (0, 0)),
        ],
        out_specs=[pl.BlockSpec((tm, wd), lambda i: (i, 0)) for wd in widths],
        out_shape=[jax.ShapeDtypeStruct((n, wd), F32) for wd in widths],
        compiler_params=_cparams(("arbitrary",), VMEM_LIMIT),
        name="inproj",
    )(x, gain.reshape(1, d), modg, modg, w)


def _rec_kernel(*refs, mode, heads, chunk, nchunks, zero_init):
    refs = list(refs)
    if mode == "gla":
        q_ref, k_ref, v_ref, r_ref, a_ref, wah_ref, wal_ref, ab_ref, gn_ref = refs[:9]
        refs = refs[9:]
    else:
        q_ref, k_ref, v_ref, r_ref, lb_ref, gn_ref = refs[:6]
        refs = refs[6:]
    s0_ref = None
    if not zero_init:
        s0_ref = refs.pop(0)
    o_ref, sout_ref, st_scr = refs
    ci = pl.program_id(1)

    @pl.when(ci == 0)
    def _():
        for h in range(heads):
            if zero_init:
                st_scr[h] = jnp.zeros((LANE, LANE), F32)
            else:
                st_scr[h] = s0_ref[0, h].T

    row = lax.broadcasted_iota(I32, (chunk, chunk), 0)
    col = lax.broadcasted_iota(I32, (chunk, chunk), 1)
    causal = row >= col
    tri = jnp.where(causal, 1.0, 0.0).astype(BF16)
    mid = chunk // 2 - 1

    if mode == "gla":
        a = a_ref[...]
        a_hi = a.astype(BF16)
        a_lo = (a - a_hi.astype(F32)).astype(BF16)

    for h in range(heads):
        sl = slice(h * LANE, (h + 1) * LANE)
        if mode == "gla":
            q = q_ref[:, sl] * (DK_A ** -0.5)
            k = k_ref[:, sl]
            alog = (_dot(a_hi, wah_ref[:, sl]) + _dot(a_lo, wah_ref[:, sl])
                    + _dot(a_hi, wal_ref[:, sl]) + ab_ref[:, sl])
            g = jax.nn.log_sigmoid(alog) * (1.0 / GATE_TAU)
        else:
            qr = q_ref[:, sl]
            q = qr * jax.nn.sigmoid(qr)
            lb = lb_ref[:, sl]
            fg = lb + (1.0 - lb) * jax.nn.sigmoid(k_ref[:, sl])
            k = 1.0 - fg
            g = jnp.log(fg)
        g_hi, g_mid, g_lo = _split3(g)
        b = _dot(tri, g_hi) + _dot(tri, g_mid) + _dot(tri, g_lo)
        b_last = b[chunk - 1:chunk, :]
        b_ref_row = b[mid:mid + 1, :]
        qt = (q * jnp.exp(jnp.minimum(b - b_ref_row, EXP_CLAMP))).astype(BF16)
        kt = (k * jnp.exp(jnp.minimum(b_ref_row - b, EXP_CLAMP))).astype(BF16)
        qs = (q * jnp.exp(b)).astype(BF16)
        ks = (k * jnp.exp(b_last - b)).astype(BF16)
        v = v_ref[:, sl].astype(BF16)
        att = jnp.where(causal, _dot_nt(qt, kt), 0.0).astype(BF16)
        st = st_scr[h]
        o = _dot(att, v) + _dot_nt(qs, st.astype(BF16))
        st_scr[h] = st * jnp.exp(b_last) + _dot_tn(v, ks)
        rg = r_ref[:, sl]
        o = _rms(o, gn_ref[...]) * (rg * jax.nn.sigmoid(rg))
        o_ref[:, sl] = o.astype(o_ref.dtype)

    @pl.when(ci == nchunks - 1)
    def _():
        for h in range(heads):
            sout_ref[0, h] = st_scr[h].T


def _recurrence(mode, q, k, v, r, extras, gnorm, s0, *, heads, nseq, seqlen, chunk, row0):
    cols = q.shape[1]
    nchunks = seqlen // chunk
    blk0 = row0 // chunk
    tok = lambda b, c: (blk0 + b * nchunks + c, 0)
    full = lambda a: pl.BlockSpec(a.shape, lambda b, c: (0,) * a.ndim)
    slab = pl.BlockSpec((chunk, cols), tok)
    in_specs = [slab, slab, slab, slab]
    args = [q, k, v, r]
    if mode == "gla":
        a_lr, wah, wal, ab = extras
        in_specs += [pl.BlockSpec((chunk, a_lr.shape[1]), tok), full(wah), full(wal), full(ab)]
        args += [a_lr, wah, wal, ab]
    else:
        (lb,) = extras
        in_specs += [full(lb)]
        args += [lb]
    in_specs.append(full(gnorm))
    args.append(gnorm)
    zero_init = s0 is None
    if not zero_init:
        in_specs.append(pl.BlockSpec((1, heads, LANE, LANE), lambda b, c: (b, 0, 0, 0)))
        args.append(s0)
    return pl.pallas_call(
        functools.partial(_rec_kernel, mode=mode, heads=heads, chunk=chunk, nchunks=nchunks,
                          zero_init=zero_init),
        grid=(nseq, nchunks),
        in_specs=in_specs,
        out_specs=[pl.BlockSpec((chunk, cols), lambda b, c: (b * nchunks + c, 0)),
                   pl.BlockSpec((1, heads, LANE, LANE), lambda b, c: (b, 0, 0, 0))],
        out_shape=[jax.ShapeDtypeStruct((nseq * seqlen, cols), BF16),
                   jax.ShapeDtypeStruct((nseq, heads, LANE, LANE), F32)],
        scratch_shapes=[pltpu.VMEM((heads, LANE, LANE), F32)],
        compiler_params=_cparams(("arbitrary", "arbitrary")),
        name="recurrence_" + mode,
    )(*args)


def _mla_pre_kernel(cq_ref, ckv_ref, kpe_ref, cosq_ref, sinq_ref, cosk_ref, sink_ref, qg_ref, kvg_ref,
                    wuq_ref, wuk_ref, perm_ref, qcat_ref, kcat_ref, lat_ref, kpeo_ref):
    cqn = _rms(cq_ref[...], qg_ref[...]).astype(BF16)
    qf = _dot(cqn, wuq_ref[...])
    off = H_B * NOPE
    x1 = qf[:, off:off + LANE]
    x2 = qf[:, off + LANE:off + 2 * LANE]
    cq, sq = cosq_ref[...], sinq_ref[...]
    o1 = (x1 * cq - x2 * sq) * MLA_SCALE
    o2 = (x2 * cq + x1 * sq) * MLA_SCALE
    pe = _dot(o1.astype(BF16), perm_ref[0:LANE, :]) + _dot(o2.astype(BF16), perm_ref[LANE:2 * LANE, :])
    for h in range(H_B):
        ql = _dot(qf[:, h * NOPE:(h + 1) * NOPE].astype(BF16), wuk_ref[h]) * MLA_SCALE
        qcat_ref[h, :, 0:KV_LORA] = ql.astype(BF16)
        qcat_ref[h, :, KV_LORA:KCAT] = pe[:, h * LANE:(h + 1) * LANE].astype(BF16)
    latn = _rms(ckv_ref[...], kvg_ref[...])
    lat_ref[...] = latn
    x = kpe_ref[...]
    half = ROPE // 2
    lane = lax.broadcasted_iota(I32, x.shape, 1)
    rot = jnp.where(lane < half, -pltpu.roll(x, LANE - half, 1), pltpu.roll(x, half, 1))
    kro = x * cosk_ref[...] + rot * sink_ref[...]
    kpeo_ref[...] = kro[:, 0:ROPE]
    kcat_ref[:, 0:KV_LORA] = latn.astype(BF16)
    kcat_ref[:, KV_LORA:KCAT] = kro.astype(BF16)


def _mla_pre(zcq, zckv, zkpe, tabs, qg, kvg, wuq, wuk, perm):
    n = zcq.shape[0]
    tm = _pick_tile(n, 512)
    tokspec = lambda wd: pl.BlockSpec((tm, wd), lambda i: (i, 0))
    full = lambda a: pl.BlockSpec(a.shape, lambda i: (0,) * a.ndim)
    return pl.pallas_call(
        _mla_pre_kernel,
        grid=(n // tm,),
        in_specs=[tokspec(Q_LORA), tokspec(KV_LORA), tokspec(LANE)] + [tokspec(LANE)] * 4
                 + [full(qg), full(kvg), full(wuq), full(wuk), full(perm)],
        out_specs=[pl.BlockSpec((H_B, tm, KCAT), lambda i: (0, i, 0)), tokspec(KCAT),
                   tokspec(KV_LORA), tokspec(ROPE)],
        out_shape=[jax.ShapeDtypeStruct((H_B, n, KCAT), BF16), jax.ShapeDtypeStruct((n, KCAT), BF16),
                   jax.ShapeDtypeStruct((n, KV_LORA), F32), jax.ShapeDtypeStruct((n, ROPE), F32)],
        compiler_params=_cparams(("arbitrary",)),
        name="mla_pre",
    )(zcq, zckv, zkpe, *tabs, qg, kvg, wuq, wuk, perm)


def _softmax_update(s, vals, m_scr, l_scr, acc_scr):
    m_prev = m_scr[...]
    m_new = jnp.maximum(m_prev, jnp.max(s, axis=1, keepdims=True))
    alpha = jnp.exp(m_prev - m_new)
    p = jnp.exp(s - m_new)
    l_scr[...] = alpha * l_scr[...] + jnp.sum(p, axis=1, keepdims=True)
    acc_scr[...] = alpha * acc_scr[...] + _dot(p.astype(BF16), vals)
    m_scr[...] = m_new


def _softmax_init(m_scr, l_scr, acc_scr):
    m_scr[...] = jnp.full(m_scr.shape, -jnp.inf, F32)
    l_scr[...] = jnp.zeros(l_scr.shape, F32)
    acc_scr[...] = jnp.zeros(acc_scr.shape, F32)


def _attn_finish(o_ref, wuv_ref, l_scr, acc_scr, tq):
    inv = 1.0 / l_scr[...]
    for h in range(H_B):
        rows = slice(h * tq, (h + 1) * tq)
        oh = (acc_scr[rows, :] * inv[rows, :]).astype(BF16)
        o_ref[:, h * V_B:(h + 1) * V_B] = _dot(oh, wuv_ref[h]).astype(o_ref.dtype)


def _attn_prompt_kernel(q_ref, k_ref, wuv_ref, o_ref, m_scr, l_scr, acc_scr, *, tq, tk, nkb):
    qi = pl.program_id(1)
    ki = pl.program_id(2)
    kb_last = ((qi + 1) * tq - 1) // tk

    @pl.when(ki == 0)
    def _():
        _softmax_init(m_scr, l_scr, acc_scr)

    @pl.when(ki <= kb_last)
    def _():
        q = q_ref[...].reshape(H_B * tq, KCAT)
        k = k_ref[...]
        s = _dot_nt(q, k)
        tok = qi * tq + (lax.broadcasted_iota(I32, (H_B * tq, 1), 0) & (tq - 1))
        limit = (tok // CHUNK + 1) * CHUNK
        key = ki * tk + lax.broadcasted_iota(I32, (1, tk), 1)
        s = jnp.where(key < limit, s, -jnp.inf)
        _softmax_update(s, k[:, 0:KV_LORA], m_scr, l_scr, acc_scr)

    @pl.when(ki == nkb - 1)
    def _():
        _attn_finish(o_ref, wuv_ref, l_scr, acc_scr, tq)


def _attn_prompt(qcat, kcat, wuv, nseq, seqlen):
    n = nseq * seqlen
    tq = 128
    tk = _pick_tile(seqlen, 512)
    nq, nkb = seqlen // tq, seqlen // tk
    return pl.pallas_call(
        functools.partial(_attn_prompt_kernel, tq=tq, tk=tk, nkb=nkb),
        grid=(nseq, nq, nkb),
        in_specs=[
            pl.BlockSpec((H_B, tq, KCAT), lambda b, qi, ki: (0, b * nq + qi, 0)),
            pl.BlockSpec((tk, KCAT),
                         lambda b, qi, ki: (b * nkb + jnp.minimum(ki, ((qi + 1) * tq - 1) // tk), 0)),
            pl.BlockSpec(wuv.shape, lambda b, qi, ki: (0, 0, 0)),
        ],
        out_specs=pl.BlockSpec((tq, H_B * V_B), lambda b, qi, ki: (b * nq + qi, 0)),
        out_shape=jax.ShapeDtypeStruct((n, H_B * V_B), BF16),
        scratch_shapes=[pltpu.VMEM((H_B * tq, 1), F32), pltpu.VMEM((H_B * tq, 1), F32),
                        pltpu.VMEM((H_B * tq, KV_LORA), F32)],
        compiler_params=_cparams(("arbitrary", "arbitrary", "arbitrary")),
        name="attn_prompt",
    )(qcat, kcat, wuv)


def _attn_sample_kernel(q_ref, plat_ref, pkpe_ref, knew_ref, wuv_ref, o_ref,
                        m_scr, l_scr, acc_scr, *, tq, nkp):
    ki = pl.program_id(1)

    @pl.when(ki == 0)
    def _():
        _softmax_init(m_scr, l_scr, acc_scr)

    q = q_ref[...].reshape(H_B * tq, KCAT)
    lat = plat_ref[...].astype(BF16)
    kpe = pkpe_ref[...].astype(BF16)
    s = _dot_nt(q[:, 0:KV_LORA], lat) + _dot_nt(q[:, KV_LORA:KV_LORA + ROPE], kpe)
    _softmax_update(s, lat, m_scr, l_scr, acc_scr)

    @pl.when(ki == nkp - 1)
    def _():
        kn = knew_ref[...]
        _softmax_update(_dot_nt(q, kn), kn[:, 0:KV_LORA], m_scr, l_scr, acc_scr)
        _attn_finish(o_ref, wuv_ref, l_scr, acc_scr, tq)


def _attn_sample(qcat, kcat, past_lat, past_kpe, wuv, row0):
    nseq, past, _ = past_lat.shape
    tq = (kcat.shape[0] - row0) // nseq
    tkp = _pick_tile(past, 1024)
    nkp = past // tkp
    blk0 = row0 // tq
    return pl.pallas_call(
        functools.partial(_attn_sample_kernel, tq=tq, nkp=nkp),
        grid=(nseq, nkp),
        in_specs=[
            pl.BlockSpec((H_B, tq, KCAT), lambda b, ki: (0, blk0 + b, 0)),
            pl.BlockSpec((None, tkp, KV_LORA), lambda b, ki: (b, ki, 0)),
            pl.BlockSpec((None, tkp, ROPE), lambda b, ki: (b, ki, 0)),
            pl.BlockSpec((tq, KCAT), lambda b, ki: (blk0 + b, 0)),
            pl.BlockSpec(wuv.shape, lambda b, ki: (0, 0, 0)),
        ],
        out_specs=pl.BlockSpec((tq, H_B * V_B), lambda b, ki: (b, 0)),
        out_shape=jax.ShapeDtypeStruct((nseq * tq, H_B * V_B), BF16),
        scratch_shapes=[pltpu.VMEM((H_B * tq, 1), F32), pltpu.VMEM((H_B * tq, 1), F32),
                        pltpu.VMEM((H_B * tq, KV_LORA), F32)],
        compiler_params=_cparams(("arbitrary", "arbitrary")),
        name="attn_sample",
    )(qcat, past_lat, past_kpe, kcat, wuv)


def _outproj_kernel(*refs, nlhs, group, ptiles):
    x_ref, g_ref = refs[0], refs[1]
    lhs_p = refs[2:2 + nlhs]
    lhs_s = refs[2 + nlhs:2 + 2 * nlhs]
    ws = refs[2 + 2 * nlhs:2 + 3 * nlhs]
    o_ref = refs[2 + 3 * nlhs]
    is_prompt = pl.program_id(0) < ptiles
    acc = None
    for ap, asm, w in zip(lhs_p, lhs_s, ws):
        a = jnp.where(is_prompt, ap[...], asm[...])
        t = _dot(a, w[...])
        acc = t if acc is None else acc + t
    o_ref[...] = x_ref[...] + _group_affine(acc, g_ref[...], None, group)


def _outproj(x, modg, l, lhs_p, lhs_s, ws, group):
    n, d = x.shape
    n_p, n_s = lhs_p[0].shape[0], lhs_s[0].shape[0]
    tm = _pick_tile(math.gcd(n_p, n_s), 512)
    tg = tm // group
    ptiles = n_p // tm
    return pl.pallas_call(
        functools.partial(_outproj_kernel, nlhs=len(ws), group=group, ptiles=ptiles),
        grid=(n // tm,),
        in_specs=[pl.BlockSpec((tm, d), lambda i: (i, 0)),
                  pl.BlockSpec((None, None, tg, 1, d), lambda i: (l, 2, i, 0, 0))]
                 + [pl.BlockSpec((tm, a.shape[1]), lambda i: (jnp.minimum(i, ptiles - 1), 0)) for a in lhs_p]
                 + [pl.BlockSpec((tm, a.shape[1]), lambda i: (jnp.maximum(i - ptiles, 0), 0)) for a in lhs_s]
                 + [pl.BlockSpec(w.shape, lambda i: (0, 0)) for w in ws],
        out_specs=pl.BlockSpec((tm, d), lambda i: (i, 0)),
        out_shape=jax.ShapeDtypeStruct((n, d), F32),
        compiler_params=_cparams(("arbitrary",)),
        name="outproj",
    )(x, modg, *lhs_p, *lhs_s, *ws)


def _router_kernel(x_ref, g_ref, sc_ref, sh_ref, wh_ref, wl_ref, br_ref, h_ref, ti_ref, gt_ref,
                   *, group, nexp):
    h = _rms(x_ref[...], g_ref[...])
    h = _group_affine(h, 1.0 + sc_ref[...], sh_ref[...], group)
    h_ref[...] = h
    h_hi = h.astype(BF16)
    h_lo = (h - h_hi.astype(F32)).astype(BF16)
    logits = _dot(h_hi, wh_ref[...]) + _dot(h_lo, wh_ref[...]) + _dot(h_hi, wl_ref[...]) + br_ref[...]
    lane = lax.broadcasted_iota(I32, logits.shape, 1)
    lane_f = lane.astype(F32)
    cur = jnp.where(lane < nexp, logits, -jnp.inf)
    tops, idxs = [], []
    for _ in range(TOP_K):
        m = jnp.max(cur, axis=1, keepdims=True)
        i = jnp.min(jnp.where(cur == m, lane_f, float(LANE)), axis=1, keepdims=True)
        cur = jnp.where(lane_f == i, -jnp.inf, cur)
        tops.append(m)
        idxs.append(i.astype(I32))
    es = [jnp.exp(t - tops[0]) for t in tops]
    inv = 1.0 / (es[0] + es[1] + es[2] + es[3])
    ti = jnp.zeros(logits.shape, I32)
    gt = jnp.zeros(logits.shape, F32)
    for k in range(TOP_K):
        ti = jnp.where(lane == k, idxs[k], ti)
        gt = jnp.where(lane == k, es[k] * inv, gt)
    ti_ref[...] = ti
    gt_ref[...] = gt


def _router(x, gain, modg, l, wr_hi, wr_lo, br, group, nexp):
    n, d = x.shape
    tm = _pick_tile(n, 512)
    tg = tm // group
    mod_spec = lambda comp: pl.BlockSpec((None, None, tg, 1, d), lambda i: (l, comp, i, 0, 0))
    full = lambda a: pl.BlockSpec(a.shape, lambda i: (0,) * a.ndim)
    tok = lambda wd: pl.BlockSpec((tm, wd), lambda i: (i, 0))
    return pl.pallas_call(
        functools.partial(_router_kernel, group=group, nexp=nexp),
        grid=(n // tm,),
        in_specs=[tok(d), pl.BlockSpec((1, d), lambda i: (0, 0)), mod_spec(4), mod_spec(3),
                  full(wr_hi), full(wr_lo), full(br)],
        out_specs=[tok(d), tok(LANE), tok(LANE)],
        out_shape=[jax.ShapeDtypeStruct((n, d), F32), jax.ShapeDtypeStruct((n, LANE), I32),
                   jax.ShapeDtypeStruct((n, LANE), F32)],
        compiler_params=_cparams(("arbitrary",)),
        name="router",
    )(x, gain.reshape(1, d), modg, modg, wr_hi, wr_lo, br)


def _rank_kernel(ti_ref, rank_ref, cnt_ref, carry_scr, *, tm):
    i = pl.program_id(0)

    @pl.when(i == 0)
    def _():
        carry_scr[...] = jnp.zeros(carry_scr.shape, F32)

    ti = ti_ref[...]
    lane = lax.broadcasted_iota(I32, ti.shape, 1)
    sel = [lane == ti[:, k:k + 1] for k in range(TOP_K)]
    hot = jnp.zeros(ti.shape, F32)
    for s in sel:
        hot = hot + jnp.where(s, 1.0, 0.0)
    row = lax.broadcasted_iota(I32, (tm, tm), 0)
    col = lax.broadcasted_iota(I32, (tm, tm), 1)
    strict = jnp.where(row > col, 1.0, 0.0).astype(BF16)
    before = _dot(strict, hot.astype(BF16)) + carry_scr[0:1, :]
    rank = jnp.zeros(ti.shape, F32)
    for k in range(TOP_K):
        rk = jnp.sum(jnp.where(sel[k], before, 0.0), axis=1, keepdims=True)
        rank = jnp.where(lane == k, rk, rank)
    rank_ref[...] = rank.astype(I32)
    carry_scr[...] = carry_scr[...] + jnp.sum(hot, axis=0, keepdims=True)
    cnt_ref[...] = carry_scr[...]


def _rank(topi):
    n = topi.shape[0]
    tm = _pick_tile(n, 512)
    return pl.pallas_call(
        functools.partial(_rank_kernel, tm=tm),
        grid=(n // tm,),
        in_specs=[pl.BlockSpec((tm, LANE), lambda i: (i, 0))],
        out_specs=[pl.BlockSpec((tm, LANE), lambda i: (i, 0)), pl.BlockSpec((8, LANE), lambda i: (0, 0))],
        out_shape=[jax.ShapeDtypeStruct((n, LANE), I32), jax.ShapeDtypeStruct((8, LANE), F32)],
        scratch_shapes=[pltpu.VMEM((8, LANE), F32)],
        compiler_params=_cparams(("arbitrary",)),
        name="rank",
    )(topi)


def _dest_kernel(ti_ref, rank_ref, ps_ref, d_ref):
    ti = ti_ref[...]
    lane = lax.broadcasted_iota(I32, ti.shape, 1)
    ps = ps_ref[...]
    dest = rank_ref[...]
    for k in range(TOP_K):
        base = jnp.sum(jnp.where(lane == ti[:, k:k + 1], ps, 0.0), axis=1, keepdims=True).astype(I32)
        dest = dest + jnp.where(lane == k, base, 0)
    d_ref[...] = dest


def _dest(topi, rank, pstarts_row):
    n = topi.shape[0]
    tm = _pick_tile(n, 512)
    tok = pl.BlockSpec((tm, LANE), lambda i: (i, 0))
    return pl.pallas_call(
        _dest_kernel,
        grid=(n // tm,),
        in_specs=[tok, tok, pl.BlockSpec((1, LANE), lambda i: (0, 0))],
        out_specs=tok,
        out_shape=jax.ShapeDtypeStruct((n, LANE), I32),
        compiler_params=_cparams(("arbitrary",)),
        name="dest",
    )(topi, rank, pstarts_row)


def _dispatch_kernel(dest_ref, pad_ref, h_ref, xs_ref, zbuf, sem, zsem, *, tm, nexp, nblocks):
    def issue(r, carry):
        for k in range(TOP_K):
            f = r * TOP_K + k
            d = dest_ref[f // LANE, f % LANE]
            pltpu.make_async_copy(h_ref.at[pl.ds(r, 1)], xs_ref.at[pl.ds(d, 1)], sem).start()
        return carry

    lax.fori_loop(0, tm, issue, 0)

    @pl.when(pl.program_id(0) == pl.num_programs(0) - 1)
    def _():
        zbuf[...] = jnp.zeros(zbuf.shape, F32)

        def fill_expert(e, carry):
            end = pad_ref[0, e]
            npad = pad_ref[1, e]
            p = MOE_BLOCK // 2
            while p >= 1:
                bit = npad & p
                end = end - bit

                @pl.when(bit != 0)
                def _(end=end, p=p):
                    if p >= 8:
                        start = pl.multiple_of(end, 8)
                        pltpu.make_async_copy(zbuf.at[pl.ds(0, p)], xs_ref.at[pl.ds(start, p)], zsem).start()
                    else:
                        for i in range(p):
                            pltpu.make_async_copy(zbuf.at[pl.ds(0, 1)], xs_ref.at[pl.ds(end + i, 1)],
                                                  zsem).start()

                p //= 2
            return carry

        lax.fori_loop(0, nexp, fill_expert, 0)
        nused = pad_ref[2, 0]

        def fill_tail(b, carry):
            @pl.when(b >= nused)
            def _():
                start = pl.multiple_of(b * MOE_BLOCK, MOE_BLOCK)
                pltpu.make_async_copy(zbuf, xs_ref.at[pl.ds(start, MOE_BLOCK)], zsem).start()

            return carry

        lax.fori_loop(0, nblocks, fill_tail, 0)
        for _ in range(nexp):
            pltpu.make_async_copy(zbuf, xs_ref.at[pl.ds(0, MOE_BLOCK)], zsem).wait()

    for k in range(TOP_K):
        pltpu.make_async_copy(h_ref, xs_ref.at[pl.ds(0, tm)], sem).wait()


def _dispatch(dest2d, padinfo, h, rows, nexp):
    n, d = h.shape
    tm = _pick_tile(n, 256)
    tr = tm * TOP_K // LANE
    return pl.pallas_call(
        functools.partial(_dispatch_kernel, tm=tm, nexp=nexp, nblocks=rows // MOE_BLOCK),
        grid=(n // tm,),
        in_specs=[pl.BlockSpec((tr, LANE), lambda i: (i, 0), memory_space=pltpu.SMEM),
                  pl.BlockSpec(memory_space=pltpu.SMEM),
                  pl.BlockSpec((tm, d), lambda i: (i, 0))],
        out_specs=pl.BlockSpec(memory_space=pl.ANY),
        out_shape=jax.ShapeDtypeStruct((rows, d), F32),
        scratch_shapes=[pltpu.VMEM((MOE_BLOCK, d), F32), pltpu.SemaphoreType.DMA(()),
                        pltpu.SemaphoreType.DMA(())],
        compiler_params=pltpu.CompilerParams(dimension_semantics=("arbitrary",), has_side_effects=True),
        name="dispatch",
    )(dest2d, padinfo, h)


def _experts_kernel(bexp_ref, nused_ref, xs_ref, w1_ref, b1_ref, w2_ref, b2_ref, y_ref,
                    w1b, w2b, act_scr, prev_scr, *, dff, dm):
    i = pl.program_id(0)
    nu = nused_ref[0]
    e = bexp_ref[jnp.minimum(i, nu - 1)]

    @pl.when(i == 0)
    def _():
        prev_scr[0] = -1

    @pl.when(i >= nu)
    def _():
        y_ref[...] = jnp.zeros(y_ref.shape, F32)

    @pl.when(i < nu)
    def _():
        @pl.when(e != prev_scr[0])
        def _():
            rows = 128

            def cast1(c, carry):
                r0 = pl.multiple_of(c * rows, rows)
                w1b[pl.ds(r0, rows), :] = w1_ref[pl.ds(r0, rows), :].astype(BF16)
                return carry

            def cast2(c, carry):
                r0 = pl.multiple_of(c * rows, rows)
                w2b[pl.ds(r0, rows), :] = w2_ref[pl.ds(r0, rows), :].astype(BF16)
                return carry

            lax.fori_loop(0, dm // rows, cast1, 0)
            lax.fori_loop(0, dff // rows, cast2, 0)
            prev_scr[0] = e

        x = xs_ref[...].astype(BF16)
        cw = 256
        for c in range(dff // cw):
            gt = _dot(x, w1b[:, c * cw:(c + 1) * cw]) + b1_ref[:, c * cw:(c + 1) * cw]
            up = _dot(x, w1b[:, dff + c * cw:dff + (c + 1) * cw]) + b1_ref[:, dff + c * cw:dff + (c + 1) * cw]
            gt = jnp.minimum(gt, SWIGLU_LIMIT)
            up = jnp.clip(up, -SWIGLU_LIMIT, SWIGLU_LIMIT)
            act = gt * jax.nn.sigmoid(gt * SWIGLU_ALPHA) * (up + 1.0)
            act_scr[:, c * cw:(c + 1) * cw] = act.astype(BF16)
        a = act_scr[...]
        for c in range(dm // cw):
            y_ref[:, c * cw:(c + 1) * cw] = _dot(a, w2b[:, c * cw:(c + 1) * cw]) + b2_ref[:, c * cw:(c + 1) * cw]


def _experts(bexp, nused, xs, w1, b1, w2, b2, l):
    rows, dm = xs.shape
    dff = w2.shape[2]
    nb = rows // MOE_BLOCK
    blk = lambda i, be, nu: (jnp.minimum(i, nu[0] - 1), 0)
    wsel = lambda i, be, nu: (l, be[jnp.minimum(i, nu[0] - 1)], 0, 0)
    grid_spec = pltpu.PrefetchScalarGridSpec(
        num_scalar_prefetch=2,
        grid=(nb,),
        in_specs=[
            pl.BlockSpec((MOE_BLOCK, dm), blk),
            pl.BlockSpec((None, None, dm, 2 * dff), wsel),
            pl.BlockSpec((None, None, 1, 2 * dff), wsel),
            pl.BlockSpec((None, None, dff, dm), wsel),
            pl.BlockSpec((None, None, 1, dm), wsel),
        ],
        out_specs=pl.BlockSpec((MOE_BLOCK, dm), lambda i, be, nu: (i, 0)),
        scratch_shapes=[pltpu.VMEM((dm, 2 * dff), BF16), pltpu.VMEM((dff, dm), BF16),
                        pltpu.VMEM((MOE_BLOCK, dff), BF16), pltpu.SMEM((1,), I32)],
    )
    return pl.pallas_call(
        functools.partial(_experts_kernel, dff=dff, dm=dm),
        grid_spec=grid_spec,
        out_shape=jax.ShapeDtypeStruct((rows, dm), F32),
        compiler_params=_cparams(("arbitrary",), VMEM_LIMIT),
        name="experts",
    )(bexp, nused, xs, w1, b1, w2, b2)


def _combine_kernel(dest_ref, gate_ref, x_ref, g_ref, y_ref, o_ref, buf, sem, *, tm, group):
    def issue(r, carry):
        for k in range(TOP_K):
            f = r * TOP_K + k
            d = dest_ref[f // LANE, f % LANE]
            pltpu.make_async_copy(y_ref.at[pl.ds(d, 1)], buf.at[k, pl.ds(r, 1)], sem).start()
        return carry

    lax.fori_loop(0, tm, issue, 0)
    for k in range(TOP_K):
        pltpu.make_async_copy(y_ref.at[pl.ds(0, tm)], buf.at[k], sem).wait()
    gate = gate_ref[...]
    moe = gate[:, 0:1] * buf[0]
    for k in range(1, TOP_K):
        moe = moe + gate[:, k:k + 1] * buf[k]
    o_ref[...] = x_ref[...] + _group_affine(moe, g_ref[...], None, group)


def _combine(dest2d, gate, x, modg, l, y, group):
    n, d = x.shape
    tm = _pick_tile(n, 256)
    tr = tm * TOP_K // LANE
    tg = tm // group
    return pl.pallas_call(
        functools.partial(_combine_kernel, tm=tm, group=group),
        grid=(n // tm,),
        in_specs=[pl.BlockSpec((tr, LANE), lambda i: (i, 0), memory_space=pltpu.SMEM),
                  pl.BlockSpec((tm, LANE), lambda i: (i, 0)),
                  pl.BlockSpec((tm, d), lambda i: (i, 0)),
                  pl.BlockSpec((None, None, tg, 1, d), lambda i: (l, 5, i, 0, 0)),
                  pl.BlockSpec(memory_space=pl.ANY)],
        out_specs=pl.BlockSpec((tm, d), lambda i: (i, 0)),
        out_shape=jax.ShapeDtypeStruct((n, d), F32),
        scratch_shapes=[pltpu.VMEM((TOP_K, tm, d), F32), pltpu.SemaphoreType.DMA(())],
        compiler_params=_cparams(("arbitrary",)),
        name="combine",
    )(dest2d, gate, x, modg, y)


def _final_norm_kernel(x_ref, g_ref, o_ref):
    o_ref[...] = _rms(x_ref[...], g_ref[...])


def _final_norm(x, g):
    n, d = x.shape
    tm = _pick_tile(n, 512)
    return pl.pallas_call(
        _final_norm_kernel,
        grid=(n // tm,),
        in_specs=[pl.BlockSpec((tm, d), lambda i: (i, 0)), pl.BlockSpec((1, d), lambda i: (0, 0))],
        out_specs=pl.BlockSpec((tm, d), lambda i: (i, 0)),
        out_shape=jax.ShapeDtypeStruct((n, d), F32),
        compiler_params=_cparams(("arbitrary",)),
        name="final_norm",
    )(x, g.reshape(1, d))


def _pad_heads(w, heads, dk):
    lead = w.shape[:-1]
    w = w.reshape(*lead, heads, dk)
    w = jnp.pad(w, [(0, 0)] * len(lead) + [(0, 0), (0, LANE - dk)])
    return w.reshape(*lead, heads * LANE)


def _pad_cols(w, width):
    return jnp.pad(w, [(0, 0)] * (w.ndim - 1) + [(0, width - w.shape[-1])])


EVEN_WIDTHS = (H_A * LANE, H_A * LANE, H_A * DV_A, H_A * DV_A, LANE, Q_LORA, KV_LORA, LANE)


def _even_weight(w):
    a_qk, a_v = H_A * DK_A, H_A * DV_A
    c = [0, a_qk, 2 * a_qk, 2 * a_qk + a_v, 2 * a_qk + 2 * a_v, 2 * a_qk + 2 * a_v + GATE_RANK]
    c.append(c[-1] + Q_LORA)
    c.append(c[-1] + KV_LORA)
    c.append(c[-1] + ROPE)
    parts = [
        _pad_heads(w[:, c[0]:c[1]], H_A, DK_A),
        _pad_heads(w[:, c[1]:c[2]], H_A, DK_A),
        w[:, c[2]:c[3]],
        w[:, c[3]:c[4]],
        _pad_cols(w[:, c[4]:c[5]], LANE),
        w[:, c[5]:c[6]],
        w[:, c[6]:c[7]],
        _pad_cols(w[:, c[7]:c[8]], LANE),
    ]
    return jnp.concatenate(parts, axis=1).astype(BF16)


def _uq_weight(w):
    w = w.reshape(Q_LORA, H_B, NOPE + ROPE)
    half = ROPE // 2
    nope = w[:, :, :NOPE].reshape(Q_LORA, H_B * NOPE)
    r1 = w[:, :, NOPE:NOPE + half].reshape(Q_LORA, H_B * half)
    r2 = w[:, :, NOPE + half:].reshape(Q_LORA, H_B * half)
    return jnp.concatenate([nope, r1, r2], axis=1).astype(BF16)


def _rope_perm():
    half = ROPE // 2
    r = jnp.arange(2 * LANE)
    second = r // LANE
    h = (r % LANE) // half
    i = r % half
    col = h * LANE + second * half + i
    return (col[:, None] == jnp.arange(H_B * LANE)[None, :]).astype(BF16)


def _rope_tables(pos):
    half = ROPE // 2
    inv_freq = jnp.exp(-math.log(ROPE_BASE) * jnp.arange(half, dtype=F32) / half)
    ang = pos.astype(F32)[:, None] * inv_freq[None, :]
    cos, sin = jnp.cos(ang), jnp.sin(ang)
    z = jnp.zeros((pos.shape[0], LANE - ROPE), F32)
    cosq, sinq = jnp.tile(cos, (1, LANE // half)), jnp.tile(sin, (1, LANE // half))
    cosk = jnp.concatenate([cos, cos, z], axis=1)
    sink = jnp.concatenate([sin, sin, z], axis=1)
    return cosq, sinq, cosk, sink


def kernel(x_prompt, x_sample, cache_mla_latent, cache_mla_krope, state_gla, state_hgrn, c_prompt, c_sample,
           w_ada, b_ada, norm1_g, norm2_g, w_in_even, w_gla_a2, b_gla_a, gla_norm_g, mla_q_norm_g, w_mla_uq,
           mla_kv_norm_g, w_mla_uk, w_mla_uv, w_out_even, w_in_odd, hgrn_lb, hgrn_norm_g, w_out_odd,
           w_router, b_router, w_e1, b_e1, w_e2, b_e2, final_norm_g):
    bp, tp, d = x_prompt.shape
    bs, ts, _ = x_sample.shape
    past = cache_mla_latent.shape[2]
    depth = w_ada.shape[0]
    nexp = w_router.shape[2]
    n_p, n_s = bp * tp, bs * ts
    n = n_p + n_s
    group = ts
    assert tp % group == 0 and group % 8 == 0 and tp % CHUNK == 0 and ts <= CHUNK
    assert (n * TOP_K) % MOE_BLOCK == 0
    n_even, n_odd = (depth + 1) // 2, depth // 2

    x = jnp.concatenate([x_prompt.reshape(n_p, d), x_sample.reshape(n_s, d)], axis=0)

    mod = _ada(jnp.concatenate([c_prompt, c_sample], axis=0), w_ada, b_ada)
    mod = jnp.concatenate([jnp.repeat(mod[:, :bp], tp // group, axis=1), mod[:, bp:]], axis=1)
    modg = mod.reshape(depth, n // group, 6, 1, d).transpose(0, 2, 1, 3, 4)

    pos = jnp.concatenate([jnp.tile(jnp.arange(tp), bp), jnp.tile(jnp.arange(ts) + past, bs)])
    tabs = _rope_tables(pos)
    perm = _rope_perm()

    lb_soft = jax.nn.softmax(hgrn_lb.astype(F32), axis=0)
    lb_all = jnp.cumsum(lb_soft, axis=0) - lb_soft[0]

    lat_p, kpe_p, gla_p, hgrn_p, lat_s, kpe_s, gla_s, hgrn_s = [], [], [], [], [], [], [], []
    rows = (-(-(n * TOP_K) // MOE_BLOCK)) * MOE_BLOCK + nexp * MOE_BLOCK
    nblocks = rows // MOE_BLOCK

    for l in range(depth):
        j = l // 2
        if l % 2 == 0:
            zq, zk, zv, zr, za, zcq, zckv, zkpe = _inproj(
                x, norm1_g[l], modg, l, _even_weight(w_in_even[j]), EVEN_WIDTHS, group)
            wa = _pad_heads(jnp.pad(w_gla_a2[j], ((0, LANE - GATE_RANK), (0, 0))), H_A, DK_A)
            wa_hi = wa.astype(BF16)
            wa_lo = (wa - wa_hi.astype(F32)).astype(BF16)
            ab = _pad_heads(b_gla_a[j].reshape(1, -1), H_A, DK_A)
            gn = gla_norm_g[j].reshape(1, DV_A)
            extras = (za, wa_hi, wa_lo, ab)
            oa_p, sp = _recurrence("gla", zq, zk, zv, zr, extras, gn, None, heads=H_A, nseq=bp,
                                   seqlen=tp, chunk=CHUNK, row0=0)
            s0 = jnp.pad(state_gla[j], ((0, 0), (0, 0), (0, LANE - DK_A), (0, 0)))
            oa_s, ss = _recurrence("gla", zq, zk, zv, zr, extras, gn, s0, heads=H_A, nseq=bs,
                                   seqlen=ts, chunk=ts, row0=n_p)
            gla_p.append(sp[:, :, :DK_A, :])
            gla_s.append(ss[:, :, :DK_A, :])

            wuk = w_mla_uk[j].transpose(1, 2, 0).astype(BF16)
            wuv = w_mla_uv[j].transpose(1, 0, 2).astype(BF16)
            qcat, kcat, lat, kpe = _mla_pre(
                zcq, zckv, zkpe, tabs, mla_q_norm_g[j].reshape(1, -1), mla_kv_norm_g[j].reshape(1, -1),
                _uq_weight(w_mla_uq[j]), wuk, perm)
            ob_p = _attn_prompt(qcat, kcat, wuv, bp, tp)
            ob_s = _attn_sample(qcat, kcat, cache_mla_latent[j], cache_mla_krope[j], wuv, n_p)
            lat_p.append(lat[:n_p].reshape(bp, tp, KV_LORA))
            lat_s.append(lat[n_p:].reshape(bs, ts, KV_LORA))
            kpe_p.append(kpe[:n_p].reshape(bp, tp, ROPE))
            kpe_s.append(kpe[n_p:].reshape(bs, ts, ROPE))
            wo = w_out_even[j].astype(BF16)
            x = _outproj(x, modg, l, [oa_p, ob_p], [oa_s, ob_s], [wo[:H_A * DV_A], wo[H_A * DV_A:]], group)
        else:
            zq, zf, zi, zg = _inproj(x, norm1_g[l], modg, l, w_in_odd[j].astype(BF16),
                                     (H_C * DK_C,) * 2 + (H_C * DV_C,) * 2, group)
            extras = (lb_all[l].reshape(1, -1),)
            gn = hgrn_norm_g[j].reshape(1, DV_C)
            oc_p, sp = _recurrence("hgrn", zq, zf, zi, zg, extras, gn, None, heads=H_C, nseq=bp,
                                   seqlen=tp, chunk=CHUNK, row0=0)
            oc_s, ss = _recurrence("hgrn", zq, zf, zi, zg, extras, gn, state_hgrn[j], heads=H_C, nseq=bs,
                                   seqlen=ts, chunk=ts, row0=n_p)
            hgrn_p.append(sp)
            hgrn_s.append(ss)
            x = _outproj(x, modg, l, [oc_p], [oc_s], [w_out_odd[j].astype(BF16)], group)

        wr = _pad_cols(w_router[l], LANE)
        wr_hi = wr.astype(BF16)
        wr_lo = (wr - wr_hi.astype(F32)).astype(BF16)
        br = _pad_cols(b_router[l].reshape(1, -1), LANE)
        h2, topi, gate = _router(x, norm2_g[l], modg, l, wr_hi, wr_lo, br, group, nexp)
        rank, cnt = _rank(topi)
        counts = cnt[0, :nexp].astype(I32)
        padded = (counts + MOE_BLOCK - 1) // MOE_BLOCK * MOE_BLOCK
        pends = jnp.cumsum(padded)
        pstarts = _pad_cols((pends - padded).astype(F32).reshape(1, -1), LANE)
        dest = _dest(topi, rank, pstarts)
        dest2d = dest[:, :TOP_K].reshape(n * TOP_K // LANE, LANE)
        bexp = jnp.clip(jnp.sum(pends[None, :] <= (jnp.arange(nblocks) * MOE_BLOCK)[:, None], axis=1),
                        0, nexp - 1).astype(I32)
        nused = (pends[-1:] // MOE_BLOCK).astype(I32)
        padinfo = jnp.stack([_pad_cols(pends, LANE), _pad_cols(padded - counts, LANE),
                             _pad_cols(nused, LANE)]).astype(I32)
        xs = _dispatch(dest2d, padinfo, h2, rows, nexp)
        y = _experts(bexp, nused, xs, w_e1, b_e1.reshape(depth, nexp, 1, -1), w_e2,
                     b_e2.reshape(depth, nexp, 1, -1), l)
        x = _combine(dest2d, gate, x, modg, l, y, group)

    yo = _final_norm(x, final_norm_g)
    return (yo[:n_p].reshape(bp, tp, d), yo[n_p:].reshape(bs, ts, d),
            jnp.stack(lat_p), jnp.stack(kpe_p), jnp.stack(gla_p), jnp.stack(hgrn_p),
            jnp.stack(lat_s), jnp.stack(kpe_s), jnp.stack(gla_s), jnp.stack(hgrn_s))
```

```python
import functools
import math

import jax
import jax.numpy as jnp
from jax import lax
from jax.experimental import pallas as pl
from jax.experimental.pallas import tpu as pltpu

F32 = jnp.float32
BF16 = jnp.bfloat16
I32 = jnp.int32

EPS = 1e-6
CHUNK = 64
LANE = 128

H_A, DK_A, DV_A, GATE_RANK, GATE_TAU = 4, 64, 128, 16, 16.0
H_B, Q_LORA, KV_LORA, NOPE, ROPE, V_B = 4, 384, 256, 128, 64, 128
ROPE_BASE = 10000.0
MLA_SCALE = (NOPE + ROPE) ** -0.5
KCAT = KV_LORA + LANE
H_C, DK_C, DV_C = 8, 128, 128
TOP_K = 4
SWIGLU_LIMIT = 7.0
SWIGLU_ALPHA = 1.702
MOE_BLOCK = 256
SLOT_ROWS = LANE // TOP_K
SLOT_SHIFT = SLOT_ROWS.bit_length() - 1
EXP_CLAMP = 80.0

VMEM_LIMIT = 56 * 1024 * 1024


def _cparams(sem, vmem=None):
    return pltpu.CompilerParams(dimension_semantics=sem, vmem_limit_bytes=vmem)


def _pick_tile(n, pref):
    t = pref
    while n % t:
        t //= 2
    return t


def _rms(x, g):
    return x * lax.rsqrt(jnp.mean(x * x, axis=-1, keepdims=True) + EPS) * g


def _group_affine(x, scale, shift, group):
    tm, d = x.shape
    x3 = x.reshape(tm // group, group, d)
    if scale is not None:
        x3 = x3 * scale
    if shift is not None:
        x3 = x3 + shift
    return x3.reshape(tm, d)


def _split3(x):
    hi = x.astype(BF16)
    r1 = x - hi.astype(F32)
    mid = r1.astype(BF16)
    lo = (r1 - mid.astype(F32)).astype(BF16)
    return hi, mid, lo


def _dot(a, b):
    return jnp.dot(a, b, preferred_element_type=F32)


def _dot_nt(a, b):
    return lax.dot_general(a, b, (((1,), (1,)), ((), ())), preferred_element_type=F32)


def _dot_tn(a, b):
    return lax.dot_general(a, b, (((0,), (0,)), ((), ())), preferred_element_type=F32)


def _ada_kernel(c_ref, w_ref, b_ref, o_ref):
    c = c_ref[...]
    a = (c * jax.nn.sigmoid(c)).astype(BF16)
    o_ref[...] = _dot(a, w_ref[...].astype(BF16)) + b_ref[...]


def _ada(c_all, w_ada, b_ada):
    depth, d, n6 = w_ada.shape
    s = c_all.shape[0]
    tn = _pick_tile(n6, 1536)
    return pl.pallas_call(
        _ada_kernel,
        grid=(depth, n6 // tn),
        in_specs=[
            pl.BlockSpec((s, d), lambda l, j: (0, 0)),
            pl.BlockSpec((None, d, tn), lambda l, j: (l, 0, j)),
            pl.BlockSpec((None, 1, tn), lambda l, j: (l, 0, j)),
        ],
        out_specs=pl.BlockSpec((None, s, tn), lambda l, j: (l, 0, j)),
        out_shape=jax.ShapeDtypeStruct((depth, s, n6), F32),
        compiler_params=_cparams(("arbitrary", "arbitrary")),
        name="ada",
    )(c_all, w_ada, b_ada.reshape(depth, 1, n6))


def _inproj_kernel(x_ref, g_ref, sc_ref, sh_ref, w_ref, *out_refs, splits, group):
    h = _rms(x_ref[...], g_ref[...])
    h = _group_affine(h, 1.0 + sc_ref[...], sh_ref[...], group).astype(BF16)
    for o_ref, (c0, c1) in zip(out_refs, splits):
        o_ref[...] = _dot(h, w_ref[:, c0:c1]).astype(o_ref.dtype)


def _inproj(x, gain, modg, l, w, widths, group):
    n, d = x.shape
    tm = _pick_tile(n, 512)
    tg = tm // group
    splits, c = [], 0
    for wd in widths:
        splits.append((c, c + wd))
        c += wd
    mod_spec = lambda comp: pl.BlockSpec((None, None, tg, 1, d), lambda i: (l, comp, i, 0, 0))
    return pl.pallas_call(
        functools.partial(_inproj_kernel, splits=tuple(splits), group=group),
        grid=(n // tm,),
        in_specs=[
            pl.BlockSpec((tm, d), lambda i: (i, 0)),
            pl.BlockSpec((1, d), lambda i: (0, 0)),
            mod_spec(1),
            mod_spec(0),
            pl.BlockSpec((d, c), lambda i: (0, 0)),
        ],
        out_specs=[pl.BlockSpec((tm, wd), lambda i: (i, 0)) for wd in widths],
        out_shape=[jax.ShapeDtypeStruct((n, wd), F32) for wd in widths],
        compiler_params=_cparams(("arbitrary",), VMEM_LIMIT),
        name="inproj",
    )(x, gain.reshape(1, d), modg, modg, w)


def _rec_kernel(*refs, mode, heads, chunk, nchunks, zero_init):
    refs = list(refs)
    if mode == "gla":
        q_ref, k_ref, v_ref, r_ref, a_ref, wah_ref, wal_ref, ab_ref, gn_ref = refs[:9]
        refs = refs[9:]
    else:
        q_ref, k_ref, v_ref, r_ref, lb_ref, gn_ref = refs[:6]
        refs = refs[6:]
    s0_ref = None
    if not zero_init:
        s0_ref = refs.pop(0)
    o_ref, sout_ref = refs[:2]
    st_scr = refs[2:]
    ci = pl.program_id(1)

    @pl.when(ci == 0)
    def _():
        for h in range(heads):
            if zero_init:
                st_scr[h][...] = jnp.zeros((LANE, LANE), F32)
            else:
                st_scr[h][...] = s0_ref[0, h].T

    row = lax.broadcasted_iota(I32, (chunk, chunk), 0)
    col = lax.broadcasted_iota(I32, (chunk, chunk), 1)
    causal = row >= col
    tri = jnp.where(causal, 1.0, 0.0).astype(BF16)
    mid = chunk // 2 - 1

    if mode == "gla":
        a = a_ref[...]
        a_hi = a.astype(BF16)
        a_lo = (a - a_hi.astype(F32)).astype(BF16)

    for h in range(heads):
        sl = slice(h * LANE, (h + 1) * LANE)
        if mode == "gla":
            q = q_ref[:, sl] * (DK_A ** -0.5)
            k = k_ref[:, sl]
            alog = (_dot(a_hi, wah_ref[:, sl]) + _dot(a_lo, wah_ref[:, sl])
                    + _dot(a_hi, wal_ref[:, sl]) + ab_ref[:, sl])
            g = jax.nn.log_sigmoid(alog) * (1.0 / GATE_TAU)
        else:
            qr = q_ref[:, sl]
            q = qr * jax.nn.sigmoid(qr)
            lb = lb_ref[:, sl]
            fg = lb + (1.0 - lb) * jax.nn.sigmoid(k_ref[:, sl])
            k = 1.0 - fg
            g = jnp.log(fg)
        g_hi, g_mid, g_lo = _split3(g)
        b = _dot(tri, g_hi) + _dot(tri, g_mid) + _dot(tri, g_lo)
        b_last = b[chunk - 1:chunk, :]
        b_ref_row = b[mid:mid + 1, :]
        qt = (q * jnp.exp(jnp.minimum(b - b_ref_row, EXP_CLAMP))).astype(BF16)
        kt = (k * jnp.exp(jnp.minimum(b_ref_row - b, EXP_CLAMP))).astype(BF16)
        qs = (q * jnp.exp(b)).astype(BF16)
        ks = (k * jnp.exp(b_last - b)).astype(BF16)
        v = v_ref[:, sl].astype(BF16)
        att = jnp.where(causal, _dot_nt(qt, kt), 0.0).astype(BF16)
        st = st_scr[h][...]
        o = _dot(att, v) + _dot_nt(qs, st.astype(BF16))
        st_scr[h][...] = st * jnp.exp(b_last) + _dot_tn(v, ks)
        rg = r_ref[:, sl]
        o = _rms(o, gn_ref[...]) * (rg * jax.nn.sigmoid(rg))
        o_ref[:, sl] = o.astype(o_ref.dtype)

    @pl.when(ci == nchunks - 1)
    def _():
        for h in range(heads):
            sout_ref[0, h] = st_scr[h][...].T


def _recurrence(mode, q, k, v, r, extras, gnorm, s0, *, heads, nseq, seqlen, chunk, row0):
    cols = q.shape[1]
    nchunks = seqlen // chunk
    blk0 = row0 // chunk
    tok = lambda b, c: (blk0 + b * nchunks + c, 0)
    full = lambda a: pl.BlockSpec(a.shape, lambda b, c: (0,) * a.ndim)
    slab = pl.BlockSpec((chunk, cols), tok)
    in_specs = [slab, slab, slab, slab]
    args = [q, k, v, r]
    if mode == "gla":
        a_lr, wah, wal, ab = extras
        in_specs += [pl.BlockSpec((chunk, a_lr.shape[1]), tok), full(wah), full(wal), full(ab)]
        args += [a_lr, wah, wal, ab]
    else:
        (lb,) = extras
        in_specs += [full(lb)]
        args += [lb]
    in_specs.append(full(gnorm))
    args.append(gnorm)
    zero_init = s0 is None
    if not zero_init:
        in_specs.append(pl.BlockSpec((1, heads, LANE, LANE), lambda b, c: (b, 0, 0, 0)))
        args.append(s0)
    return pl.pallas_call(
        functools.partial(_rec_kernel, mode=mode, heads=heads, chunk=chunk, nchunks=nchunks,
                          zero_init=zero_init),
        grid=(nseq, nchunks),
        in_specs=in_specs,
        out_specs=[pl.BlockSpec((chunk, cols), lambda b, c: (b * nchunks + c, 0)),
                   pl.BlockSpec((1, heads, LANE, LANE), lambda b, c: (b, 0, 0, 0))],
        out_shape=[jax.ShapeDtypeStruct((nseq * seqlen, cols), BF16),
                   jax.ShapeDtypeStruct((nseq, heads, LANE, LANE), F32)],
        scratch_shapes=[pltpu.VMEM((LANE, LANE), F32) for _ in range(heads)],
        compiler_params=_cparams(("arbitrary", "arbitrary")),
        name="recurrence_" + mode,
    )(*args)


def _mla_pre_kernel(cq_ref, ckv_ref, kpe_ref, cosq_ref, sinq_ref, cosk_ref, sink_ref, qg_ref, kvg_ref,
                    wuq_ref, wuk_ref, perm_ref, qcat_ref, kcat_ref, lat_ref, kpeo_ref, qt_ref, latt_ref, *, tq):
    tm = cq_ref.shape[0]
    cqn = _rms(cq_ref[...], qg_ref[...]).astype(BF16)
    qf = _dot(cqn, wuq_ref[...])
    off = H_B * NOPE
    x1 = qf[:, off:off + LANE]
    x2 = qf[:, off + LANE:off + 2 * LANE]
    cq, sq = cosq_ref[...], sinq_ref[...]
    o1 = (x1 * cq - x2 * sq) * MLA_SCALE
    o2 = (x2 * cq + x1 * sq) * MLA_SCALE
    pe = _dot(o1.astype(BF16), perm_ref[0:LANE, :]) + _dot(o2.astype(BF16), perm_ref[LANE:2 * LANE, :])
    for h in range(H_B):
        ql = _dot(qf[:, h * NOPE:(h + 1) * NOPE].astype(BF16), wuk_ref[h]) * MLA_SCALE
        peh = pe[:, h * LANE:(h + 1) * LANE]
        qcat_ref[h, :, 0:KV_LORA] = ql.astype(BF16)
        qcat_ref[h, :, KV_LORA:KCAT] = peh.astype(BF16)
        for jb in range(tm // tq):
            rows = slice(jb * tq, (jb + 1) * tq)
            cols = slice(h * tq, (h + 1) * tq)
            qt_ref[jb, 0:KV_LORA, cols] = ql[rows, :].T.astype(BF16)
            qt_ref[jb, KV_LORA:KCAT, cols] = peh[rows, :].T.astype(BF16)
    latn = _rms(ckv_ref[...], kvg_ref[...])
    lat_ref[...] = latn
    latt_ref[...] = latn.T.astype(BF16)
    x = kpe_ref[...]
    half = ROPE // 2
    lane = lax.broadcasted_iota(I32, x.shape, 1)
    rot = jnp.where(lane < half, -pltpu.roll(x, LANE - half, 1), pltpu.roll(x, half, 1))
    kro = x * cosk_ref[...] + rot * sink_ref[...]
    kpeo_ref[...] = kro[:, 0:ROPE]
    kcat_ref[:, 0:KV_LORA] = latn.astype(BF16)
    kcat_ref[:, KV_LORA:KCAT] = kro.astype(BF16)


ATTN_TQ = 2 * CHUNK


def _mla_pre(zcq, zckv, zkpe, tabs, qg, kvg, wuq, wuk, perm):
    n = zcq.shape[0]
    tm = _pick_tile(n, 512)
    tq = ATTN_TQ
    tokspec = lambda wd: pl.BlockSpec((tm, wd), lambda i: (i, 0))
    full = lambda a: pl.BlockSpec(a.shape, lambda i: (0,) * a.ndim)
    return pl.pallas_call(
        functools.partial(_mla_pre_kernel, tq=tq),
        grid=(n // tm,),
        in_specs=[tokspec(Q_LORA), tokspec(KV_LORA), tokspec(LANE)] + [tokspec(LANE)] * 4
                 + [full(qg), full(kvg), full(wuq), full(wuk), full(perm)],
        out_specs=[pl.BlockSpec((H_B, tm, KCAT), lambda i: (0, i, 0)), tokspec(KCAT),
                   tokspec(KV_LORA), tokspec(ROPE),
                   pl.BlockSpec((tm // tq, KCAT, H_B * tq), lambda i: (i, 0, 0)),
                   pl.BlockSpec((KV_LORA, tm), lambda i: (0, i))],
        out_shape=[jax.ShapeDtypeStruct((H_B, n, KCAT), BF16), jax.ShapeDtypeStruct((n, KCAT), BF16),
                   jax.ShapeDtypeStruct((n, KV_LORA), F32), jax.ShapeDtypeStruct((n, ROPE), F32),
                   jax.ShapeDtypeStruct((n // tq, KCAT, H_B * tq), BF16),
                   jax.ShapeDtypeStruct((KV_LORA, n), BF16)],
        compiler_params=_cparams(("arbitrary",)),
        name="mla_pre",
    )(zcq, zckv, zkpe, *tabs, qg, kvg, wuq, wuk, perm)


def _softmax_update(s, vals, m_scr, l_scr, acc_scr):
    m_prev = m_scr[...]
    m_new = jnp.maximum(m_prev, jnp.max(s, axis=1, keepdims=True))
    alpha = jnp.exp(m_prev - m_new)
    p = jnp.exp(s - m_new)
    l_scr[...] = alpha * l_scr[...] + jnp.sum(p, axis=1, keepdims=True)
    acc_scr[...] = alpha * acc_scr[...] + _dot(p.astype(BF16), vals)
    m_scr[...] = m_new


def _softmax_init(m_scr, l_scr, acc_scr):
    m_scr[...] = jnp.full(m_scr.shape, -jnp.inf, F32)
    l_scr[...] = jnp.zeros(l_scr.shape, F32)
    acc_scr[...] = jnp.zeros(acc_scr.shape, F32)


def _attn_finish(o_ref, wuv_ref, l_scr, acc_scr, tq):
    inv = 1.0 / l_scr[...]
    for h in range(H_B):
        rows = slice(h * tq, (h + 1) * tq)
        oh = (acc_scr[rows, :] * inv[rows, :]).astype(BF16)
        o_ref[:, h * V_B:(h + 1) * V_B] = _dot(oh, wuv_ref[h]).astype(o_ref.dtype)


def _attn_prompt_kernel(qt_ref, k_ref, latt_ref, wuvt_ref, o_ref, m_scr, l_scr, acc_scr, p_scr, *, tq, tk):
    qi = pl.program_id(1)
    cols = H_B * tq
    qt = qt_ref[...]
    m_scr[...] = jnp.full(m_scr.shape, -jnp.inf, F32)
    l_scr[...] = jnp.zeros(l_scr.shape, F32)
    acc_scr[...] = jnp.zeros(acc_scr.shape, F32)

    def block(start, masked):
        s = _dot(k_ref[pl.ds(start, tk), :], qt)
        if masked:
            tok = qi * tq + (lax.broadcasted_iota(I32, (1, cols), 1) & (tq - 1))
            limit = (tok // CHUNK + 1) * CHUNK
            key = start + lax.broadcasted_iota(I32, (tk, 1), 0)
            s = jnp.where(key < limit, s, -jnp.inf)
        m_prev = m_scr[...]
        m_new = jnp.maximum(m_prev, jnp.max(s, axis=0, keepdims=True))
        alpha = jnp.exp(m_prev - m_new)
        p = jnp.exp(s - m_new)
        l_scr[...] = alpha * l_scr[...] + jnp.sum(p, axis=0, keepdims=True)
        m_scr[...] = m_new
        p_scr[...] = p.astype(BF16)
        acc_scr[...] = alpha * acc_scr[...] + _dot(latt_ref[:, pl.ds(start, tk)], p_scr[...])

    per_pair = 2 * tk // tq
    npairs = qi // per_pair

    def pair(j, carry):
        base = pl.multiple_of(j * (2 * tk), 2 * tk)
        block(base, False)
        block(base + tk, False)
        return carry

    lax.fori_loop(0, npairs, pair, 0)
    base = pl.multiple_of(npairs * (2 * tk), 2 * tk)
    block(base, True)

    @pl.when((qi + 1) * tq > npairs * (2 * tk) + tk)
    def _():
        block(base + tk, True)

    ot = (acc_scr[...] * (1.0 / l_scr[...])).astype(BF16)
    for h in range(H_B):
        oh = _dot(wuvt_ref[h], ot[:, h * tq:(h + 1) * tq])
        o_ref[:, h * V_B:(h + 1) * V_B] = oh.T.astype(o_ref.dtype)


def _attn_prompt(qt, kcat, latt, wuvt, nseq, seqlen):
    n = nseq * seqlen
    tq = ATTN_TQ
    tk = 256
    assert seqlen % tk == 0 and (2 * tk) % tq == 0
    nq = seqlen // tq
    return pl.pallas_call(
        functools.partial(_attn_prompt_kernel, tq=tq, tk=tk),
        grid=(nseq, nq),
        in_specs=[
            pl.BlockSpec((None, KCAT, H_B * tq), lambda b, qi: (b * nq + qi, 0, 0)),
            pl.BlockSpec((seqlen, KCAT), lambda b, qi: (b, 0)),
            pl.BlockSpec((KV_LORA, seqlen), lambda b, qi: (0, b)),
            pl.BlockSpec(wuvt.shape, lambda b, qi: (0, 0, 0)),
        ],
        out_specs=pl.BlockSpec((tq, H_B * V_B), lambda b, qi: (b * nq + qi, 0)),
        out_shape=jax.ShapeDtypeStruct((n, H_B * V_B), BF16),
        scratch_shapes=[pltpu.VMEM((1, H_B * tq), F32), pltpu.VMEM((1, H_B * tq), F32),
                        pltpu.VMEM((KV_LORA, H_B * tq), F32), pltpu.VMEM((tk, H_B * tq), BF16)],
        compiler_params=_cparams(("arbitrary", "arbitrary")),
        name="attn_prompt",
    )(qt, kcat, latt, wuvt)


def _attn_sample_kernel(q_ref, plat_ref, pkpe_ref, knew_ref, wuv_ref, o_ref,
                        m_scr, l_scr, acc_scr, *, tq, nkp):
    ki = pl.program_id(1)

    @pl.when(ki == 0)
    def _():
        _softmax_init(m_scr, l_scr, acc_scr)

    q = q_ref[...].reshape(H_B * tq, KCAT)
    lat = plat_ref[...].astype(BF16)
    kpe = pkpe_ref[...].astype(BF16)
    s = _dot_nt(q[:, 0:KV_LORA], lat) + _dot_nt(q[:, KV_LORA:KV_LORA + ROPE], kpe)
    _softmax_update(s, lat, m_scr, l_scr, acc_scr)

    @pl.when(ki == nkp - 1)
    def _():
        kn = knew_ref[...]
        _softmax_update(_dot_nt(q, kn), kn[:, 0:KV_LORA], m_scr, l_scr, acc_scr)
        _attn_finish(o_ref, wuv_ref, l_scr, acc_scr, tq)


def _attn_sample(qcat, kcat, past_lat, past_kpe, wuv, row0):
    nseq, past, _ = past_lat.shape
    tq = (kcat.shape[0] - row0) // nseq
    tkp = _pick_tile(past, 1024)
    nkp = past // tkp
    blk0 = row0 // tq
    return pl.pallas_call(
        functools.partial(_attn_sample_kernel, tq=tq, nkp=nkp),
        grid=(nseq, nkp),
        in_specs=[
            pl.BlockSpec((H_B, tq, KCAT), lambda b, ki: (0, blk0 + b, 0)),
            pl.BlockSpec((None, tkp, KV_LORA), lambda b, ki: (b, ki, 0)),
            pl.BlockSpec((None, tkp, ROPE), lambda b, ki: (b, ki, 0)),
            pl.BlockSpec((tq, KCAT), lambda b, ki: (blk0 + b, 0)),
            pl.BlockSpec(wuv.shape, lambda b, ki: (0, 0, 0)),
        ],
        out_specs=pl.BlockSpec((tq, H_B * V_B), lambda b, ki: (b, 0)),
        out_shape=jax.ShapeDtypeStruct((nseq * tq, H_B * V_B), BF16),
        scratch_shapes=[pltpu.VMEM((H_B * tq, 1), F32), pltpu.VMEM((H_B * tq, 1), F32),
                        pltpu.VMEM((H_B * tq, KV_LORA), F32)],
        compiler_params=_cparams(("arbitrary", "arbitrary")),
        name="attn_sample",
    )(qcat, past_lat, past_kpe, kcat, wuv)


def _outproj_kernel(*refs, nlhs, group, ptiles):
    x_ref, g_ref = refs[0], refs[1]
    lhs_p = refs[2:2 + nlhs]
    lhs_s = refs[2 + nlhs:2 + 2 * nlhs]
    ws = refs[2 + 2 * nlhs:2 + 3 * nlhs]
    o_ref = refs[2 + 3 * nlhs]
    is_prompt = pl.program_id(0) < ptiles
    acc = None
    for ap, asm, w in zip(lhs_p, lhs_s, ws):
        a = jnp.where(is_prompt, ap[...], asm[...])
        t = _dot(a, w[...])
        acc = t if acc is None else acc + t
    o_ref[...] = x_ref[...] + _group_affine(acc, g_ref[...], None, group)


def _outproj(x, modg, l, lhs_p, lhs_s, ws, group):
    n, d = x.shape
    n_p, n_s = lhs_p[0].shape[0], lhs_s[0].shape[0]
    tm = _pick_tile(math.gcd(n_p, n_s), 512)
    tg = tm // group
    ptiles = n_p // tm
    return pl.pallas_call(
        functools.partial(_outproj_kernel, nlhs=len(ws), group=group, ptiles=ptiles),
        grid=(n // tm,),
        in_specs=[pl.BlockSpec((tm, d), lambda i: (i, 0)),
                  pl.BlockSpec((None, None, tg, 1, d), lambda i: (l, 2, i, 0, 0))]
                 + [pl.BlockSpec((tm, a.shape[1]), lambda i: (jnp.minimum(i, ptiles - 1), 0)) for a in lhs_p]
                 + [pl.BlockSpec((tm, a.shape[1]), lambda i: (jnp.maximum(i - ptiles, 0), 0)) for a in lhs_s]
                 + [pl.BlockSpec(w.shape, lambda i: (0, 0)) for w in ws],
        out_specs=pl.BlockSpec((tm, d), lambda i: (i, 0)),
        out_shape=jax.ShapeDtypeStruct((n, d), F32),
        compiler_params=_cparams(("arbitrary",)),
        name="outproj",
    )(x, modg, *lhs_p, *lhs_s, *ws)


def _router_kernel(x_ref, g_ref, sc_ref, sh_ref, wh_ref, wl_ref, br_ref, h_ref, ti_ref, gt_ref,
                   *, group, nexp):
    h = _rms(x_ref[...], g_ref[...])
    h = _group_affine(h, 1.0 + sc_ref[...], sh_ref[...], group)
    h_ref[...] = h
    h_hi = h.astype(BF16)
    h_lo = (h - h_hi.astype(F32)).astype(BF16)
    logits = _dot(h_hi, wh_ref[...]) + _dot(h_lo, wh_ref[...]) + _dot(h_hi, wl_ref[...]) + br_ref[...]
    lane = lax.broadcasted_iota(I32, logits.shape, 1)
    lane_f = lane.astype(F32)
    cur = jnp.where(lane < nexp, logits, -jnp.inf)
    tops, idxs = [], []
    for _ in range(TOP_K):
        m = jnp.max(cur, axis=1, keepdims=True)
        i = jnp.min(jnp.where(cur == m, lane_f, float(LANE)), axis=1, keepdims=True)
        cur = jnp.where(lane_f == i, -jnp.inf, cur)
        tops.append(m)
        idxs.append(i.astype(I32))
    es = [jnp.exp(t - tops[0]) for t in tops]
    inv = 1.0 / (es[0] + es[1] + es[2] + es[3])
    ti = jnp.zeros(logits.shape, I32)
    gt = jnp.zeros(logits.shape, F32)
    for k in range(TOP_K):
        ti = jnp.where(lane == k, idxs[k], ti)
        gt = jnp.where(lane == k, es[k] * inv, gt)
    ti_ref[...] = ti
    gt_ref[...] = gt


def _router(x, gain, modg, l, wr_hi, wr_lo, br, group, nexp):
    n, d = x.shape
    tm = _pick_tile(n, 512)
    tg = tm // group
    mod_spec = lambda comp: pl.BlockSpec((None, None, tg, 1, d), lambda i: (l, comp, i, 0, 0))
    full = lambda a: pl.BlockSpec(a.shape, lambda i: (0,) * a.ndim)
    tok = lambda wd: pl.BlockSpec((tm, wd), lambda i: (i, 0))
    return pl.pallas_call(
        functools.partial(_router_kernel, group=group, nexp=nexp),
        grid=(n // tm,),
        in_specs=[tok(d), pl.BlockSpec((1, d), lambda i: (0, 0)), mod_spec(4), mod_spec(3),
                  full(wr_hi), full(wr_lo), full(br)],
        out_specs=[tok(d), tok(LANE), tok(LANE)],
        out_shape=[jax.ShapeDtypeStruct((n, d), F32), jax.ShapeDtypeStruct((n, LANE), I32),
                   jax.ShapeDtypeStruct((n, LANE), F32)],
        compiler_params=_cparams(("arbitrary",)),
        name="router",
    )(x, gain.reshape(1, d), modg, modg, wr_hi, wr_lo, br)


def _rank_kernel(ti_ref, rank_ref, cnt_ref, carry_scr, *, tm):
    i = pl.program_id(0)

    @pl.when(i == 0)
    def _():
        carry_scr[...] = jnp.zeros(carry_scr.shape, F32)

    ti = ti_ref[...]
    lane = lax.broadcasted_iota(I32, ti.shape, 1)
    sel = [lane == ti[:, k:k + 1] for k in range(TOP_K)]
    hot = jnp.zeros(ti.shape, F32)
    for s in sel:
        hot = hot + jnp.where(s, 1.0, 0.0)
    row = lax.broadcasted_iota(I32, (tm, tm), 0)
    col = lax.broadcasted_iota(I32, (tm, tm), 1)
    strict = jnp.where(row > col, 1.0, 0.0).astype(BF16)
    before = _dot(strict, hot.astype(BF16)) + carry_scr[0:1, :]
    rank = jnp.zeros(ti.shape, F32)
    for k in range(TOP_K):
        rk = jnp.sum(jnp.where(sel[k], before, 0.0), axis=1, keepdims=True)
        rank = jnp.where(lane == k, rk, rank)
    rank_ref[...] = rank.astype(I32)
    carry_scr[...] = carry_scr[...] + jnp.sum(hot, axis=0, keepdims=True)
    cnt_ref[...] = carry_scr[...]


def _rank(topi):
    n = topi.shape[0]
    tm = _pick_tile(n, 512)
    return pl.pallas_call(
        functools.partial(_rank_kernel, tm=tm),
        grid=(n // tm,),
        in_specs=[pl.BlockSpec((tm, LANE), lambda i: (i, 0))],
        out_specs=[pl.BlockSpec((tm, LANE), lambda i: (i, 0)), pl.BlockSpec((8, LANE), lambda i: (0, 0))],
        out_shape=[jax.ShapeDtypeStruct((n, LANE), I32), jax.ShapeDtypeStruct((8, LANE), F32)],
        scratch_shapes=[pltpu.VMEM((8, LANE), F32)],
        compiler_params=_cparams(("arbitrary",)),
        name="rank",
    )(topi)


def _dest_kernel(ti_ref, rank_ref, ps_ref, d_ref):
    ti = ti_ref[...]
    lane = lax.broadcasted_iota(I32, ti.shape, 1)
    ps = ps_ref[...]
    dest = rank_ref[...]
    for k in range(TOP_K):
        base = jnp.sum(jnp.where(lane == ti[:, k:k + 1], ps, 0.0), axis=1, keepdims=True).astype(I32)
        dest = dest + jnp.where(lane == k, base, 0)
    d_ref[...] = dest


def _dest(topi, rank, pstarts_row):
    n = topi.shape[0]
    tm = _pick_tile(n, 512)
    tok = pl.BlockSpec((tm, LANE), lambda i: (i, 0))
    return pl.pallas_call(
        _dest_kernel,
        grid=(n // tm,),
        in_specs=[tok, tok, pl.BlockSpec((1, LANE), lambda i: (0, 0))],
        out_specs=tok,
        out_shape=jax.ShapeDtypeStruct((n, LANE), I32),
        compiler_params=_cparams(("arbitrary",)),
        name="dest",
    )(topi, rank, pstarts_row)


def _dispatch_kernel(dest_ref, pad_ref, h_ref, xs_ref, zbuf, sem, zsem, *, tm, nexp, nblocks):
    def issue(r, carry):
        for k in range(TOP_K):
            d = dest_ref[lax.shift_right_logical(r, SLOT_SHIFT), lax.bitwise_and(r, SLOT_ROWS - 1) * TOP_K + k]
            pltpu.make_async_copy(h_ref.at[pl.ds(r, 1)], xs_ref.at[pl.ds(d, 1)], sem).start()
        return carry

    lax.fori_loop(0, tm, issue, 0)

    @pl.when(pl.program_id(0) == pl.num_programs(0) - 1)
    def _():
        zbuf[...] = jnp.zeros(zbuf.shape, F32)

        def fill_expert(e, carry):
            end = pad_ref[0, e]
            npad = pad_ref[1, e]
            p = MOE_BLOCK // 2
            while p >= 1:
                bit = npad & p
                end = end - bit

                @pl.when(bit != 0)
                def _(end=end, p=p):
                    if p >= 8:
                        start = pl.multiple_of(end, 8)
                        pltpu.make_async_copy(zbuf.at[pl.ds(0, p)], xs_ref.at[pl.ds(start, p)], zsem).start()
                    else:
                        for i in range(p):
                            pltpu.make_async_copy(zbuf.at[pl.ds(0, 1)], xs_ref.at[pl.ds(end + i, 1)],
                                                  zsem).start()

                p //= 2
            return carry

        lax.fori_loop(0, nexp, fill_expert, 0)
        nused = pad_ref[2, 0]

        def fill_tail(b, carry):
            @pl.when(b >= nused)
            def _():
                start = pl.multiple_of(b * MOE_BLOCK, MOE_BLOCK)
                pltpu.make_async_copy(zbuf, xs_ref.at[pl.ds(start, MOE_BLOCK)], zsem).start()

            return carry

        lax.fori_loop(0, nblocks, fill_tail, 0)
        for _ in range(nexp):
            pltpu.make_async_copy(zbuf, xs_ref.at[pl.ds(0, MOE_BLOCK)], zsem).wait()

    for k in range(TOP_K):
        pltpu.make_async_copy(h_ref, xs_ref.at[pl.ds(0, tm)], sem).wait()


def _dispatch(dest2d, padinfo, h, rows, nexp):
    n, d = h.shape
    tm = _pick_tile(n, 256)
    tr = tm * TOP_K // LANE
    return pl.pallas_call(
        functools.partial(_dispatch_kernel, tm=tm, nexp=nexp, nblocks=rows // MOE_BLOCK),
        grid=(n // tm,),
        in_specs=[pl.BlockSpec((tr, LANE), lambda i: (i, 0), memory_space=pltpu.SMEM),
                  pl.BlockSpec(memory_space=pltpu.SMEM),
                  pl.BlockSpec((tm, d), lambda i: (i, 0))],
        out_specs=pl.BlockSpec(memory_space=pl.ANY),
        out_shape=jax.ShapeDtypeStruct((rows, d), F32),
        scratch_shapes=[pltpu.VMEM((MOE_BLOCK, d), F32), pltpu.SemaphoreType.DMA(()),
                        pltpu.SemaphoreType.DMA(())],
        compiler_params=pltpu.CompilerParams(dimension_semantics=("arbitrary",), has_side_effects=True),
        name="dispatch",
    )(dest2d, padinfo, h)


def _experts_kernel(bexp_ref, nused_ref, xs_ref, w1_ref, b1_ref, w2_ref, b2_ref, y_ref,
                    w1b, w2b, act_scr, prev_scr, *, dff, dm):
    i = pl.program_id(0)
    nu = nused_ref[0]
    e = bexp_ref[jnp.minimum(i, nu - 1)]

    @pl.when(i == 0)
    def _():
        prev_scr[0] = -1

    @pl.when(i >= nu)
    def _():
        y_ref[...] = jnp.zeros(y_ref.shape, F32)

    @pl.when(i < nu)
    def _():
        @pl.when(e != prev_scr[0])
        def _():
            rows = 128

            def cast1(c, carry):
                r0 = pl.multiple_of(c * rows, rows)
                w1b[pl.ds(r0, rows), :] = w1_ref[pl.ds(r0, rows), :].astype(BF16)
                return carry

            def cast2(c, carry):
                r0 = pl.multiple_of(c * rows, rows)
                w2b[pl.ds(r0, rows), :] = w2_ref[pl.ds(r0, rows), :].astype(BF16)
                return carry

            lax.fori_loop(0, dm // rows, cast1, 0)
            lax.fori_loop(0, dff // rows, cast2, 0)
            prev_scr[0] = e

        x = xs_ref[...].astype(BF16)
        cw = 256
        for c in range(dff // cw):
            gt = _dot(x, w1b[:, c * cw:(c + 1) * cw]) + b1_ref[:, c * cw:(c + 1) * cw]
            up = _dot(x, w1b[:, dff + c * cw:dff + (c + 1) * cw]) + b1_ref[:, dff + c * cw:dff + (c + 1) * cw]
            gt = jnp.minimum(gt, SWIGLU_LIMIT)
            up = jnp.clip(up, -SWIGLU_LIMIT, SWIGLU_LIMIT)
            act = gt * jax.nn.sigmoid(gt * SWIGLU_ALPHA) * (up + 1.0)
            act_scr[:, c * cw:(c + 1) * cw] = act.astype(BF16)
        a = act_scr[...]
        for c in range(dm // cw):
            y_ref[:, c * cw:(c + 1) * cw] = _dot(a, w2b[:, c * cw:(c + 1) * cw]) + b2_ref[:, c * cw:(c + 1) * cw]


def _experts(bexp, nused, xs, w1, b1, w2, b2, l):
    rows, dm = xs.shape
    dff = w2.shape[2]
    nb = rows // MOE_BLOCK
    blk = lambda i, be, nu: (jnp.minimum(i, nu[0] - 1), 0)
    wsel = lambda i, be, nu: (l, be[jnp.minimum(i, nu[0] - 1)], 0, 0)
    grid_spec = pltpu.PrefetchScalarGridSpec(
        num_scalar_prefetch=2,
        grid=(nb,),
        in_specs=[
            pl.BlockSpec((MOE_BLOCK, dm), blk),
            pl.BlockSpec((None, None, dm, 2 * dff), wsel),
            pl.BlockSpec((None, None, 1, 2 * dff), wsel),
            pl.BlockSpec((None, None, dff, dm), wsel),
            pl.BlockSpec((None, None, 1, dm), wsel),
        ],
        out_specs=pl.BlockSpec((MOE_BLOCK, dm), lambda i, be, nu: (i, 0)),
        scratch_shapes=[pltpu.VMEM((dm, 2 * dff), BF16), pltpu.VMEM((dff, dm), BF16),
                        pltpu.VMEM((MOE_BLOCK, dff), BF16), pltpu.SMEM((1,), I32)],
    )
    return pl.pallas_call(
        functools.partial(_experts_kernel, dff=dff, dm=dm),
        grid_spec=grid_spec,
        out_shape=jax.ShapeDtypeStruct((rows, dm), F32),
        compiler_params=_cparams(("arbitrary",), VMEM_LIMIT),
        name="experts",
    )(bexp, nused, xs, w1, b1, w2, b2)


def _combine_kernel(dest_ref, gate_ref, x_ref, g_ref, y_ref, o_ref, buf, sem, *, tm, group):
    def issue(r, carry):
        for k in range(TOP_K):
            d = dest_ref[lax.shift_right_logical(r, SLOT_SHIFT), lax.bitwise_and(r, SLOT_ROWS - 1) * TOP_K + k]
            pltpu.make_async_copy(y_ref.at[pl.ds(d, 1)], buf.at[k, pl.ds(r, 1)], sem).start()
        return carry

    lax.fori_loop(0, tm, issue, 0)
    for k in range(TOP_K):
        pltpu.make_async_copy(y_ref.at[pl.ds(0, tm)], buf.at[k], sem).wait()
    gate = gate_ref[...]
    moe = gate[:, 0:1] * buf[0]
    for k in range(1, TOP_K):
        moe = moe + gate[:, k:k + 1] * buf[k]
    o_ref[...] = x_ref[...] + _group_affine(moe, g_ref[...], None, group)


def _combine(dest2d, gate, x, modg, l, y, group):
    n, d = x.shape
    tm = _pick_tile(n, 256)
    tr = tm * TOP_K // LANE
    tg = tm // group
    return pl.pallas_call(
        functools.partial(_combine_kernel, tm=tm, group=group),
        grid=(n // tm,),
        in_specs=[pl.BlockSpec((tr, LANE), lambda i: (i, 0), memory_space=pltpu.SMEM),
                  pl.BlockSpec((tm, LANE), lambda i: (i, 0)),
                  pl.BlockSpec((tm, d), lambda i: (i, 0)),
                  pl.BlockSpec((None, None, tg, 1, d), lambda i: (l, 5, i, 0, 0)),
                  pl.BlockSpec(memory_space=pl.ANY)],
        out_specs=pl.BlockSpec((tm, d), lambda i: (i, 0)),
        out_shape=jax.ShapeDtypeStruct((n, d), F32),
        scratch_shapes=[pltpu.VMEM((TOP_K, tm, d), F32), pltpu.SemaphoreType.DMA(())],
        compiler_params=_cparams(("arbitrary",)),
        name="combine",
    )(dest2d, gate, x, modg, y)


def _final_norm_kernel(x_ref, g_ref, o_ref):
    o_ref[...] = _rms(x_ref[...], g_ref[...])


def _final_norm(x, g):
    n, d = x.shape
    tm = _pick_tile(n, 512)
    return pl.pallas_call(
        _final_norm_kernel,
        grid=(n // tm,),
        in_specs=[pl.BlockSpec((tm, d), lambda i: (i, 0)), pl.BlockSpec((1, d), lambda i: (0, 0))],
        out_specs=pl.BlockSpec((tm, d), lambda i: (i, 0)),
        out_shape=jax.ShapeDtypeStruct((n, d), F32),
        compiler_params=_cparams(("arbitrary",)),
        name="final_norm",
    )(x, g.reshape(1, d))


def _pad_heads(w, heads, dk):
    lead = w.shape[:-1]
    w = w.reshape(*lead, heads, dk)
    w = jnp.pad(w, [(0, 0)] * len(lead) + [(0, 0), (0, LANE - dk)])
    return w.reshape(*lead, heads * LANE)


def _pad_cols(w, width):
    return jnp.pad(w, [(0, 0)] * (w.ndim - 1) + [(0, width - w.shape[-1])])


EVEN_WIDTHS = (H_A * LANE, H_A * LANE, H_A * DV_A, H_A * DV_A, LANE, Q_LORA, KV_LORA, LANE)


def _even_weight(w):
    a_qk, a_v = H_A * DK_A, H_A * DV_A
    c = [0, a_qk, 2 * a_qk, 2 * a_qk + a_v, 2 * a_qk + 2 * a_v, 2 * a_qk + 2 * a_v + GATE_RANK]
    c.append(c[-1] + Q_LORA)
    c.append(c[-1] + KV_LORA)
    c.append(c[-1] + ROPE)
    parts = [
        _pad_heads(w[:, c[0]:c[1]], H_A, DK_A),
        _pad_heads(w[:, c[1]:c[2]], H_A, DK_A),
        w[:, c[2]:c[3]],
        w[:, c[3]:c[4]],
        _pad_cols(w[:, c[4]:c[5]], LANE),
        w[:, c[5]:c[6]],
        w[:, c[6]:c[7]],
        _pad_cols(w[:, c[7]:c[8]], LANE),
    ]
    return jnp.concatenate(parts, axis=1).astype(BF16)


def _uq_weight(w):
    w = w.reshape(Q_LORA, H_B, NOPE + ROPE)
    half = ROPE // 2
    nope = w[:, :, :NOPE].reshape(Q_LORA, H_B * NOPE)
    r1 = w[:, :, NOPE:NOPE + half].reshape(Q_LORA, H_B * half)
    r2 = w[:, :, NOPE + half:].reshape(Q_LORA, H_B * half)
    return jnp.concatenate([nope, r1, r2], axis=1).astype(BF16)


def _rope_perm():
    half = ROPE // 2
    r = jnp.arange(2 * LANE)
    second = r // LANE
    h = (r % LANE) // half
    i = r % half
    col = h * LANE + second * half + i
    return (col[:, None] == jnp.arange(H_B * LANE)[None, :]).astype(BF16)


def _rope_tables(pos):
    half = ROPE // 2
    inv_freq = jnp.exp(-math.log(ROPE_BASE) * jnp.arange(half, dtype=F32) / half)
    ang = pos.astype(F32)[:, None] * inv_freq[None, :]
    cos, sin = jnp.cos(ang), jnp.sin(ang)
    z = jnp.zeros((pos.shape[0], LANE - ROPE), F32)
    cosq, sinq = jnp.tile(cos, (1, LANE // half)), jnp.tile(sin, (1, LANE // half))
    cosk = jnp.concatenate([cos, cos, z], axis=1)
    sink = jnp.concatenate([sin, sin, z], axis=1)
    return cosq, sinq, cosk, sink


def kernel(x_prompt, x_sample, cache_mla_latent, cache_mla_krope, state_gla, state_hgrn, c_prompt, c_sample,
           w_ada, b_ada, norm1_g, norm2_g, w_in_even, w_gla_a2, b_gla_a, gla_norm_g, mla_q_norm_g, w_mla_uq,
           mla_kv_norm_g, w_mla_uk, w_mla_uv, w_out_even, w_in_odd, hgrn_lb, hgrn_norm_g, w_out_odd,
           w_router, b_router, w_e1, b_e1, w_e2, b_e2, final_norm_g):
    bp, tp, d = x_prompt.shape
    bs, ts, _ = x_sample.shape
    past = cache_mla_latent.shape[2]
    depth = w_ada.shape[0]
    nexp = w_router.shape[2]
    n_p, n_s = bp * tp, bs * ts
    n = n_p + n_s
    group = ts
    assert tp % group == 0 and group % 8 == 0 and tp % CHUNK == 0 and ts <= CHUNK
    assert (n * TOP_K) % MOE_BLOCK == 0
    n_even, n_odd = (depth + 1) // 2, depth // 2

    x = jnp.concatenate([x_prompt.reshape(n_p, d), x_sample.reshape(n_s, d)], axis=0)

    mod = _ada(jnp.concatenate([c_prompt, c_sample], axis=0), w_ada, b_ada)
    mod = jnp.concatenate([jnp.repeat(mod[:, :bp], tp // group, axis=1), mod[:, bp:]], axis=1)
    modg = mod.reshape(depth, n // group, 6, 1, d).transpose(0, 2, 1, 3, 4)

    pos = jnp.concatenate([jnp.tile(jnp.arange(tp), bp), jnp.tile(jnp.arange(ts) + past, bs)])
    tabs = _rope_tables(pos)
    perm = _rope_perm()

    lb_soft = jax.nn.softmax(hgrn_lb.astype(F32), axis=0)
    lb_all = jnp.cumsum(lb_soft, axis=0) - lb_soft[0]

    lat_p, kpe_p, gla_p, hgrn_p, lat_s, kpe_s, gla_s, hgrn_s = [], [], [], [], [], [], [], []
    rows = (-(-(n * TOP_K) // MOE_BLOCK)) * MOE_BLOCK + nexp * MOE_BLOCK
    nblocks = rows // MOE_BLOCK

    for l in range(depth):
        j = l // 2
        if l % 2 == 0:
            zq, zk, zv, zr, za, zcq, zckv, zkpe = _inproj(
                x, norm1_g[l], modg, l, _even_weight(w_in_even[j]), EVEN_WIDTHS, group)
            wa = _pad_heads(jnp.pad(w_gla_a2[j], ((0, LANE - GATE_RANK), (0, 0))), H_A, DK_A)
            wa_hi = wa.astype(BF16)
            wa_lo = (wa - wa_hi.astype(F32)).astype(BF16)
            ab = _pad_heads(b_gla_a[j].reshape(1, -1), H_A, DK_A)
            gn = gla_norm_g[j].reshape(1, DV_A)
            extras = (za, wa_hi, wa_lo, ab)
            oa_p, sp = _recurrence("gla", zq, zk, zv, zr, extras, gn, None, heads=H_A, nseq=bp,
                                   seqlen=tp, chunk=CHUNK, row0=0)
            s0 = jnp.pad(state_gla[j], ((0, 0), (0, 0), (0, LANE - DK_A), (0, 0)))
            oa_s, ss = _recurrence("gla", zq, zk, zv, zr, extras, gn, s0, heads=H_A, nseq=bs,
                                   seqlen=ts, chunk=ts, row0=n_p)
            gla_p.append(sp[:, :, :DK_A, :])
            gla_s.append(ss[:, :, :DK_A, :])

            wuk = w_mla_uk[j].transpose(1, 2, 0).astype(BF16)
            wuv = w_mla_uv[j].transpose(1, 0, 2).astype(BF16)
            wuvt = w_mla_uv[j].transpose(1, 2, 0).astype(BF16)
            qcat, kcat, lat, kpe, qt, latt = _mla_pre(
                zcq, zckv, zkpe, tabs, mla_q_norm_g[j].reshape(1, -1), mla_kv_norm_g[j].reshape(1, -1),
                _uq_weight(w_mla_uq[j]), wuk, perm)
            ob_p = _attn_prompt(qt, kcat, latt, wuvt, bp, tp)
            ob_s = _attn_sample(qcat, kcat, cache_mla_latent[j], cache_mla_krope[j], wuv, n_p)
            lat_p.append(lat[:n_p].reshape(bp, tp, KV_LORA))
            lat_s.append(lat[n_p:].reshape(bs, ts, KV_LORA))
            kpe_p.append(kpe[:n_p].reshape(bp, tp, ROPE))
            kpe_s.append(kpe[n_p:].reshape(bs, ts, ROPE))
            wo = w_out_even[j].astype(BF16)
            x = _outproj(x, modg, l, [oa_p, ob_p], [oa_s, ob_s], [wo[:H_A * DV_A], wo[H_A * DV_A:]], group)
        else:
            zq, zf, zi, zg = _inproj(x, norm1_g[l], modg, l, w_in_odd[j].astype(BF16),
                                     (H_C * DK_C,) * 2 + (H_C * DV_C,) * 2, group)
            extras = (lb_all[l].reshape(1, -1),)
            gn = hgrn_norm_g[j].reshape(1, DV_C)
            oc_p, sp = _recurrence("hgrn", zq, zf, zi, zg, extras, gn, None, heads=H_C, nseq=bp,
                                   seqlen=tp, chunk=CHUNK, row0=0)
            oc_s, ss = _recurrence("hgrn", zq, zf, zi, zg, extras, gn, state_hgrn[j], heads=H_C, nseq=bs,
                                   seqlen=ts, chunk=ts, row0=n_p)
            hgrn_p.append(sp)
            hgrn_s.append(ss)
            x = _outproj(x, modg, l, [oc_p], [oc_s], [w_out_odd[j].astype(BF16)], group)

        wr = _pad_cols(w_router[l], LANE)
        wr_hi = wr.astype(BF16)
        wr_lo = (wr - wr_hi.astype(F32)).astype(BF16)
        br = _pad_cols(b_router[l].reshape(1, -1), LANE)
        h2, topi, gate = _router(x, norm2_g[l], modg, l, wr_hi, wr_lo, br, group, nexp)
        rank, cnt = _rank(topi)
        counts = cnt[0, :nexp].astype(I32)
        padded = (counts + MOE_BLOCK - 1) // MOE_BLOCK * MOE_BLOCK
        pends = jnp.cumsum(padded)
        pstarts = _pad_cols((pends - padded).astype(F32).reshape(1, -1), LANE)
        dest = _dest(topi, rank, pstarts)
        dest2d = dest[:, :TOP_K].reshape(n * TOP_K // LANE, LANE)
        bexp = jnp.clip(jnp.sum(pends[None, :] <= (jnp.arange(nblocks) * MOE_BLOCK)[:, None], axis=1),
                        0, nexp - 1).astype(I32)
        nused = (pends[-1:] // MOE_BLOCK).astype(I32)
        padinfo = jnp.stack([_pad_cols(pends, LANE), _pad_cols(padded - counts, LANE),
                             _pad_cols(nused, LANE)]).astype(I32)
        xs = _dispatch(dest2d, padinfo, h2, rows, nexp)
        y = _experts(bexp, nused, xs, w_e1, b_e1.reshape(depth, nexp, 1, -1), w_e2,
                     b_e2.reshape(depth, nexp, 1, -1), l)
        x = _combine(dest2d, gate, x, modg, l, y, group)

    yo = _final_norm(x, final_norm_g)
    return (yo[:n_p].reshape(bp, tp, d), yo[n_p:].reshape(bs, ts, d),
            jnp.stack(lat_p), jnp.stack(kpe_p), jnp.stack(gla_p), jnp.stack(hgrn_p),
            jnp.stack(lat_s), jnp.stack(kpe_s), jnp.stack(gla_s), jnp.stack(hgrn_s))
```

```python
import functools
import math

import jax
import jax.numpy as jnp
from jax import lax
from jax.experimental import pallas as pl
from jax.experimental.pallas import tpu as pltpu

F32 = jnp.float32
BF16 = jnp.bfloat16
I32 = jnp.int32

EPS = 1e-6
CHUNK = 64
LANE = 128

H_A, DK_A, DV_A, GATE_RANK, GATE_TAU = 4, 64, 128, 16, 16.0
H_B, Q_LORA, KV_LORA, NOPE, ROPE, V_B = 4, 384, 256, 128, 64, 128
ROPE_BASE = 10000.0
MLA_SCALE = (NOPE + ROPE) ** -0.5
KCAT = KV_LORA + LANE
H_C, DK_C, DV_C = 8, 128, 128
TOP_K = 4
SWIGLU_LIMIT = 7.0
SWIGLU_ALPHA = 1.702
MOE_BLOCK = 512
ISSUE_UNROLL = 4
EXP_CLAMP = 80.0

VMEM_LIMIT = 56 * 1024 * 1024


def _cparams(sem, vmem=None):
    return pltpu.CompilerParams(dimension_semantics=sem, vmem_limit_bytes=vmem)


def _pick_tile(n, pref):
    t = pref
    while n % t:
        t //= 2
    return t


def _rms(x, g):
    return x * lax.rsqrt(jnp.mean(x * x, axis=-1, keepdims=True) + EPS) * g


def _group_affine(x, scale, shift, group):
    tm, d = x.shape
    x3 = x.reshape(tm // group, group, d)
    if scale is not None:
        x3 = x3 * scale
    if shift is not None:
        x3 = x3 + shift
    return x3.reshape(tm, d)


def _cumsum_rows(x):
    c, w = x.shape
    row = lax.broadcasted_iota(I32, (c, 1), 0)
    s = 1
    while s < c:
        if s % 8 == 0:
            shifted = jnp.concatenate([jnp.zeros((s, w), x.dtype), x[:c - s, :]], axis=0)
        else:
            shifted = jnp.where(row >= s, pltpu.roll(x, s, 0), 0.0)
        x = x + shifted
        s *= 2
    return x


def _dot(a, b):
    return jnp.dot(a, b, preferred_element_type=F32)


def _dot_nt(a, b):
    return lax.dot_general(a, b, (((1,), (1,)), ((), ())), preferred_element_type=F32)


def _dot_tn(a, b):
    return lax.dot_general(a, b, (((0,), (0,)), ((), ())), preferred_element_type=F32)


def _ada_kernel(c_ref, w_ref, b_ref, o_ref):
    c = c_ref[...]
    a = (c * jax.nn.sigmoid(c)).astype(BF16)
    o_ref[...] = _dot(a, w_ref[...].astype(BF16)) + b_ref[...]


def _ada(c_all, w_ada, b_ada):
    depth, d, n6 = w_ada.shape
    s = c_all.shape[0]
    tn = _pick_tile(n6, 1536)
    return pl.pallas_call(
        _ada_kernel,
        grid=(depth, n6 // tn),
        in_specs=[
            pl.BlockSpec((s, d), lambda l, j: (0, 0)),
            pl.BlockSpec((None, d, tn), lambda l, j: (l, 0, j)),
            pl.BlockSpec((None, 1, tn), lambda l, j: (l, 0, j)),
        ],
        out_specs=pl.BlockSpec((None, s, tn), lambda l, j: (l, 0, j)),
        out_shape=jax.ShapeDtypeStruct((depth, s, n6), F32),
        compiler_params=_cparams(("arbitrary", "arbitrary")),
        name="ada",
    )(c_all, w_ada, b_ada.reshape(depth, 1, n6))


def _inproj_kernel(x_ref, g_ref, sc_ref, sh_ref, w_ref, *out_refs, splits, group):
    h = _rms(x_ref[...], g_ref[...])
    h = _group_affine(h, 1.0 + sc_ref[...], sh_ref[...], group).astype(BF16)
    for o_ref, (c0, c1) in zip(out_refs, splits):
        o_ref[...] = _dot(h, w_ref[:, c0:c1]).astype(o_ref.dtype)


def _inproj(x, gain, modg, l, w, widths, group):
    n, d = x.shape
    tm = _pick_tile(n, 512)
    tg = tm // group
    splits, c = [], 0
    for wd in widths:
        splits.append((c, c + wd))
        c += wd
    mod_spec = lambda comp: pl.BlockSpec((None, None, tg, 1, d), lambda i: (l, comp, i, 0, 0))
    return pl.pallas_call(
        functools.partial(_inproj_kernel, splits=tuple(splits), group=group),
        grid=(n // tm,),
        in_specs=[
            pl.BlockSpec((tm, d), lambda i: (i, 0)),
            pl.BlockSpec((1, d), lambda i: (0, 0)),
            mod_spec(1),
            mod_spec(0),
            pl.BlockSpec((d, c), lambda i: (0, 0)),
        ],
        out_specs=[pl.BlockSpec((tm, wd), lambda i: (i, 0)) for wd in widths],
        out_shape=[jax.ShapeDtypeStruct((n, wd), F32) for wd in widths],
        compiler_params=_cparams(("arbitrary",), VMEM_LIMIT),
        name="inproj",
    )(x, gain.reshape(1, d), modg, modg, w)


def _rec_kernel(*refs, mode, heads, chunk, nchunks, zero_init):
    refs = list(refs)
    if mode == "gla":
        q_ref, k_ref, v_ref, r_ref, a_ref, wah_ref, wal_ref, ab_ref, gn_ref = refs[:9]
        refs = refs[9:]
    else:
        q_ref, k_ref, v_ref, r_ref, lb_ref, gn_ref = refs[:6]
        refs = refs[6:]
    s0_ref = None
    if not zero_init:
        s0_ref = refs.pop(0)
    o_ref, sout_ref = refs[:2]
    st_scr = refs[2:]
    ci = pl.program_id(1)

    @pl.when(ci == 0)
    def _():
        for h in range(heads):
            if zero_init:
                st_scr[h][...] = jnp.zeros((LANE, LANE), F32)
            else:
                st_scr[h][...] = s0_ref[0, h].T

    row = lax.broadcasted_iota(I32, (chunk, chunk), 0)
    col = lax.broadcasted_iota(I32, (chunk, chunk), 1)
    causal = row >= col
    mid = chunk // 2 - 1

    if mode == "gla":
        a = a_ref[...]
        a_hi = a.astype(BF16)
        a_lo = (a - a_hi.astype(F32)).astype(BF16)
        alog_all = (_dot(a_hi, wah_ref[...]) + _dot(a_lo, wah_ref[...]) + _dot(a_hi, wal_ref[...])
                    + ab_ref[...])
        g_all = jax.nn.log_sigmoid(alog_all) * (1.0 / GATE_TAU)
        fg_all = None
    else:
        lb = lb_ref[...]
        fg_all = lb + (1.0 - lb) * jax.nn.sigmoid(k_ref[...])
        g_all = jnp.log(fg_all)
    b_all = _cumsum_rows(g_all)

    for h in range(heads):
        sl = slice(h * LANE, (h + 1) * LANE)
        if mode == "gla":
            q = q_ref[:, sl] * (DK_A ** -0.5)
            k = k_ref[:, sl]
        else:
            qr = q_ref[:, sl]
            q = qr * jax.nn.sigmoid(qr)
            k = 1.0 - fg_all[:, sl]
        b = b_all[:, sl]
        b_last = b[chunk - 1:chunk, :]
        b_ref_row = b[mid:mid + 1, :]
        qt = (q * jnp.exp(jnp.minimum(b - b_ref_row, EXP_CLAMP))).astype(BF16)
        kt = (k * jnp.exp(jnp.minimum(b_ref_row - b, EXP_CLAMP))).astype(BF16)
        qs = (q * jnp.exp(b)).astype(BF16)
        ks = (k * jnp.exp(b_last - b)).astype(BF16)
        v = v_ref[:, sl].astype(BF16)
        att = jnp.where(causal, _dot_nt(qt, kt), 0.0).astype(BF16)
        st = st_scr[h][...]
        o = _dot(att, v) + _dot_nt(qs, st.astype(BF16))
        st_scr[h][...] = st * jnp.exp(b_last) + _dot_tn(v, ks)
        rg = r_ref[:, sl]
        o = _rms(o, gn_ref[...]) * (rg * jax.nn.sigmoid(rg))
        o_ref[:, sl] = o.astype(o_ref.dtype)

    @pl.when(ci == nchunks - 1)
    def _():
        for h in range(heads):
            sout_ref[0, h] = st_scr[h][...].T


def _recurrence(mode, q, k, v, r, extras, gnorm, s0, *, heads, nseq, seqlen, chunk, row0):
    cols = q.shape[1]
    nchunks = seqlen // chunk
    blk0 = row0 // chunk
    tok = lambda b, c: (blk0 + b * nchunks + c, 0)
    full = lambda a: pl.BlockSpec(a.shape, lambda b, c: (0,) * a.ndim)
    slab = pl.BlockSpec((chunk, cols), tok)
    in_specs = [slab, slab, slab, slab]
    args = [q, k, v, r]
    if mode == "gla":
        a_lr, wah, wal, ab = extras
        in_specs += [pl.BlockSpec((chunk, a_lr.shape[1]), tok), full(wah), full(wal), full(ab)]
        args += [a_lr, wah, wal, ab]
    else:
        (lb,) = extras
        in_specs += [full(lb)]
        args += [lb]
    in_specs.append(full(gnorm))
    args.append(gnorm)
    zero_init = s0 is None
    if not zero_init:
        s0, s0_layer = s0
        in_specs.append(pl.BlockSpec((None, 1, heads, LANE, LANE), lambda b, c: (s0_layer, b, 0, 0, 0)))
        args.append(s0)
    return pl.pallas_call(
        functools.partial(_rec_kernel, mode=mode, heads=heads, chunk=chunk, nchunks=nchunks,
                          zero_init=zero_init),
        grid=(nseq, nchunks),
        in_specs=in_specs,
        out_specs=[pl.BlockSpec((chunk, cols), lambda b, c: (b * nchunks + c, 0)),
                   pl.BlockSpec((1, heads, LANE, LANE), lambda b, c: (b, 0, 0, 0))],
        out_shape=[jax.ShapeDtypeStruct((nseq * seqlen, cols), BF16),
                   jax.ShapeDtypeStruct((nseq, heads, LANE, LANE), F32)],
        scratch_shapes=[pltpu.VMEM((LANE, LANE), F32) for _ in range(heads)],
        compiler_params=_cparams(("arbitrary", "arbitrary")),
        name="recurrence_" + mode,
    )(*args)


def _mla_pre_kernel(cq_ref, ckv_ref, kpe_ref, cosq_ref, sinq_ref, cosk_ref, sink_ref, qg_ref, kvg_ref,
                    wuq_ref, wuk_ref, perm_ref, qcat_ref, kcat_ref, lat_ref, kpeo_ref, qt_ref, latt_ref, *, tq):
    tm = cq_ref.shape[0]
    cqn = _rms(cq_ref[...], qg_ref[...]).astype(BF16)
    qf = _dot(cqn, wuq_ref[...])
    off = H_B * NOPE
    x1 = qf[:, off:off + LANE]
    x2 = qf[:, off + LANE:off + 2 * LANE]
    cq, sq = cosq_ref[...], sinq_ref[...]
    o1 = (x1 * cq - x2 * sq) * MLA_SCALE
    o2 = (x2 * cq + x1 * sq) * MLA_SCALE
    pe = _dot(o1.astype(BF16), perm_ref[0:LANE, :]) + _dot(o2.astype(BF16), perm_ref[LANE:2 * LANE, :])
    for h in range(H_B):
        ql = _dot(qf[:, h * NOPE:(h + 1) * NOPE].astype(BF16), wuk_ref[h]) * MLA_SCALE
        peh = pe[:, h * LANE:(h + 1) * LANE]
        qcat_ref[h, :, 0:KV_LORA] = ql.astype(BF16)
        qcat_ref[h, :, KV_LORA:KCAT] = peh.astype(BF16)
        for jb in range(tm // tq):
            rows = slice(jb * tq, (jb + 1) * tq)
            cols = slice(h * tq, (h + 1) * tq)
            qt_ref[jb, 0:KV_LORA, cols] = ql[rows, :].T.astype(BF16)
            qt_ref[jb, KV_LORA:KCAT, cols] = peh[rows, :].T.astype(BF16)
    latn = _rms(ckv_ref[...], kvg_ref[...])
    lat_ref[...] = latn
    latt_ref[...] = latn.T.astype(BF16)
    x = kpe_ref[...]
    half = ROPE // 2
    lane = lax.broadcasted_iota(I32, x.shape, 1)
    rot = jnp.where(lane < half, -pltpu.roll(x, LANE - half, 1), pltpu.roll(x, half, 1))
    kro = x * cosk_ref[...] + rot * sink_ref[...]
    kpeo_ref[...] = kro[:, 0:ROPE]
    kcat_ref[:, 0:KV_LORA] = latn.astype(BF16)
    kcat_ref[:, KV_LORA:KCAT] = kro.astype(BF16)


ATTN_TQ = 2 * CHUNK


def _mla_pre(zcq, zckv, zkpe, tabs, qg, kvg, wuq, wuk, perm):
    n = zcq.shape[0]
    tm = _pick_tile(n, 512)
    tq = ATTN_TQ
    tokspec = lambda wd: pl.BlockSpec((tm, wd), lambda i: (i, 0))
    full = lambda a: pl.BlockSpec(a.shape, lambda i: (0,) * a.ndim)
    return pl.pallas_call(
        functools.partial(_mla_pre_kernel, tq=tq),
        grid=(n // tm,),
        in_specs=[tokspec(Q_LORA), tokspec(KV_LORA), tokspec(LANE)] + [tokspec(LANE)] * 4
                 + [full(qg), full(kvg), full(wuq), full(wuk), full(perm)],
        out_specs=[pl.BlockSpec((H_B, tm, KCAT), lambda i: (0, i, 0)), tokspec(KCAT),
                   tokspec(KV_LORA), tokspec(ROPE),
                   pl.BlockSpec((tm // tq, KCAT, H_B * tq), lambda i: (i, 0, 0)),
                   pl.BlockSpec((KV_LORA, tm), lambda i: (0, i))],
        out_shape=[jax.ShapeDtypeStruct((H_B, n, KCAT), BF16), jax.ShapeDtypeStruct((n, KCAT), BF16),
                   jax.ShapeDtypeStruct((n, KV_LORA), F32), jax.ShapeDtypeStruct((n, ROPE), F32),
                   jax.ShapeDtypeStruct((n // tq, KCAT, H_B * tq), BF16),
                   jax.ShapeDtypeStruct((KV_LORA, n), BF16)],
        compiler_params=_cparams(("arbitrary",)),
        name="mla_pre",
    )(zcq, zckv, zkpe, *tabs, qg, kvg, wuq, wuk, perm)


def _softmax_update(s, vals, m_scr, l_scr, acc_scr):
    m_prev = m_scr[...]
    m_new = jnp.maximum(m_prev, jnp.max(s, axis=1, keepdims=True))
    alpha = jnp.exp(m_prev - m_new)
    p = jnp.exp(s - m_new)
    l_scr[...] = alpha * l_scr[...] + jnp.sum(p, axis=1, keepdims=True)
    acc_scr[...] = alpha * acc_scr[...] + _dot(p.astype(BF16), vals)
    m_scr[...] = m_new


def _softmax_init(m_scr, l_scr, acc_scr):
    m_scr[...] = jnp.full(m_scr.shape, -jnp.inf, F32)
    l_scr[...] = jnp.zeros(l_scr.shape, F32)
    acc_scr[...] = jnp.zeros(acc_scr.shape, F32)


def _attn_finish(o_ref, wuv_ref, l_scr, acc_scr, tq):
    inv = 1.0 / l_scr[...]
    for h in range(H_B):
        rows = slice(h * tq, (h + 1) * tq)
        oh = (acc_scr[rows, :] * inv[rows, :]).astype(BF16)
        o_ref[:, h * V_B:(h + 1) * V_B] = _dot(oh, wuv_ref[h]).astype(o_ref.dtype)


def _attn_prompt_kernel(qt_ref, k_ref, latt_ref, wuvt_ref, o_ref, m_scr, l_scr, acc_scr, p_scr, *, tq, tk):
    qi = pl.program_id(1)
    cols = H_B * tq
    qt = qt_ref[...]
    m_scr[...] = jnp.full(m_scr.shape, -jnp.inf, F32)
    l_scr[...] = jnp.zeros(l_scr.shape, F32)
    acc_scr[...] = jnp.zeros(acc_scr.shape, F32)

    def block(start, masked):
        s = _dot(k_ref[pl.ds(start, tk), :], qt)
        if masked:
            tok = qi * tq + (lax.broadcasted_iota(I32, (1, cols), 1) & (tq - 1))
            limit = (tok // CHUNK + 1) * CHUNK
            key = start + lax.broadcasted_iota(I32, (tk, 1), 0)
            s = jnp.where(key < limit, s, -jnp.inf)
        m_prev = m_scr[...]
        m_new = jnp.maximum(m_prev, jnp.max(s, axis=0, keepdims=True))
        alpha = jnp.exp(m_prev - m_new)
        p = jnp.exp(s - m_new)
        l_scr[...] = alpha * l_scr[...] + jnp.sum(p, axis=0, keepdims=True)
        m_scr[...] = m_new
        p_scr[...] = p.astype(BF16)
        acc_scr[...] = alpha * acc_scr[...] + _dot(latt_ref[:, pl.ds(start, tk)], p_scr[...])

    per_pair = 2 * tk // tq
    npairs = qi // per_pair

    def pair(j, carry):
        base = pl.multiple_of(j * (2 * tk), 2 * tk)
        block(base, False)
        block(base + tk, False)
        return carry

    lax.fori_loop(0, npairs, pair, 0)
    base = pl.multiple_of(npairs * (2 * tk), 2 * tk)
    block(base, True)

    @pl.when((qi + 1) * tq > npairs * (2 * tk) + tk)
    def _():
        block(base + tk, True)

    ot = (acc_scr[...] * (1.0 / l_scr[...])).astype(BF16)
    for h in range(H_B):
        oh = _dot(wuvt_ref[h], ot[:, h * tq:(h + 1) * tq])
        o_ref[:, h * V_B:(h + 1) * V_B] = oh.T.astype(o_ref.dtype)


def _attn_prompt(qt, kcat, latt, wuvt, nseq, seqlen):
    n = nseq * seqlen
    tq = ATTN_TQ
    tk = 256
    assert seqlen % tk == 0 and (2 * tk) % tq == 0
    nq = seqlen // tq
    return pl.pallas_call(
        functools.partial(_attn_prompt_kernel, tq=tq, tk=tk),
        grid=(nseq, nq),
        in_specs=[
            pl.BlockSpec((None, KCAT, H_B * tq), lambda b, qi: (b * nq + qi, 0, 0)),
            pl.BlockSpec((seqlen, KCAT), lambda b, qi: (b, 0)),
            pl.BlockSpec((KV_LORA, seqlen), lambda b, qi: (0, b)),
            pl.BlockSpec(wuvt.shape, lambda b, qi: (0, 0, 0)),
        ],
        out_specs=pl.BlockSpec((tq, H_B * V_B), lambda b, qi: (b * nq + qi, 0)),
        out_shape=jax.ShapeDtypeStruct((n, H_B * V_B), BF16),
        scratch_shapes=[pltpu.VMEM((1, H_B * tq), F32), pltpu.VMEM((1, H_B * tq), F32),
                        pltpu.VMEM((KV_LORA, H_B * tq), F32), pltpu.VMEM((tk, H_B * tq), BF16)],
        compiler_params=_cparams(("arbitrary", "arbitrary")),
        name="attn_prompt",
    )(qt, kcat, latt, wuvt)


def _attn_sample_kernel(q_ref, plat_ref, pkpe_ref, knew_ref, wuv_ref, o_ref,
                        m_scr, l_scr, acc_scr, *, tq, nkp):
    ki = pl.program_id(1)

    @pl.when(ki == 0)
    def _():
        _softmax_init(m_scr, l_scr, acc_scr)

    q = q_ref[...].reshape(H_B * tq, KCAT)
    lat = plat_ref[...].astype(BF16)
    kpe = pkpe_ref[...].astype(BF16)
    s = _dot_nt(q[:, 0:KV_LORA], lat) + _dot_nt(q[:, KV_LORA:KV_LORA + ROPE], kpe)
    _softmax_update(s, lat, m_scr, l_scr, acc_scr)

    @pl.when(ki == nkp - 1)
    def _():
        kn = knew_ref[...]
        _softmax_update(_dot_nt(q, kn), kn[:, 0:KV_LORA], m_scr, l_scr, acc_scr)
        _attn_finish(o_ref, wuv_ref, l_scr, acc_scr, tq)


def _attn_sample(qcat, kcat, past_lat, past_kpe, layer, wuv, row0):
    _, nseq, past, _ = past_lat.shape
    tq = (kcat.shape[0] - row0) // nseq
    tkp = _pick_tile(past, 1024)
    nkp = past // tkp
    blk0 = row0 // tq
    return pl.pallas_call(
        functools.partial(_attn_sample_kernel, tq=tq, nkp=nkp),
        grid=(nseq, nkp),
        in_specs=[
            pl.BlockSpec((H_B, tq, KCAT), lambda b, ki: (0, blk0 + b, 0)),
            pl.BlockSpec((None, None, tkp, KV_LORA), lambda b, ki: (layer, b, ki, 0)),
            pl.BlockSpec((None, None, tkp, ROPE), lambda b, ki: (layer, b, ki, 0)),
            pl.BlockSpec((tq, KCAT), lambda b, ki: (blk0 + b, 0)),
            pl.BlockSpec(wuv.shape, lambda b, ki: (0, 0, 0)),
        ],
        out_specs=pl.BlockSpec((tq, H_B * V_B), lambda b, ki: (b, 0)),
        out_shape=jax.ShapeDtypeStruct((nseq * tq, H_B * V_B), BF16),
        scratch_shapes=[pltpu.VMEM((H_B * tq, 1), F32), pltpu.VMEM((H_B * tq, 1), F32),
                        pltpu.VMEM((H_B * tq, KV_LORA), F32)],
        compiler_params=_cparams(("arbitrary", "arbitrary")),
        name="attn_sample",
    )(qcat, past_lat, past_kpe, kcat, wuv)


def _outproj_kernel(*refs, nlhs, group, ptiles):
    x_ref, g_ref = refs[0], refs[1]
    lhs_p = refs[2:2 + nlhs]
    lhs_s = refs[2 + nlhs:2 + 2 * nlhs]
    ws = refs[2 + 2 * nlhs:2 + 3 * nlhs]
    o_ref = refs[2 + 3 * nlhs]
    is_prompt = pl.program_id(0) < ptiles
    acc = None
    for ap, asm, w in zip(lhs_p, lhs_s, ws):
        a = jnp.where(is_prompt, ap[...], asm[...])
        t = _dot(a, w[...])
        acc = t if acc is None else acc + t
    o_ref[...] = x_ref[...] + _group_affine(acc, g_ref[...], None, group)


def _outproj(x, modg, l, lhs_p, lhs_s, ws, group):
    n, d = x.shape
    n_p, n_s = lhs_p[0].shape[0], lhs_s[0].shape[0]
    tm = _pick_tile(math.gcd(n_p, n_s), 512)
    tg = tm // group
    ptiles = n_p // tm
    return pl.pallas_call(
        functools.partial(_outproj_kernel, nlhs=len(ws), group=group, ptiles=ptiles),
        grid=(n // tm,),
        in_specs=[pl.BlockSpec((tm, d), lambda i: (i, 0)),
                  pl.BlockSpec((None, None, tg, 1, d), lambda i: (l, 2, i, 0, 0))]
                 + [pl.BlockSpec((tm, a.shape[1]), lambda i: (jnp.minimum(i, ptiles - 1), 0)) for a in lhs_p]
                 + [pl.BlockSpec((tm, a.shape[1]), lambda i: (jnp.maximum(i - ptiles, 0), 0)) for a in lhs_s]
                 + [pl.BlockSpec(w.shape, lambda i: (0, 0)) for w in ws],
        out_specs=pl.BlockSpec((tm, d), lambda i: (i, 0)),
        out_shape=jax.ShapeDtypeStruct((n, d), F32),
        compiler_params=_cparams(("arbitrary",)),
        name="outproj",
    )(x, modg, *lhs_p, *lhs_s, *ws)


def _router_kernel(x_ref, g_ref, sc_ref, sh_ref, wh_ref, wl_ref, br_ref, h_ref, ti_ref, gt_ref,
                   *, group, nexp):
    h = _rms(x_ref[...], g_ref[...])
    h = _group_affine(h, 1.0 + sc_ref[...], sh_ref[...], group)
    h_ref[...] = h
    h_hi = h.astype(BF16)
    h_lo = (h - h_hi.astype(F32)).astype(BF16)
    logits = _dot(h_hi, wh_ref[...]) + _dot(h_lo, wh_ref[...]) + _dot(h_hi, wl_ref[...]) + br_ref[...]
    lane = lax.broadcasted_iota(I32, logits.shape, 1)
    lane_f = lane.astype(F32)
    cur = jnp.where(lane < nexp, logits, -jnp.inf)
    tops, idxs = [], []
    for _ in range(TOP_K):
        m = jnp.max(cur, axis=1, keepdims=True)
        i = jnp.min(jnp.where(cur == m, lane_f, float(LANE)), axis=1, keepdims=True)
        cur = jnp.where(lane_f == i, -jnp.inf, cur)
        tops.append(m)
        idxs.append(i.astype(I32))
    es = [jnp.exp(t - tops[0]) for t in tops]
    inv = 1.0 / (es[0] + es[1] + es[2] + es[3])
    ti = jnp.zeros(logits.shape, I32)
    gt = jnp.zeros(logits.shape, F32)
    for k in range(TOP_K):
        ti = jnp.where(lane == k, idxs[k], ti)
        gt = jnp.where(lane == k, es[k] * inv, gt)
    ti_ref[...] = ti
    gt_ref[...] = gt


def _router(x, gain, modg, l, wr_hi, wr_lo, br, group, nexp):
    n, d = x.shape
    tm = _pick_tile(n, 512)
    tg = tm // group
    mod_spec = lambda comp: pl.BlockSpec((None, None, tg, 1, d), lambda i: (l, comp, i, 0, 0))
    full = lambda a: pl.BlockSpec(a.shape, lambda i: (0,) * a.ndim)
    tok = lambda wd: pl.BlockSpec((tm, wd), lambda i: (i, 0))
    return pl.pallas_call(
        functools.partial(_router_kernel, group=group, nexp=nexp),
        grid=(n // tm,),
        in_specs=[tok(d), pl.BlockSpec((1, d), lambda i: (0, 0)), mod_spec(4), mod_spec(3),
                  full(wr_hi), full(wr_lo), full(br)],
        out_specs=[tok(d), tok(LANE), tok(LANE)],
        out_shape=[jax.ShapeDtypeStruct((n, d), F32), jax.ShapeDtypeStruct((n, LANE), I32),
                   jax.ShapeDtypeStruct((n, LANE), F32)],
        compiler_params=_cparams(("arbitrary",)),
        name="router",
    )(x, gain.reshape(1, d), modg, modg, wr_hi, wr_lo, br)


def _rank_kernel(ti_ref, rank_ref, cnt_ref, carry_scr, *, tm):
    i = pl.program_id(0)

    @pl.when(i == 0)
    def _():
        carry_scr[...] = jnp.zeros(carry_scr.shape, F32)

    ti = ti_ref[...]
    lane = lax.broadcasted_iota(I32, ti.shape, 1)
    sel = [lane == ti[:, k:k + 1] for k in range(TOP_K)]
    hot = jnp.zeros(ti.shape, F32)
    for s in sel:
        hot = hot + jnp.where(s, 1.0, 0.0)
    row = lax.broadcasted_iota(I32, (tm, tm), 0)
    col = lax.broadcasted_iota(I32, (tm, tm), 1)
    strict = jnp.where(row > col, 1.0, 0.0).astype(BF16)
    before = _dot(strict, hot.astype(BF16)) + carry_scr[0:1, :]
    rank = jnp.zeros(ti.shape, F32)
    for k in range(TOP_K):
        rk = jnp.sum(jnp.where(sel[k], before, 0.0), axis=1, keepdims=True)
        rank = jnp.where(lane == k, rk, rank)
    rank_ref[...] = rank.astype(I32)
    carry_scr[...] = carry_scr[...] + jnp.sum(hot, axis=0, keepdims=True)
    cnt_ref[...] = carry_scr[...]


def _rank(topi):
    n = topi.shape[0]
    tm = _pick_tile(n, 512)
    return pl.pallas_call(
        functools.partial(_rank_kernel, tm=tm),
        grid=(n // tm,),
        in_specs=[pl.BlockSpec((tm, LANE), lambda i: (i, 0))],
        out_specs=[pl.BlockSpec((tm, LANE), lambda i: (i, 0)), pl.BlockSpec((8, LANE), lambda i: (0, 0))],
        out_shape=[jax.ShapeDtypeStruct((n, LANE), I32), jax.ShapeDtypeStruct((8, LANE), F32)],
        scratch_shapes=[pltpu.VMEM((8, LANE), F32)],
        compiler_params=_cparams(("arbitrary",)),
        name="rank",
    )(topi)


def _dest_kernel(ti_ref, rank_ref, ps_ref, d_ref):
    ti = ti_ref[...]
    lane = lax.broadcasted_iota(I32, ti.shape, 1)
    ps = ps_ref[...]
    dest = rank_ref[...]
    for k in range(TOP_K):
        base = jnp.sum(jnp.where(lane == ti[:, k:k + 1], ps, 0.0), axis=1, keepdims=True).astype(I32)
        dest = dest + jnp.where(lane == k, base, 0)
    d_ref[...] = dest


def _dest(topi, rank, pstarts_row):
    n = topi.shape[0]
    tm = _pick_tile(n, 512)
    tok = pl.BlockSpec((tm, LANE), lambda i: (i, 0))
    return pl.pallas_call(
        _dest_kernel,
        grid=(n // tm,),
        in_specs=[tok, tok, pl.BlockSpec((1, LANE), lambda i: (0, 0))],
        out_specs=tok,
        out_shape=jax.ShapeDtypeStruct((n, LANE), I32),
        compiler_params=_cparams(("arbitrary",)),
        name="dest",
    )(topi, rank, pstarts_row)


def _dispatch_kernel(dest_ref, pad_ref, h_ref, xs_ref, zbuf, sem, zsem, *, tm, nexp, nblocks):
    def issue(it, carry):
        for u in range(ISSUE_UNROLL):
            r = it * ISSUE_UNROLL + u
            for k in range(TOP_K):
                d = dest_ref[r * TOP_K + k]
                pltpu.make_async_copy(h_ref.at[pl.ds(r, 1)], xs_ref.at[pl.ds(d, 1)], sem).start()
        return carry

    lax.fori_loop(0, tm // ISSUE_UNROLL, issue, 0)

    @pl.when(pl.program_id(0) == pl.num_programs(0) - 1)
    def _():
        zbuf[...] = jnp.zeros(zbuf.shape, F32)

        def fill_expert(e, carry):
            end = pad_ref[0, e]
            npad = pad_ref[1, e]
            p = MOE_BLOCK // 2
            while p >= 1:
                bit = npad & p
                end = end - bit

                @pl.when(bit != 0)
                def _(end=end, p=p):
                    if p >= 8:
                        start = pl.multiple_of(end, 8)
                        pltpu.make_async_copy(zbuf.at[pl.ds(0, p)], xs_ref.at[pl.ds(start, p)], zsem).start()
                    else:
                        for i in range(p):
                            pltpu.make_async_copy(zbuf.at[pl.ds(0, 1)], xs_ref.at[pl.ds(end + i, 1)],
                                                  zsem).start()

                p //= 2
            return carry

        lax.fori_loop(0, nexp, fill_expert, 0)
        nused = pad_ref[2, 0]

        def fill_tail(b, carry):
            @pl.when(b >= nused)
            def _():
                start = pl.multiple_of(b * MOE_BLOCK, MOE_BLOCK)
                pltpu.make_async_copy(zbuf, xs_ref.at[pl.ds(start, MOE_BLOCK)], zsem).start()

            return carry

        lax.fori_loop(0, nblocks, fill_tail, 0)
        for _ in range(nexp):
            pltpu.make_async_copy(zbuf, xs_ref.at[pl.ds(0, MOE_BLOCK)], zsem).wait()

    for k in range(TOP_K):
        pltpu.make_async_copy(h_ref, xs_ref.at[pl.ds(0, tm)], sem).wait()


def _dispatch(dest1d, padinfo, h, rows, nexp):
    n, d = h.shape
    tm = _pick_tile(n, 256)
    return pl.pallas_call(
        functools.partial(_dispatch_kernel, tm=tm, nexp=nexp, nblocks=rows // MOE_BLOCK),
        grid=(n // tm,),
        in_specs=[pl.BlockSpec((tm * TOP_K,), lambda i: (i,), memory_space=pltpu.SMEM),
                  pl.BlockSpec(memory_space=pltpu.SMEM),
                  pl.BlockSpec((tm, d), lambda i: (i, 0))],
        out_specs=pl.BlockSpec(memory_space=pl.ANY),
        out_shape=jax.ShapeDtypeStruct((rows, d), F32),
        scratch_shapes=[pltpu.VMEM((MOE_BLOCK, d), F32), pltpu.SemaphoreType.DMA(()),
                        pltpu.SemaphoreType.DMA(())],
        compiler_params=pltpu.CompilerParams(dimension_semantics=("arbitrary",), has_side_effects=True),
        name="dispatch",
    )(dest1d, padinfo, h)


def _experts_kernel(bexp_ref, nused_ref, xs_ref, w1_ref, b1_ref, w2_ref, b2_ref, y_ref,
                    w1b, w2b, act_scr, prev_scr, *, dff, dm):
    i = pl.program_id(0)
    nu = nused_ref[0]
    e = bexp_ref[jnp.minimum(i, nu - 1)]

    @pl.when(i == 0)
    def _():
        prev_scr[0] = -1

    @pl.when(i >= nu)
    def _():
        y_ref[...] = jnp.zeros(y_ref.shape, F32)

    @pl.when(i < nu)
    def _():
        @pl.when(e != prev_scr[0])
        def _():
            rows = 128

            def cast1(c, carry):
                r0 = pl.multiple_of(c * rows, rows)
                w1b[pl.ds(r0, rows), :] = w1_ref[pl.ds(r0, rows), :].astype(BF16)
                return carry

            def cast2(c, carry):
                r0 = pl.multiple_of(c * rows, rows)
                w2b[pl.ds(r0, rows), :] = w2_ref[pl.ds(r0, rows), :].astype(BF16)
                return carry

            lax.fori_loop(0, dm // rows, cast1, 0)
            lax.fori_loop(0, dff // rows, cast2, 0)
            prev_scr[0] = e

        x = xs_ref[...].astype(BF16)
        cw = 256
        for c in range(dff // cw):
            gt = _dot(x, w1b[:, c * cw:(c + 1) * cw]) + b1_ref[:, c * cw:(c + 1) * cw]
            up = _dot(x, w1b[:, dff + c * cw:dff + (c + 1) * cw]) + b1_ref[:, dff + c * cw:dff + (c + 1) * cw]
            gt = jnp.minimum(gt, SWIGLU_LIMIT)
            up = jnp.clip(up, -SWIGLU_LIMIT, SWIGLU_LIMIT)
            act = gt * jax.nn.sigmoid(gt * SWIGLU_ALPHA) * (up + 1.0)
            act_scr[:, c * cw:(c + 1) * cw] = act.astype(BF16)
        a = act_scr[...]
        for c in range(dm // cw):
            y_ref[:, c * cw:(c + 1) * cw] = _dot(a, w2b[:, c * cw:(c + 1) * cw]) + b2_ref[:, c * cw:(c + 1) * cw]


def _experts(bexp, nused, xs, w1, b1, w2, b2, l):
    rows, dm = xs.shape
    dff = w2.shape[2]
    nb = rows // MOE_BLOCK
    blk = lambda i, be, nu: (jnp.minimum(i, nu[0] - 1), 0)
    wsel = lambda i, be, nu: (l, be[jnp.minimum(i, nu[0] - 1)], 0, 0)
    grid_spec = pltpu.PrefetchScalarGridSpec(
        num_scalar_prefetch=2,
        grid=(nb,),
        in_specs=[
            pl.BlockSpec((MOE_BLOCK, dm), blk),
            pl.BlockSpec((None, None, dm, 2 * dff), wsel),
            pl.BlockSpec((None, None, 1, 2 * dff), wsel),
            pl.BlockSpec((None, None, dff, dm), wsel),
            pl.BlockSpec((None, None, 1, dm), wsel),
        ],
        out_specs=pl.BlockSpec((MOE_BLOCK, dm), lambda i, be, nu: (i, 0)),
        scratch_shapes=[pltpu.VMEM((dm, 2 * dff), BF16), pltpu.VMEM((dff, dm), BF16),
                        pltpu.VMEM((MOE_BLOCK, dff), BF16), pltpu.SMEM((1,), I32)],
    )
    return pl.pallas_call(
        functools.partial(_experts_kernel, dff=dff, dm=dm),
        grid_spec=grid_spec,
        out_shape=jax.ShapeDtypeStruct((rows, dm), F32),
        compiler_params=_cparams(("arbitrary",), VMEM_LIMIT),
        name="experts",
    )(bexp, nused, xs, w1, b1, w2, b2)


def _combine_kernel(dest_ref, gate_ref, x_ref, g_ref, y_ref, o_ref, buf, sem, *, tm, group):
    def issue(it, carry):
        for u in range(ISSUE_UNROLL):
            r = it * ISSUE_UNROLL + u
            for k in range(TOP_K):
                d = dest_ref[r * TOP_K + k]
                pltpu.make_async_copy(y_ref.at[pl.ds(d, 1)], buf.at[k, pl.ds(r, 1)], sem).start()
        return carry

    lax.fori_loop(0, tm // ISSUE_UNROLL, issue, 0)
    for k in range(TOP_K):
        pltpu.make_async_copy(y_ref.at[pl.ds(0, tm)], buf.at[k], sem).wait()
    gate = gate_ref[...]
    moe = gate[:, 0:1] * buf[0]
    for k in range(1, TOP_K):
        moe = moe + gate[:, k:k + 1] * buf[k]
    o_ref[...] = x_ref[...] + _group_affine(moe, g_ref[...], None, group)


def _combine(dest1d, gate, x, modg, l, y, group):
    n, d = x.shape
    tm = _pick_tile(n, 256)
    tg = tm // group
    return pl.pallas_call(
        functools.partial(_combine_kernel, tm=tm, group=group),
        grid=(n // tm,),
        in_specs=[pl.BlockSpec((tm * TOP_K,), lambda i: (i,), memory_space=pltpu.SMEM),
                  pl.BlockSpec((tm, LANE), lambda i: (i, 0)),
                  pl.BlockSpec((tm, d), lambda i: (i, 0)),
                  pl.BlockSpec((None, None, tg, 1, d), lambda i: (l, 5, i, 0, 0)),
                  pl.BlockSpec(memory_space=pl.ANY)],
        out_specs=pl.BlockSpec((tm, d), lambda i: (i, 0)),
        out_shape=jax.ShapeDtypeStruct((n, d), F32),
        scratch_shapes=[pltpu.VMEM((TOP_K, tm, d), F32), pltpu.SemaphoreType.DMA(())],
        compiler_params=_cparams(("arbitrary",)),
        name="combine",
    )(dest1d, gate, x, modg, y)


def _final_norm_kernel(x_ref, g_ref, o_ref):
    o_ref[...] = _rms(x_ref[...], g_ref[...])


def _final_norm(x, g, row0, nrows):
    d = x.shape[1]
    tm = _pick_tile(math.gcd(row0, nrows) if row0 else nrows, 512)
    blk0 = row0 // tm
    return pl.pallas_call(
        _final_norm_kernel,
        grid=(nrows // tm,),
        in_specs=[pl.BlockSpec((tm, d), lambda i: (blk0 + i, 0)), pl.BlockSpec((1, d), lambda i: (0, 0))],
        out_specs=pl.BlockSpec((tm, d), lambda i: (i, 0)),
        out_shape=jax.ShapeDtypeStruct((nrows, d), F32),
        compiler_params=_cparams(("arbitrary",)),
        name="final_norm",
    )(x, g.reshape(1, d))


def _pad_heads(w, heads, dk):
    lead = w.shape[:-1]
    w = w.reshape(*lead, heads, dk)
    w = jnp.pad(w, [(0, 0)] * len(lead) + [(0, 0), (0, LANE - dk)])
    return w.reshape(*lead, heads * LANE)


def _pad_cols(w, width):
    return jnp.pad(w, [(0, 0)] * (w.ndim - 1) + [(0, width - w.shape[-1])])


EVEN_WIDTHS = (H_A * LANE, H_A * LANE, H_A * DV_A, H_A * DV_A, LANE, Q_LORA, KV_LORA, LANE)


def _even_weight(w):
    a_qk, a_v = H_A * DK_A, H_A * DV_A
    c = [0, a_qk, 2 * a_qk, 2 * a_qk + a_v, 2 * a_qk + 2 * a_v, 2 * a_qk + 2 * a_v + GATE_RANK]
    c.append(c[-1] + Q_LORA)
    c.append(c[-1] + KV_LORA)
    c.append(c[-1] + ROPE)
    parts = [
        _pad_heads(w[:, c[0]:c[1]], H_A, DK_A),
        _pad_heads(w[:, c[1]:c[2]], H_A, DK_A),
        w[:, c[2]:c[3]],
        w[:, c[3]:c[4]],
        _pad_cols(w[:, c[4]:c[5]], LANE),
        w[:, c[5]:c[6]],
        w[:, c[6]:c[7]],
        _pad_cols(w[:, c[7]:c[8]], LANE),
    ]
    return jnp.concatenate(parts, axis=1).astype(BF16)


def _uq_weight(w):
    w = w.reshape(Q_LORA, H_B, NOPE + ROPE)
    half = ROPE // 2
    nope = w[:, :, :NOPE].reshape(Q_LORA, H_B * NOPE)
    r1 = w[:, :, NOPE:NOPE + half].reshape(Q_LORA, H_B * half)
    r2 = w[:, :, NOPE + half:].reshape(Q_LORA, H_B * half)
    return jnp.concatenate([nope, r1, r2], axis=1).astype(BF16)


def _rope_perm():
    half = ROPE // 2
    r = jnp.arange(2 * LANE)
    second = r // LANE
    h = (r % LANE) // half
    i = r % half
    col = h * LANE + second * half + i
    return (col[:, None] == jnp.arange(H_B * LANE)[None, :]).astype(BF16)


def _rope_tables(pos):
    half = ROPE // 2
    inv_freq = jnp.exp(-math.log(ROPE_BASE) * jnp.arange(half, dtype=F32) / half)
    ang = pos.astype(F32)[:, None] * inv_freq[None, :]
    cos, sin = jnp.cos(ang), jnp.sin(ang)
    z = jnp.zeros((pos.shape[0], LANE - ROPE), F32)
    cosq, sinq = jnp.tile(cos, (1, LANE // half)), jnp.tile(sin, (1, LANE // half))
    cosk = jnp.concatenate([cos, cos, z], axis=1)
    sink = jnp.concatenate([sin, sin, z], axis=1)
    return cosq, sinq, cosk, sink


def kernel(x_prompt, x_sample, cache_mla_latent, cache_mla_krope, state_gla, state_hgrn, c_prompt, c_sample,
           w_ada, b_ada, norm1_g, norm2_g, w_in_even, w_gla_a2, b_gla_a, gla_norm_g, mla_q_norm_g, w_mla_uq,
           mla_kv_norm_g, w_mla_uk, w_mla_uv, w_out_even, w_in_odd, hgrn_lb, hgrn_norm_g, w_out_odd,
           w_router, b_router, w_e1, b_e1, w_e2, b_e2, final_norm_g):
    bp, tp, d = x_prompt.shape
    bs, ts, _ = x_sample.shape
    past = cache_mla_latent.shape[2]
    depth = w_ada.shape[0]
    nexp = w_router.shape[2]
    n_p, n_s = bp * tp, bs * ts
    n = n_p + n_s
    group = ts
    assert tp % group == 0 and group % 8 == 0 and tp % CHUNK == 0 and ts <= CHUNK
    assert (n * TOP_K) % MOE_BLOCK == 0
    n_even, n_odd = (depth + 1) // 2, depth // 2

    x = jnp.concatenate([x_prompt.reshape(n_p, d), x_sample.reshape(n_s, d)], axis=0)

    mod = _ada(jnp.concatenate([c_prompt, c_sample], axis=0), w_ada, b_ada)
    mod = mod.reshape(depth, bp + bs, 6, d).transpose(0, 2, 1, 3)
    modg = jnp.concatenate([jnp.repeat(mod[:, :, :bp], tp // group, axis=2), mod[:, :, bp:]], axis=2)
    modg = modg.reshape(depth, 6, n // group, 1, d)

    pos = jnp.concatenate([jnp.tile(jnp.arange(tp), bp), jnp.tile(jnp.arange(ts) + past, bs)])
    tabs = _rope_tables(pos)
    perm = _rope_perm()

    lb_soft = jax.nn.softmax(hgrn_lb.astype(F32), axis=0)
    lb_all = jnp.cumsum(lb_soft, axis=0) - lb_soft[0]
    gla_s0 = jnp.pad(state_gla, ((0, 0), (0, 0), (0, 0), (0, LANE - DK_A), (0, 0)))

    lat_p, kpe_p, gla_p, hgrn_p, lat_s, kpe_s, gla_s, hgrn_s = [], [], [], [], [], [], [], []
    rows = (-(-(n * TOP_K) // MOE_BLOCK)) * MOE_BLOCK + nexp * MOE_BLOCK
    nblocks = rows // MOE_BLOCK

    for l in range(depth):
        j = l // 2
        if l % 2 == 0:
            zq, zk, zv, zr, za, zcq, zckv, zkpe = _inproj(
                x, norm1_g[l], modg, l, _even_weight(w_in_even[j]), EVEN_WIDTHS, group)
            wa = _pad_heads(jnp.pad(w_gla_a2[j], ((0, LANE - GATE_RANK), (0, 0))), H_A, DK_A)
            wa_hi = wa.astype(BF16)
            wa_lo = (wa - wa_hi.astype(F32)).astype(BF16)
            ab = _pad_heads(b_gla_a[j].reshape(1, -1), H_A, DK_A)
            gn = gla_norm_g[j].reshape(1, DV_A)
            extras = (za, wa_hi, wa_lo, ab)
            oa_p, sp = _recurrence("gla", zq, zk, zv, zr, extras, gn, None, heads=H_A, nseq=bp,
                                   seqlen=tp, chunk=CHUNK, row0=0)
            oa_s, ss = _recurrence("gla", zq, zk, zv, zr, extras, gn, (gla_s0, j), heads=H_A, nseq=bs,
                                   seqlen=ts, chunk=ts, row0=n_p)
            gla_p.append(sp[:, :, :DK_A, :])
            gla_s.append(ss[:, :, :DK_A, :])

            wuk = w_mla_uk[j].transpose(1, 2, 0).astype(BF16)
            wuv = w_mla_uv[j].transpose(1, 0, 2).astype(BF16)
            wuvt = w_mla_uv[j].transpose(1, 2, 0).astype(BF16)
            qcat, kcat, lat, kpe, qt, latt = _mla_pre(
                zcq, zckv, zkpe, tabs, mla_q_norm_g[j].reshape(1, -1), mla_kv_norm_g[j].reshape(1, -1),
                _uq_weight(w_mla_uq[j]), wuk, perm)
            ob_p = _attn_prompt(qt, kcat, latt, wuvt, bp, tp)
            ob_s = _attn_sample(qcat, kcat, cache_mla_latent, cache_mla_krope, j, wuv, n_p)
            lat_p.append(lat[:n_p].reshape(bp, tp, KV_LORA))
            lat_s.append(lat[n_p:].reshape(bs, ts, KV_LORA))
            kpe_p.append(kpe[:n_p].reshape(bp, tp, ROPE))
            kpe_s.append(kpe[n_p:].reshape(bs, ts, ROPE))
            wo = w_out_even[j].astype(BF16)
            x = _outproj(x, modg, l, [oa_p, ob_p], [oa_s, ob_s], [wo[:H_A * DV_A], wo[H_A * DV_A:]], group)
        else:
            zq, zf, zi, zg = _inproj(x, norm1_g[l], modg, l, w_in_odd[j].astype(BF16),
                                     (H_C * DK_C,) * 2 + (H_C * DV_C,) * 2, group)
            extras = (lb_all[l].reshape(1, -1),)
            gn = hgrn_norm_g[j].reshape(1, DV_C)
            oc_p, sp = _recurrence("hgrn", zq, zf, zi, zg, extras, gn, None, heads=H_C, nseq=bp,
                                   seqlen=tp, chunk=CHUNK, row0=0)
            oc_s, ss = _recurrence("hgrn", zq, zf, zi, zg, extras, gn, (state_hgrn, j), heads=H_C, nseq=bs,
                                   seqlen=ts, chunk=ts, row0=n_p)
            hgrn_p.append(sp)
            hgrn_s.append(ss)
            x = _outproj(x, modg, l, [oc_p], [oc_s], [w_out_odd[j].astype(BF16)], group)

        wr = _pad_cols(w_router[l], LANE)
        wr_hi = wr.astype(BF16)
        wr_lo = (wr - wr_hi.astype(F32)).astype(BF16)
        br = _pad_cols(b_router[l].reshape(1, -1), LANE)
        h2, topi, gate = _router(x, norm2_g[l], modg, l, wr_hi, wr_lo, br, group, nexp)
        rank, cnt = _rank(topi)
        counts = cnt[0, :nexp].astype(I32)
        padded = (counts + MOE_BLOCK - 1) // MOE_BLOCK * MOE_BLOCK
        pends = jnp.cumsum(padded)
        pstarts = _pad_cols((pends - padded).astype(F32).reshape(1, -1), LANE)
        dest = _dest(topi, rank, pstarts)
        dest1d = dest[:, :TOP_K].reshape(n * TOP_K)
        bexp = jnp.clip(jnp.sum(pends[None, :] <= (jnp.arange(nblocks) * MOE_BLOCK)[:, None], axis=1),
                        0, nexp - 1).astype(I32)
        nused = (pends[-1:] // MOE_BLOCK).astype(I32)
        padinfo = jnp.stack([_pad_cols(pends, LANE), _pad_cols(padded - counts, LANE),
                             _pad_cols(nused, LANE)]).astype(I32)
        xs = _dispatch(dest1d, padinfo, h2, rows, nexp)
        y = _experts(bexp, nused, xs, w_e1, b_e1.reshape(depth, nexp, 1, -1), w_e2,
                     b_e2.reshape(depth, nexp, 1, -1), l)
        x = _combine(dest1d, gate, x, modg, l, y, group)

    y_p = _final_norm(x, final_norm_g, 0, n_p)
    y_s = _final_norm(x, final_norm_g, n_p, n_s)
    return (y_p.reshape(bp, tp, d), y_s.reshape(bs, ts, d),
            jnp.stack(lat_p), jnp.stack(kpe_p), jnp.stack(gla_p), jnp.stack(hgrn_p),
            jnp.stack(lat_s), jnp.stack(kpe_s), jnp.stack(gla_s), jnp.stack(hgrn_s))
```

```python
import functools
import math

import jax
import jax.numpy as jnp
from jax import lax
from jax.experimental import pallas as pl
from jax.experimental.pallas import tpu as pltpu

F32 = jnp.float32
BF16 = jnp.bfloat16
I32 = jnp.int32

EPS = 1e-6
CHUNK = 64
LANE = 128

H_A, DK_A, DV_A, GATE_RANK, GATE_TAU = 4, 64, 128, 16, 16.0
H_B, Q_LORA, KV_LORA, NOPE, ROPE, V_B = 4, 384, 256, 128, 64, 128
ROPE_BASE = 10000.0
MLA_SCALE = (NOPE + ROPE) ** -0.5
KCAT = KV_LORA + LANE
H_C, DK_C, DV_C = 8, 128, 128
TOP_K = 4
SWIGLU_LIMIT = 7.0
SWIGLU_ALPHA = 1.702
MOE_BLOCK = 512
ISSUE_UNROLL = 4
EXP_CLAMP = 80.0

VMEM_LIMIT = 56 * 1024 * 1024


def _cparams(sem, vmem=None):
    return pltpu.CompilerParams(dimension_semantics=sem, vmem_limit_bytes=vmem)


def _pick_tile(n, pref):
    t = pref
    while n % t:
        t //= 2
    return t


def _rms(x, g):
    return x * lax.rsqrt(jnp.mean(x * x, axis=-1, keepdims=True) + EPS) * g


def _group_affine(x, scale, shift, group):
    tm, d = x.shape
    x3 = x.reshape(tm // group, group, d)
    if scale is not None:
        x3 = x3 * scale
    if shift is not None:
        x3 = x3 + shift
    return x3.reshape(tm, d)


def _cumsum_rows(x):
    c, w = x.shape
    row = lax.broadcasted_iota(I32, (c, 1), 0)
    s = 1
    while s < c:
        if s % 8 == 0:
            shifted = jnp.concatenate([jnp.zeros((s, w), x.dtype), x[:c - s, :]], axis=0)
        else:
            shifted = jnp.where(row >= s, pltpu.roll(x, s, 0), 0.0)
        x = x + shifted
        s *= 2
    return x


def _dot(a, b):
    return jnp.dot(a, b, preferred_element_type=F32)


def _dot_nt(a, b):
    return lax.dot_general(a, b, (((1,), (1,)), ((), ())), preferred_element_type=F32)


def _dot_tn(a, b):
    return lax.dot_general(a, b, (((0,), (0,)), ((), ())), preferred_element_type=F32)


def _ada_kernel(c_ref, w_ref, b_ref, o_ref):
    c = c_ref[...]
    a = (c * jax.nn.sigmoid(c)).astype(BF16)
    o_ref[...] = _dot(a, w_ref[...].astype(BF16)) + b_ref[...]


def _ada(c_all, w_ada, b_ada):
    depth, d, n6 = w_ada.shape
    s = c_all.shape[0]
    tn = _pick_tile(n6, 1536)
    return pl.pallas_call(
        _ada_kernel,
        grid=(depth, n6 // tn),
        in_specs=[
            pl.BlockSpec((s, d), lambda l, j: (0, 0)),
            pl.BlockSpec((None, d, tn), lambda l, j: (l, 0, j)),
            pl.BlockSpec((None, 1, tn), lambda l, j: (l, 0, j)),
        ],
        out_specs=pl.BlockSpec((None, s, tn), lambda l, j: (l, 0, j)),
        out_shape=jax.ShapeDtypeStruct((depth, s, n6), F32),
        compiler_params=_cparams(("arbitrary", "arbitrary")),
        name="ada",
    )(c_all, w_ada, b_ada.reshape(depth, 1, n6))


def _inproj_kernel(x_ref, g_ref, sc_ref, sh_ref, w_ref, *out_refs, splits, group):
    h = _rms(x_ref[...], g_ref[...])
    h = _group_affine(h, 1.0 + sc_ref[...], sh_ref[...], group).astype(BF16)
    for o_ref, (c0, c1) in zip(out_refs, splits):
        o_ref[...] = _dot(h, w_ref[:, c0:c1]).astype(o_ref.dtype)


def _inproj(x, gain, modg, l, w, widths, group):
    n, d = x.shape
    tm = _pick_tile(n, 512)
    tg = tm // group
    splits, c = [], 0
    for wd in widths:
        splits.append((c, c + wd))
        c += wd
    mod_spec = lambda comp: pl.BlockSpec((None, None, tg, 1, d), lambda i: (l, comp, i, 0, 0))
    return pl.pallas_call(
        functools.partial(_inproj_kernel, splits=tuple(splits), group=group),
        grid=(n // tm,),
        in_specs=[
            pl.BlockSpec((tm, d), lambda i: (i, 0)),
            pl.BlockSpec((1, d), lambda i: (0, 0)),
            mod_spec(1),
            mod_spec(0),
            pl.BlockSpec((d, c), lambda i: (0, 0)),
        ],
        out_specs=[pl.BlockSpec((tm, wd), lambda i: (i, 0)) for wd in widths],
        out_shape=[jax.ShapeDtypeStruct((n, wd), F32) for wd in widths],
        compiler_params=_cparams(("arbitrary",), VMEM_LIMIT),
        name="inproj",
    )(x, gain.reshape(1, d), modg, modg, w)


def _rec_kernel(*refs, mode, heads, chunk, nchunks, zero_init):
    refs = list(refs)
    if mode == "gla":
        q_ref, k_ref, v_ref, r_ref, a_ref, wah_ref, wal_ref, ab_ref, gn_ref = refs[:9]
        refs = refs[9:]
    else:
        q_ref, k_ref, v_ref, r_ref, lb_ref, gn_ref = refs[:6]
        refs = refs[6:]
    s0_ref = None
    if not zero_init:
        s0_ref = refs.pop(0)
    o_ref, sout_ref = refs[:2]
    st_scr = refs[2:]
    ci = pl.program_id(1)

    @pl.when(ci == 0)
    def _():
        for h in range(heads):
            if zero_init:
                st_scr[h][...] = jnp.zeros((LANE, LANE), F32)
            else:
                st_scr[h][...] = s0_ref[0, h].T

    row = lax.broadcasted_iota(I32, (chunk, chunk), 0)
    col = lax.broadcasted_iota(I32, (chunk, chunk), 1)
    causal = row >= col
    mid = chunk // 2 - 1

    if mode == "gla":
        a = a_ref[...]
        a_hi = a.astype(BF16)
        a_lo = (a - a_hi.astype(F32)).astype(BF16)
        alog_all = (_dot(a_hi, wah_ref[...]) + _dot(a_lo, wah_ref[...]) + _dot(a_hi, wal_ref[...])
                    + ab_ref[...])
        g_all = jax.nn.log_sigmoid(alog_all) * (1.0 / GATE_TAU)
        fg_all = None
    else:
        lb = lb_ref[...]
        fg_all = lb + (1.0 - lb) * jax.nn.sigmoid(k_ref[...])
        g_all = jnp.log(fg_all)
    b_all = _cumsum_rows(g_all)

    for h in range(heads):
        sl = slice(h * LANE, (h + 1) * LANE)
        if mode == "gla":
            q = q_ref[:, sl] * (DK_A ** -0.5)
            k = k_ref[:, sl]
        else:
            qr = q_ref[:, sl]
            q = qr * jax.nn.sigmoid(qr)
            k = 1.0 - fg_all[:, sl]
        b = b_all[:, sl]
        b_last = b[chunk - 1:chunk, :]
        b_ref_row = b[mid:mid + 1, :]
        qt = (q * jnp.exp(jnp.minimum(b - b_ref_row, EXP_CLAMP))).astype(BF16)
        kt = (k * jnp.exp(jnp.minimum(b_ref_row - b, EXP_CLAMP))).astype(BF16)
        qs = (q * jnp.exp(b)).astype(BF16)
        ks = (k * jnp.exp(b_last - b)).astype(BF16)
        v = v_ref[:, sl].astype(BF16)
        att = jnp.where(causal, _dot_nt(qt, kt), 0.0).astype(BF16)
        st = st_scr[h][...]
        o = _dot(att, v) + _dot_nt(qs, st.astype(BF16))
        st_scr[h][...] = st * jnp.exp(b_last) + _dot_tn(v, ks)
        rg = r_ref[:, sl]
        o = _rms(o, gn_ref[...]) * (rg * jax.nn.sigmoid(rg))
        o_ref[:, sl] = o.astype(o_ref.dtype)

    @pl.when(ci == nchunks - 1)
    def _():
        for h in range(heads):
            sout_ref[0, h] = st_scr[h][...].T


def _recurrence(mode, q, k, v, r, extras, gnorm, s0, *, heads, nseq, seqlen, chunk, row0):
    cols = q.shape[1]
    nchunks = seqlen // chunk
    blk0 = row0 // chunk
    tok = lambda b, c: (blk0 + b * nchunks + c, 0)
    full = lambda a: pl.BlockSpec(a.shape, lambda b, c: (0,) * a.ndim)
    slab = pl.BlockSpec((chunk, cols), tok)
    in_specs = [slab, slab, slab, slab]
    args = [q, k, v, r]
    if mode == "gla":
        a_lr, wah, wal, ab = extras
        in_specs += [pl.BlockSpec((chunk, a_lr.shape[1]), tok), full(wah), full(wal), full(ab)]
        args += [a_lr, wah, wal, ab]
    else:
        (lb,) = extras
        in_specs += [full(lb)]
        args += [lb]
    in_specs.append(full(gnorm))
    args.append(gnorm)
    zero_init = s0 is None
    if not zero_init:
        s0, s0_layer = s0
        in_specs.append(pl.BlockSpec((None, 1, heads, LANE, LANE), lambda b, c: (s0_layer, b, 0, 0, 0)))
        args.append(s0)
    return pl.pallas_call(
        functools.partial(_rec_kernel, mode=mode, heads=heads, chunk=chunk, nchunks=nchunks,
                          zero_init=zero_init),
        grid=(nseq, nchunks),
        in_specs=in_specs,
        out_specs=[pl.BlockSpec((chunk, cols), lambda b, c: (b * nchunks + c, 0)),
                   pl.BlockSpec((1, heads, LANE, LANE), lambda b, c: (b, 0, 0, 0))],
        out_shape=[jax.ShapeDtypeStruct((nseq * seqlen, cols), BF16),
                   jax.ShapeDtypeStruct((nseq, heads, LANE, LANE), F32)],
        scratch_shapes=[pltpu.VMEM((LANE, LANE), F32) for _ in range(heads)],
        compiler_params=_cparams(("arbitrary", "arbitrary")),
        name="recurrence_" + mode,
    )(*args)


def _mla_pre_kernel(cq_ref, ckv_ref, kpe_ref, cosq_ref, sinq_ref, cosk_ref, sink_ref, qg_ref, kvg_ref,
                    wuq_ref, wuk_ref, perm_ref, qcat_ref, kcat_ref, lat_ref, kpeo_ref, qt_ref, latt_ref, *, tq):
    tm = cq_ref.shape[0]
    cqn = _rms(cq_ref[...], qg_ref[...]).astype(BF16)
    qf = _dot(cqn, wuq_ref[...])
    off = H_B * NOPE
    x1 = qf[:, off:off + LANE]
    x2 = qf[:, off + LANE:off + 2 * LANE]
    cq, sq = cosq_ref[...], sinq_ref[...]
    o1 = (x1 * cq - x2 * sq) * MLA_SCALE
    o2 = (x2 * cq + x1 * sq) * MLA_SCALE
    pe = _dot(o1.astype(BF16), perm_ref[0:LANE, :]) + _dot(o2.astype(BF16), perm_ref[LANE:2 * LANE, :])
    for h in range(H_B):
        ql = _dot(qf[:, h * NOPE:(h + 1) * NOPE].astype(BF16), wuk_ref[h]) * MLA_SCALE
        peh = pe[:, h * LANE:(h + 1) * LANE]
        qcat_ref[h, :, 0:KV_LORA] = ql.astype(BF16)
        qcat_ref[h, :, KV_LORA:KCAT] = peh.astype(BF16)
        for jb in range(tm // tq):
            rows = slice(jb * tq, (jb + 1) * tq)
            cols = slice(h * tq, (h + 1) * tq)
            qt_ref[jb, 0:KV_LORA, cols] = ql[rows, :].T.astype(BF16)
            qt_ref[jb, KV_LORA:KCAT, cols] = peh[rows, :].T.astype(BF16)
    latn = _rms(ckv_ref[...], kvg_ref[...])
    lat_ref[...] = latn
    latt_ref[...] = latn.T.astype(BF16)
    x = kpe_ref[...]
    half = ROPE // 2
    lane = lax.broadcasted_iota(I32, x.shape, 1)
    rot = jnp.where(lane < half, -pltpu.roll(x, LANE - half, 1), pltpu.roll(x, half, 1))
    kro = x * cosk_ref[...] + rot * sink_ref[...]
    kpeo_ref[...] = kro[:, 0:ROPE]
    kcat_ref[:, 0:KV_LORA] = latn.astype(BF16)
    kcat_ref[:, KV_LORA:KCAT] = kro.astype(BF16)


ATTN_TQ = 4 * CHUNK


def _mla_pre(zcq, zckv, zkpe, tabs, qg, kvg, wuq, wuk, perm):
    n = zcq.shape[0]
    tm = _pick_tile(n, 512)
    tq = ATTN_TQ
    assert tm % tq == 0
    tokspec = lambda wd: pl.BlockSpec((tm, wd), lambda i: (i, 0))
    full = lambda a: pl.BlockSpec(a.shape, lambda i: (0,) * a.ndim)
    return pl.pallas_call(
        functools.partial(_mla_pre_kernel, tq=tq),
        grid=(n // tm,),
        in_specs=[tokspec(Q_LORA), tokspec(KV_LORA), tokspec(LANE)] + [tokspec(LANE)] * 4
                 + [full(qg), full(kvg), full(wuq), full(wuk), full(perm)],
        out_specs=[pl.BlockSpec((H_B, tm, KCAT), lambda i: (0, i, 0)), tokspec(KCAT),
                   tokspec(KV_LORA), tokspec(ROPE),
                   pl.BlockSpec((tm // tq, KCAT, H_B * tq), lambda i: (i, 0, 0)),
                   pl.BlockSpec((KV_LORA, tm), lambda i: (0, i))],
        out_shape=[jax.ShapeDtypeStruct((H_B, n, KCAT), BF16), jax.ShapeDtypeStruct((n, KCAT), BF16),
                   jax.ShapeDtypeStruct((n, KV_LORA), F32), jax.ShapeDtypeStruct((n, ROPE), F32),
                   jax.ShapeDtypeStruct((n // tq, KCAT, H_B * tq), BF16),
                   jax.ShapeDtypeStruct((KV_LORA, n), BF16)],
        compiler_params=_cparams(("arbitrary",)),
        name="mla_pre",
    )(zcq, zckv, zkpe, *tabs, qg, kvg, wuq, wuk, perm)


def _softmax_update(s, vals, m_scr, l_scr, acc_scr):
    m_prev = m_scr[...]
    m_new = jnp.maximum(m_prev, jnp.max(s, axis=1, keepdims=True))
    alpha = jnp.exp(m_prev - m_new)
    p = jnp.exp(s - m_new)
    l_scr[...] = alpha * l_scr[...] + jnp.sum(p, axis=1, keepdims=True)
    acc_scr[...] = alpha * acc_scr[...] + _dot(p.astype(BF16), vals)
    m_scr[...] = m_new


def _softmax_init(m_scr, l_scr, acc_scr):
    m_scr[...] = jnp.full(m_scr.shape, -jnp.inf, F32)
    l_scr[...] = jnp.zeros(l_scr.shape, F32)
    acc_scr[...] = jnp.zeros(acc_scr.shape, F32)


def _attn_finish(o_ref, wuv_ref, l_scr, acc_scr, tq):
    inv = 1.0 / l_scr[...]
    for h in range(H_B):
        rows = slice(h * tq, (h + 1) * tq)
        oh = (acc_scr[rows, :] * inv[rows, :]).astype(BF16)
        o_ref[:, h * V_B:(h + 1) * V_B] = _dot(oh, wuv_ref[h]).astype(o_ref.dtype)


def _attn_prompt_kernel(qt_ref, k_ref, latt_ref, wuvt_ref, o_ref, m_scr, l_scr, acc_scr, p_scr, *, tq, tk):
    qi = pl.program_id(1)
    cols = H_B * tq
    qt = qt_ref[...]
    m_scr[...] = jnp.full(m_scr.shape, -jnp.inf, F32)
    l_scr[...] = jnp.zeros(l_scr.shape, F32)
    acc_scr[...] = jnp.zeros(acc_scr.shape, F32)

    def block(start, masked):
        s = _dot(k_ref[pl.ds(start, tk), :], qt)
        if masked:
            tok = qi * tq + (lax.broadcasted_iota(I32, (1, cols), 1) & (tq - 1))
            limit = (tok // CHUNK + 1) * CHUNK
            key = start + lax.broadcasted_iota(I32, (tk, 1), 0)
            s = jnp.where(key < limit, s, -jnp.inf)
        m_prev = m_scr[...]
        m_new = jnp.maximum(m_prev, jnp.max(s, axis=0, keepdims=True))
        alpha = jnp.exp(m_prev - m_new)
        p = jnp.exp(s - m_new)
        l_scr[...] = alpha * l_scr[...] + jnp.sum(p, axis=0, keepdims=True)
        m_scr[...] = m_new
        p_scr[...] = p.astype(BF16)
        acc_scr[...] = alpha * acc_scr[...] + _dot(latt_ref[:, pl.ds(start, tk)], p_scr[...])

    nfull = (qi * tq) // tk

    def full(j, carry):
        block(pl.multiple_of(j * tk, tk), False)
        return carry

    lax.fori_loop(0, nfull, full, 0)
    block(pl.multiple_of(nfull * tk, tk), True)

    ot =(acc_scr[...] * (1.0 / l_scr[...])).astype(BF16)
    for h in range(H_B):
        oh = _dot(wuvt_ref[h], ot[:, h * tq:(h + 1) * tq])
        o_ref[:, h * V_B:(h + 1) * V_B] = oh.T.astype(o_ref.dtype)


def _attn_prompt(qt, kcat, latt, wuvt, nseq, seqlen):
    n = nseq * seqlen
    tq = ATTN_TQ
    tk = 512
    assert seqlen % tk == 0 and tk % tq == 0
    nq = seqlen // tq
    return pl.pallas_call(
        functools.partial(_attn_prompt_kernel, tq=tq, tk=tk),
        grid=(nseq, nq),
        in_specs=[
            pl.BlockSpec((None, KCAT, H_B * tq), lambda b, qi: (b * nq + qi, 0, 0)),
            pl.BlockSpec((seqlen, KCAT), lambda b, qi: (b, 0)),
            pl.BlockSpec((KV_LORA, seqlen), lambda b, qi: (0, b)),
            pl.BlockSpec(wuvt.shape, lambda b, qi: (0, 0, 0)),
        ],
        out_specs=pl.BlockSpec((tq, H_B * V_B), lambda b, qi: (b * nq + qi, 0)),
        out_shape=jax.ShapeDtypeStruct((n, H_B * V_B), BF16),
        scratch_shapes=[pltpu.VMEM((1, H_B * tq), F32), pltpu.VMEM((1, H_B * tq), F32),
                        pltpu.VMEM((KV_LORA, H_B * tq), F32), pltpu.VMEM((tk, H_B * tq), BF16)],
        compiler_params=_cparams(("arbitrary", "arbitrary")),
        name="attn_prompt",
    )(qt, kcat, latt, wuvt)


def _attn_sample_kernel(q_ref, plat_ref, pkpe_ref, knew_ref, wuv_ref, o_ref,
                        m_scr, l_scr, acc_scr, *, tq, nkp):
    ki = pl.program_id(1)

    @pl.when(ki == 0)
    def _():
        _softmax_init(m_scr, l_scr, acc_scr)

    q = q_ref[...].reshape(H_B * tq, KCAT)
    lat = plat_ref[...].astype(BF16)
    kpe = pkpe_ref[...].astype(BF16)
    s = _dot_nt(q[:, 0:KV_LORA], lat) + _dot_nt(q[:, KV_LORA:KV_LORA + ROPE], kpe)
    _softmax_update(s, lat, m_scr, l_scr, acc_scr)

    @pl.when(ki == nkp - 1)
    def _():
        kn = knew_ref[...]
        _softmax_update(_dot_nt(q, kn), kn[:, 0:KV_LORA], m_scr, l_scr, acc_scr)
        _attn_finish(o_ref, wuv_ref, l_scr, acc_scr, tq)


def _attn_sample(qcat, kcat, past_lat, past_kpe, layer, wuv, row0):
    _, nseq, past, _ = past_lat.shape
    tq = (kcat.shape[0] - row0) // nseq
    tkp = _pick_tile(past, 1024)
    nkp = past // tkp
    blk0 = row0 // tq
    return pl.pallas_call(
        functools.partial(_attn_sample_kernel, tq=tq, nkp=nkp),
        grid=(nseq, nkp),
        in_specs=[
            pl.BlockSpec((H_B, tq, KCAT), lambda b, ki: (0, blk0 + b, 0)),
            pl.BlockSpec((None, None, tkp, KV_LORA), lambda b, ki: (layer, b, ki, 0)),
            pl.BlockSpec((None, None, tkp, ROPE), lambda b, ki: (layer, b, ki, 0)),
            pl.BlockSpec((tq, KCAT), lambda b, ki: (blk0 + b, 0)),
            pl.BlockSpec(wuv.shape, lambda b, ki: (0, 0, 0)),
        ],
        out_specs=pl.BlockSpec((tq, H_B * V_B), lambda b, ki: (b, 0)),
        out_shape=jax.ShapeDtypeStruct((nseq * tq, H_B * V_B), BF16),
        scratch_shapes=[pltpu.VMEM((H_B * tq, 1), F32), pltpu.VMEM((H_B * tq, 1), F32),
                        pltpu.VMEM((H_B * tq, KV_LORA), F32)],
        compiler_params=_cparams(("arbitrary", "arbitrary")),
        name="attn_sample",
    )(qcat, past_lat, past_kpe, kcat, wuv)


def _outproj_kernel(*refs, nlhs, group, ptiles):
    x_ref, g_ref = refs[0], refs[1]
    lhs_p = refs[2:2 + nlhs]
    lhs_s = refs[2 + nlhs:2 + 2 * nlhs]
    ws = refs[2 + 2 * nlhs:2 + 3 * nlhs]
    o_ref = refs[2 + 3 * nlhs]
    is_prompt = pl.program_id(0) < ptiles
    acc = None
    for ap, asm, w in zip(lhs_p, lhs_s, ws):
        a = jnp.where(is_prompt, ap[...], asm[...])
        t = _dot(a, w[...])
        acc = t if acc is None else acc + t
    o_ref[...] = x_ref[...] + _group_affine(acc, g_ref[...], None, group)


def _outproj(x, modg, l, lhs_p, lhs_s, ws, group):
    n, d = x.shape
    n_p, n_s = lhs_p[0].shape[0], lhs_s[0].shape[0]
    tm = _pick_tile(math.gcd(n_p, n_s), 512)
    tg = tm // group
    ptiles = n_p // tm
    return pl.pallas_call(
        functools.partial(_outproj_kernel, nlhs=len(ws), group=group, ptiles=ptiles),
        grid=(n // tm,),
        in_specs=[pl.BlockSpec((tm, d), lambda i: (i, 0)),
                  pl.BlockSpec((None, None, tg, 1, d), lambda i: (l, 2, i, 0, 0))]
                 + [pl.BlockSpec((tm, a.shape[1]), lambda i: (jnp.minimum(i, ptiles - 1), 0)) for a in lhs_p]
                 + [pl.BlockSpec((tm, a.shape[1]), lambda i: (jnp.maximum(i - ptiles, 0), 0)) for a in lhs_s]
                 + [pl.BlockSpec(w.shape, lambda i: (0, 0)) for w in ws],
        out_specs=pl.BlockSpec((tm, d), lambda i: (i, 0)),
        out_shape=jax.ShapeDtypeStruct((n, d), F32),
        compiler_params=_cparams(("arbitrary",)),
        name="outproj",
    )(x, modg, *lhs_p, *lhs_s, *ws)


def _router_kernel(x_ref, g_ref, sc_ref, sh_ref, wh_ref, wl_ref, br_ref, h_ref, ti_ref, gt_ref,
                   *, group, nexp):
    h = _rms(x_ref[...], g_ref[...])
    h = _group_affine(h, 1.0 + sc_ref[...], sh_ref[...], group)
    h_ref[...] = h
    h_hi = h.astype(BF16)
    h_lo = (h - h_hi.astype(F32)).astype(BF16)
    logits = _dot(h_hi, wh_ref[...]) + _dot(h_lo, wh_ref[...]) + _dot(h_hi, wl_ref[...]) + br_ref[...]
    lane = lax.broadcasted_iota(I32, logits.shape, 1)
    lane_f = lane.astype(F32)
    cur = jnp.where(lane < nexp, logits, -jnp.inf)
    tops, idxs = [], []
    for _ in range(TOP_K):
        m = jnp.max(cur, axis=1, keepdims=True)
        i = jnp.min(jnp.where(cur == m, lane_f, float(LANE)), axis=1, keepdims=True)
        cur = jnp.where(lane_f == i, -jnp.inf, cur)
        tops.append(m)
        idxs.append(i.astype(I32))
    es = [jnp.exp(t - tops[0]) for t in tops]
    inv = 1.0 / (es[0] + es[1] + es[2] + es[3])
    ti = jnp.zeros(logits.shape, I32)
    gt = jnp.zeros(logits.shape, F32)
    for k in range(TOP_K):
        ti = jnp.where(lane == k, idxs[k], ti)
        gt = jnp.where(lane == k, es[k] * inv, gt)
    ti_ref[...] = ti
    gt_ref[...] = gt


def _router(x, gain, modg, l, wr_hi, wr_lo, br, group, nexp):
    n, d = x.shape
    tm = _pick_tile(n, 512)
    tg = tm // group
    mod_spec = lambda comp: pl.BlockSpec((None, None, tg, 1, d), lambda i: (l, comp, i, 0, 0))
    full = lambda a: pl.BlockSpec(a.shape, lambda i: (0,) * a.ndim)
    tok = lambda wd: pl.BlockSpec((tm, wd), lambda i: (i, 0))
    return pl.pallas_call(
        functools.partial(_router_kernel, group=group, nexp=nexp),
        grid=(n // tm,),
        in_specs=[tok(d), pl.BlockSpec((1, d), lambda i: (0, 0)), mod_spec(4), mod_spec(3),
                  full(wr_hi), full(wr_lo), full(br)],
        out_specs=[tok(d), tok(LANE), tok(LANE)],
        out_shape=[jax.ShapeDtypeStruct((n, d), F32), jax.ShapeDtypeStruct((n, LANE), I32),
                   jax.ShapeDtypeStruct((n, LANE), F32)],
        compiler_params=_cparams(("arbitrary",)),
        name="router",
    )(x, gain.reshape(1, d), modg, modg, wr_hi, wr_lo, br)


def _rank_kernel(ti_ref, rank_ref, cnt_ref, carry_scr, *, tm):
    i = pl.program_id(0)

    @pl.when(i == 0)
    def _():
        carry_scr[...] = jnp.zeros(carry_scr.shape, F32)

    ti = ti_ref[...]
    lane = lax.broadcasted_iota(I32, ti.shape, 1)
    sel = [lane == ti[:, k:k + 1] for k in range(TOP_K)]
    hot = jnp.zeros(ti.shape, F32)
    for s in sel:
        hot = hot + jnp.where(s, 1.0, 0.0)
    row = lax.broadcasted_iota(I32, (tm, tm), 0)
    col = lax.broadcasted_iota(I32, (tm, tm), 1)
    strict = jnp.where(row > col, 1.0, 0.0).astype(BF16)
    before = _dot(strict, hot.astype(BF16)) + carry_scr[0:1, :]
    rank = jnp.zeros(ti.shape, F32)
    for k in range(TOP_K):
        rk = jnp.sum(jnp.where(sel[k], before, 0.0), axis=1, keepdims=True)
        rank = jnp.where(lane == k, rk, rank)
    rank_ref[...] = rank.astype(I32)
    carry_scr[...] = carry_scr[...] + jnp.sum(hot, axis=0, keepdims=True)
    cnt_ref[...] = carry_scr[...]


def _rank(topi):
    n = topi.shape[0]
    tm = _pick_tile(n, 512)
    return pl.pallas_call(
        functools.partial(_rank_kernel, tm=tm),
        grid=(n // tm,),
        in_specs=[pl.BlockSpec((tm, LANE), lambda i: (i, 0))],
        out_specs=[pl.BlockSpec((tm, LANE), lambda i: (i, 0)), pl.BlockSpec((8, LANE), lambda i: (0, 0))],
        out_shape=[jax.ShapeDtypeStruct((n, LANE), I32), jax.ShapeDtypeStruct((8, LANE), F32)],
        scratch_shapes=[pltpu.VMEM((8, LANE), F32)],
        compiler_params=_cparams(("arbitrary",)),
        name="rank",
    )(topi)


def _dest_kernel(ti_ref, rank_ref, ps_ref, d_ref):
    ti = ti_ref[...]
    lane = lax.broadcasted_iota(I32, ti.shape, 1)
    ps = ps_ref[...]
    dest = rank_ref[...]
    for k in range(TOP_K):
        base = jnp.sum(jnp.where(lane == ti[:, k:k + 1], ps, 0.0), axis=1, keepdims=True).astype(I32)
        dest = dest + jnp.where(lane == k, base, 0)
    d_ref[...] = dest


def _dest(topi, rank, pstarts_row):
    n = topi.shape[0]
    tm = _pick_tile(n, 512)
    tok = pl.BlockSpec((tm, LANE), lambda i: (i, 0))
    return pl.pallas_call(
        _dest_kernel,
        grid=(n // tm,),
        in_specs=[tok, tok, pl.BlockSpec((1, LANE), lambda i: (0, 0))],
        out_specs=tok,
        out_shape=jax.ShapeDtypeStruct((n, LANE), I32),
        compiler_params=_cparams(("arbitrary",)),
        name="dest",
    )(topi, rank, pstarts_row)


def _dispatch_kernel(dest_ref, pad_ref, h_ref, xs_ref, hbuf, zbuf, sem, zsem, *, tm, nexp, nblocks):
    i = pl.program_id(0)
    slot = lax.rem(i, 2)
    hbuf[slot] = h_ref[...]

    def issue(it, carry):
        for u in range(ISSUE_UNROLL):
            r = it * ISSUE_UNROLL + u
            for k in range(TOP_K):
                d = dest_ref[r * TOP_K + k]
                pltpu.make_async_copy(hbuf.at[slot, pl.ds(r, 1)], xs_ref.at[pl.ds(d, 1)], sem.at[slot]).start()
        return carry

    lax.fori_loop(0, tm // ISSUE_UNROLL, issue, 0)

    def retire(s):
        for _ in range(TOP_K):
            pltpu.make_async_copy(hbuf.at[s], xs_ref.at[pl.ds(0, tm)], sem.at[s]).wait()

    @pl.when(i >= 1)
    def _():
        retire(1 - slot)

    @pl.when(i == pl.num_programs(0) - 1)
    def _():
        retire(slot)
        zbuf[...] = jnp.zeros(zbuf.shape, F32)

        def fill_expert(e, carry):
            end = pad_ref[0, e]
            npad = pad_ref[1, e]
            p = MOE_BLOCK // 2
            while p >= 1:
                bit = npad & p
                end = end - bit

                @pl.when(bit != 0)
                def _(end=end, p=p):
                    if p >= 8:
                        start = pl.multiple_of(end, 8)
                        pltpu.make_async_copy(zbuf.at[pl.ds(0, p)], xs_ref.at[pl.ds(start, p)], zsem).start()
                    else:
                        for q in range(p):
                            pltpu.make_async_copy(zbuf.at[pl.ds(0, 1)], xs_ref.at[pl.ds(end + q, 1)],
                                                  zsem).start()

                p //= 2
            return carry

        lax.fori_loop(0, nexp, fill_expert, 0)
        nused = pad_ref[2, 0]

        def fill_tail(b, carry):
            @pl.when(b >= nused)
            def _():
                start = pl.multiple_of(b * MOE_BLOCK, MOE_BLOCK)
                pltpu.make_async_copy(zbuf, xs_ref.at[pl.ds(start, MOE_BLOCK)], zsem).start()

            return carry

        lax.fori_loop(0, nblocks, fill_tail, 0)
        for _ in range(nexp):
            pltpu.make_async_copy(zbuf, xs_ref.at[pl.ds(0, MOE_BLOCK)], zsem).wait()


def _dispatch(dest1d, padinfo, h, rows, nexp):
    n, d = h.shape
    tm = _pick_tile(n, 256)
    return pl.pallas_call(
        functools.partial(_dispatch_kernel, tm=tm, nexp=nexp, nblocks=rows // MOE_BLOCK),
        grid=(n // tm,),
        in_specs=[pl.BlockSpec((tm * TOP_K,), lambda i: (i,), memory_space=pltpu.SMEM),
                  pl.BlockSpec(memory_space=pltpu.SMEM),
                  pl.BlockSpec((tm, d), lambda i: (i, 0))],
        out_specs=pl.BlockSpec(memory_space=pl.ANY),
        out_shape=jax.ShapeDtypeStruct((rows, d), F32),
        scratch_shapes=[pltpu.VMEM((2, tm, d), F32), pltpu.VMEM((MOE_BLOCK, d), F32),
                        pltpu.SemaphoreType.DMA((2,)), pltpu.SemaphoreType.DMA(())],
        compiler_params=pltpu.CompilerParams(dimension_semantics=("arbitrary",), has_side_effects=True),
        name="dispatch",
    )(dest1d, padinfo, h)


def _experts_kernel(bexp_ref, nused_ref, xs_ref, w1_ref, b1_ref, w2_ref, b2_ref, y_ref,
                    w1b, w2b, act_scr, prev_scr, *, dff, dm):
    i = pl.program_id(0)
    nu = nused_ref[0]
    e = bexp_ref[jnp.minimum(i, nu - 1)]

    @pl.when(i == 0)
    def _():
        prev_scr[0] = -1

    @pl.when(i >= nu)
    def _():
        y_ref[...] = jnp.zeros(y_ref.shape, F32)

    @pl.when(i < nu)
    def _():
        @pl.when(e != prev_scr[0])
        def _():
            rows = 128

            def cast1(c, carry):
                r0 = pl.multiple_of(c * rows, rows)
                w1b[pl.ds(r0, rows), :] = w1_ref[pl.ds(r0, rows), :].astype(BF16)
                return carry

            def cast2(c, carry):
                r0 = pl.multiple_of(c * rows, rows)
                w2b[pl.ds(r0, rows), :] = w2_ref[pl.ds(r0, rows), :].astype(BF16)
                return carry

            lax.fori_loop(0, dm // rows, cast1, 0)
            lax.fori_loop(0, dff // rows, cast2, 0)
            prev_scr[0] = e

        x = xs_ref[...].astype(BF16)
        cw = 256
        for c in range(dff // cw):
            gt = _dot(x, w1b[:, c * cw:(c + 1) * cw]) + b1_ref[:, c * cw:(c + 1) * cw]
            up = _dot(x, w1b[:, dff + c * cw:dff + (c + 1) * cw]) + b1_ref[:, dff + c * cw:dff + (c + 1) * cw]
            gt = jnp.minimum(gt, SWIGLU_LIMIT)
            up = jnp.clip(up, -SWIGLU_LIMIT, SWIGLU_LIMIT)
            act = gt * jax.nn.sigmoid(gt * SWIGLU_ALPHA) * (up + 1.0)
            act_scr[:, c * cw:(c + 1) * cw] = act.astype(BF16)
        a = act_scr[...]
        for c in range(dm // cw):
            y_ref[:, c * cw:(c + 1) * cw] = _dot(a, w2b[:, c * cw:(c + 1) * cw]) + b2_ref[:, c * cw:(c + 1) * cw]


def _experts(bexp, nused, xs, w1, b1, w2, b2, l):
    rows, dm = xs.shape
    dff = w2.shape[2]
    nb = rows // MOE_BLOCK
    blk = lambda i, be, nu: (jnp.minimum(i, nu[0] - 1), 0)
    wsel = lambda i, be, nu: (l, be[jnp.minimum(i, nu[0] - 1)], 0, 0)
    grid_spec = pltpu.PrefetchScalarGridSpec(
        num_scalar_prefetch=2,
        grid=(nb,),
        in_specs=[
            pl.BlockSpec((MOE_BLOCK, dm), blk),
            pl.BlockSpec((None, None, dm, 2 * dff), wsel),
            pl.BlockSpec((None, None, 1, 2 * dff), wsel),
            pl.BlockSpec((None, None, dff, dm), wsel),
            pl.BlockSpec((None, None, 1, dm), wsel),
        ],
        out_specs=pl.BlockSpec((MOE_BLOCK, dm), lambda i, be, nu: (i, 0)),
        scratch_shapes=[pltpu.VMEM((dm, 2 * dff), BF16), pltpu.VMEM((dff, dm), BF16),
                        pltpu.VMEM((MOE_BLOCK, dff), BF16), pltpu.SMEM((1,), I32)],
    )
    return pl.pallas_call(
        functools.partial(_experts_kernel, dff=dff, dm=dm),
        grid_spec=grid_spec,
        out_shape=jax.ShapeDtypeStruct((rows, dm), F32),
        compiler_params=_cparams(("arbitrary",), VMEM_LIMIT),
        name="experts",
    )(bexp, nused, xs, w1, b1, w2, b2)


def _combine_kernel(dest_ref, dnext_ref, gate_ref, x_ref, g_ref, y_ref, o_ref, buf, sem, *, tm, group):
    i = pl.program_id(0)
    slot = lax.rem(i, 2)

    def gather(idx_ref, s):
        def issue(it, carry):
            for u in range(ISSUE_UNROLL):
                r = it * ISSUE_UNROLL + u
                for k in range(TOP_K):
                    d = idx_ref[r * TOP_K + k]
                    pltpu.make_async_copy(y_ref.at[pl.ds(d, 1)], buf.at[s, k, pl.ds(r, 1)], sem.at[s]).start()
            return carry

        lax.fori_loop(0, tm // ISSUE_UNROLL, issue, 0)

    @pl.when(i == 0)
    def _():
        gather(dest_ref, 0)

    @pl.when(i + 1 < pl.num_programs(0))
    def _():
        gather(dnext_ref, 1 - slot)

    for k in range(TOP_K):
        pltpu.make_async_copy(y_ref.at[pl.ds(0, tm)], buf.at[slot, k], sem.at[slot]).wait()
    gate = gate_ref[...]
    moe = gate[:, 0:1] * buf[slot, 0]
    for k in range(1, TOP_K):
        moe = moe + gate[:, k:k + 1] * buf[slot, k]
    o_ref[...] = x_ref[...] + _group_affine(moe, g_ref[...], None, group)


def _combine(dest1d, gate, x, modg, l, y, group):
    n, d = x.shape
    tm = _pick_tile(n, 256)
    tg = tm // group
    nt = n // tm
    return pl.pallas_call(
        functools.partial(_combine_kernel, tm=tm, group=group),
        grid=(nt,),
        in_specs=[pl.BlockSpec((tm * TOP_K,), lambda i: (i,), memory_space=pltpu.SMEM),
                  pl.BlockSpec((tm * TOP_K,), lambda i: (jnp.minimum(i + 1, nt - 1),), memory_space=pltpu.SMEM),
                  pl.BlockSpec((tm, LANE), lambda i: (i, 0)),
                  pl.BlockSpec((tm, d), lambda i: (i, 0)),
                  pl.BlockSpec((None, None, tg, 1, d), lambda i: (l, 5, i, 0, 0)),
                  pl.BlockSpec(memory_space=pl.ANY)],
        out_specs=pl.BlockSpec((tm, d), lambda i: (i, 0)),
        out_shape=jax.ShapeDtypeStruct((n, d), F32),
        scratch_shapes=[pltpu.VMEM((2, TOP_K, tm, d), F32), pltpu.SemaphoreType.DMA((2,))],
        compiler_params=_cparams(("arbitrary",), VMEM_LIMIT),
        name="combine",
    )(dest1d, dest1d, gate, x, modg, y)


def _final_norm_kernel(x_ref, g_ref, o_ref):
    o_ref[...] = _rms(x_ref[...], g_ref[...])


def _final_norm(x, g, row0, nrows):
    d = x.shape[1]
    tm = _pick_tile(math.gcd(row0, nrows) if row0 else nrows, 512)
    blk0 = row0 // tm
    return pl.pallas_call(
        _final_norm_kernel,
        grid=(nrows // tm,),
        in_specs=[pl.BlockSpec((tm, d), lambda i: (blk0 + i, 0)), pl.BlockSpec((1, d), lambda i: (0, 0))],
        out_specs=pl.BlockSpec((tm, d), lambda i: (i, 0)),
        out_shape=jax.ShapeDtypeStruct((nrows, d), F32),
        compiler_params=_cparams(("arbitrary",)),
        name="final_norm",
    )(x, g.reshape(1, d))


def _pad_heads(w, heads, dk):
    lead = w.shape[:-1]
    w = w.reshape(*lead, heads, dk)
    w = jnp.pad(w, [(0, 0)] * len(lead) + [(0, 0), (0, LANE - dk)])
    return w.reshape(*lead, heads * LANE)


def _pad_cols(w, width):
    return jnp.pad(w, [(0, 0)] * (w.ndim - 1) + [(0, width - w.shape[-1])])


EVEN_WIDTHS = (H_A * LANE, H_A * LANE, H_A * DV_A, H_A * DV_A, LANE, Q_LORA, KV_LORA, LANE)


def _even_weight(w):
    a_qk, a_v = H_A * DK_A, H_A * DV_A
    c = [0, a_qk, 2 * a_qk, 2 * a_qk + a_v, 2 * a_qk + 2 * a_v, 2 * a_qk + 2 * a_v + GATE_RANK]
    c.append(c[-1] + Q_LORA)
    c.append(c[-1] + KV_LORA)
    c.append(c[-1] + ROPE)
    parts = [
        _pad_heads(w[:, c[0]:c[1]], H_A, DK_A),
        _pad_heads(w[:, c[1]:c[2]], H_A, DK_A),
        w[:, c[2]:c[3]],
        w[:, c[3]:c[4]],
        _pad_cols(w[:, c[4]:c[5]], LANE),
        w[:, c[5]:c[6]],
        w[:, c[6]:c[7]],
        _pad_cols(w[:, c[7]:c[8]], LANE),
    ]
    return jnp.concatenate(parts, axis=1).astype(BF16)


def _uq_weight(w):
    w = w.reshape(Q_LORA, H_B, NOPE + ROPE)
    half = ROPE // 2
    nope = w[:, :, :NOPE].reshape(Q_LORA, H_B * NOPE)
    r1 = w[:, :, NOPE:NOPE + half].reshape(Q_LORA, H_B * half)
    r2 = w[:, :, NOPE + half:].reshape(Q_LORA, H_B * half)
    return jnp.concatenate([nope, r1, r2], axis=1).astype(BF16)


def _rope_perm():
    half = ROPE // 2
    r = jnp.arange(2 * LANE)
    second = r // LANE
    h = (r % LANE) // half
    i = r % half
    col = h * LANE + second * half + i
    return (col[:, None] == jnp.arange(H_B * LANE)[None, :]).astype(BF16)


def _rope_tables(pos):
    half = ROPE // 2
    inv_freq = jnp.exp(-math.log(ROPE_BASE) * jnp.arange(half, dtype=F32) / half)
    ang = pos.astype(F32)[:, None] * inv_freq[None, :]
    cos, sin = jnp.cos(ang), jnp.sin(ang)
    z = jnp.zeros((pos.shape[0], LANE - ROPE), F32)
    cosq, sinq = jnp.tile(cos, (1, LANE // half)), jnp.tile(sin, (1, LANE // half))
    cosk = jnp.concatenate([cos, cos, z], axis=1)
    sink = jnp.concatenate([sin, sin, z], axis=1)
    return cosq, sinq, cosk, sink


def kernel(x_prompt, x_sample, cache_mla_latent, cache_mla_krope, state_gla, state_hgrn, c_prompt, c_sample,
           w_ada, b_ada, norm1_g, norm2_g, w_in_even, w_gla_a2, b_gla_a, gla_norm_g, mla_q_norm_g, w_mla_uq,
           mla_kv_norm_g, w_mla_uk, w_mla_uv, w_out_even, w_in_odd, hgrn_lb, hgrn_norm_g, w_out_odd,
           w_router, b_router, w_e1, b_e1, w_e2, b_e2, final_norm_g):
    bp, tp, d = x_prompt.shape
    bs, ts, _ = x_sample.shape
    past = cache_mla_latent.shape[2]
    depth = w_ada.shape[0]
    nexp = w_router.shape[2]
    n_p, n_s = bp * tp, bs * ts
    n = n_p + n_s
    group = ts
    assert tp % group == 0 and group % 8 == 0 and tp % CHUNK == 0 and ts <= CHUNK
    assert (n * TOP_K) % MOE_BLOCK == 0
    n_even, n_odd = (depth + 1) // 2, depth // 2

    x = jnp.concatenate([x_prompt.reshape(n_p, d), x_sample.reshape(n_s, d)], axis=0)

    mod = _ada(jnp.concatenate([c_prompt, c_sample], axis=0), w_ada, b_ada)
    mod = mod.reshape(depth, bp + bs, 6, d).transpose(0, 2, 1, 3)
    modg = jnp.concatenate([jnp.repeat(mod[:, :, :bp], tp // group, axis=2), mod[:, :, bp:]], axis=2)
    modg = modg.reshape(depth, 6, n // group, 1, d)

    pos = jnp.concatenate([jnp.tile(jnp.arange(tp), bp), jnp.tile(jnp.arange(ts) + past, bs)])
    tabs = _rope_tables(pos)
    perm = _rope_perm()

    lb_soft = jax.nn.softmax(hgrn_lb.astype(F32), axis=0)
    lb_all = jnp.cumsum(lb_soft, axis=0) - lb_soft[0]
    gla_s0 = jnp.pad(state_gla, ((0, 0), (0, 0), (0, 0), (0, LANE - DK_A), (0, 0)))

    lat_p, kpe_p, gla_p, hgrn_p, lat_s, kpe_s, gla_s, hgrn_s = [], [], [], [], [], [], [], []
    rows = (-(-(n * TOP_K) // MOE_BLOCK)) * MOE_BLOCK + nexp * MOE_BLOCK
    nblocks = rows // MOE_BLOCK

    for l in range(depth):
        j = l // 2
        if l % 2 == 0:
            zq, zk, zv, zr, za, zcq, zckv, zkpe = _inproj(
                x, norm1_g[l], modg, l, _even_weight(w_in_even[j]), EVEN_WIDTHS, group)
            wa = _pad_heads(jnp.pad(w_gla_a2[j], ((0, LANE - GATE_RANK), (0, 0))), H_A, DK_A)
            wa_hi = wa.astype(BF16)
            wa_lo = (wa - wa_hi.astype(F32)).astype(BF16)
            ab = _pad_heads(b_gla_a[j].reshape(1, -1), H_A, DK_A)
            gn = gla_norm_g[j].reshape(1, DV_A)
            extras = (za, wa_hi, wa_lo, ab)
            oa_p, sp = _recurrence("gla", zq, zk, zv, zr, extras, gn, None, heads=H_A, nseq=bp,
                                   seqlen=tp, chunk=CHUNK, row0=0)
            oa_s, ss = _recurrence("gla", zq, zk, zv, zr, extras, gn, (gla_s0, j), heads=H_A, nseq=bs,
                                   seqlen=ts, chunk=ts, row0=n_p)
            gla_p.append(sp[:, :, :DK_A, :])
            gla_s.append(ss[:, :, :DK_A, :])

            wuk = w_mla_uk[j].transpose(1, 2, 0).astype(BF16)
            wuv = w_mla_uv[j].transpose(1, 0, 2).astype(BF16)
            wuvt = w_mla_uv[j].transpose(1, 2, 0).astype(BF16)
            qcat, kcat, lat, kpe, qt, latt = _mla_pre(
                zcq, zckv, zkpe, tabs, mla_q_norm_g[j].reshape(1, -1), mla_kv_norm_g[j].reshape(1, -1),
                _uq_weight(w_mla_uq[j]), wuk, perm)
            ob_p = _attn_prompt(qt, kcat, latt, wuvt, bp, tp)
            ob_s = _attn_sample(qcat, kcat, cache_mla_latent, cache_mla_krope, j, wuv, n_p)
            lat_p.append(lat[:n_p].reshape(bp, tp, KV_LORA))
            lat_s.append(lat[n_p:].reshape(bs, ts, KV_LORA))
            kpe_p.append(kpe[:n_p].reshape(bp, tp, ROPE))
            kpe_s.append(kpe[n_p:].reshape(bs, ts, ROPE))
            wo = w_out_even[j].astype(BF16)
            x = _outproj(x, modg, l, [oa_p, ob_p], [oa_s, ob_s], [wo[:H_A * DV_A], wo[H_A * DV_A:]], group)
        else:
            zq, zf, zi, zg = _inproj(x, norm1_g[l], modg, l, w_in_odd[j].astype(BF16),
                                     (H_C * DK_C,) * 2 + (H_C * DV_C,) * 2, group)
            extras = (lb_all[l].reshape(1, -1),)
            gn = hgrn_norm_g[j].reshape(1, DV_C)
            oc_p, sp = _recurrence("hgrn", zq, zf, zi, zg, extras, gn, None, heads=H_C, nseq=bp,
                                   seqlen=tp, chunk=CHUNK, row0=0)
            oc_s, ss = _recurrence("hgrn", zq, zf, zi, zg, extras, gn, (state_hgrn, j), heads=H_C, nseq=bs,
                                   seqlen=ts, chunk=ts, row0=n_p)
            hgrn_p.append(sp)
            hgrn_s.append(ss)
            x = _outproj(x, modg, l, [oc_p], [oc_s], [w_out_odd[j].astype(BF16)], group)

        wr = _pad_cols(w_router[l], LANE)
        wr_hi = wr.astype(BF16)
        wr_lo = (wr - wr_hi.astype(F32)).astype(BF16)
        br = _pad_cols(b_router[l].reshape(1, -1), LANE)
        h2, topi, gate = _router(x, norm2_g[l], modg, l, wr_hi, wr_lo, br, group, nexp)
        rank, cnt = _rank(topi)
        counts = cnt[0, :nexp].astype(I32)
        padded = (counts + MOE_BLOCK - 1) // MOE_BLOCK * MOE_BLOCK
        pends = jnp.cumsum(padded)
        pstarts = _pad_cols((pends - padded).astype(F32).reshape(1, -1), LANE)
        dest = _dest(topi, rank, pstarts)
        dest1d = dest[:, :TOP_K].reshape(n * TOP_K)
        bexp = jnp.clip(jnp.sum(pends[None, :] <= (jnp.arange(nblocks) * MOE_BLOCK)[:, None], axis=1),
                        0, nexp - 1).astype(I32)
        nused = (pends[-1:] // MOE_BLOCK).astype(I32)
        padinfo = jnp.stack([_pad_cols(pends, LANE), _pad_cols(padded - counts, LANE),
                             _pad_cols(nused, LANE)]).astype(I32)
        xs = _dispatch(dest1d, padinfo, h2, rows, nexp)
        y = _experts(bexp, nused, xs, w_e1, b_e1.reshape(depth, nexp, 1, -1), w_e2,
                     b_e2.reshape(depth, nexp, 1, -1), l)
        x = _combine(dest1d, gate, x, modg, l, y, group)

    y_p = _final_norm(x, final_norm_g, 0, n_p)
    y_s = _final_norm(x, final_norm_g, n_p, n_s)
    return (y_p.reshape(bp, tp, d), y_s.reshape(bs, ts, d),
            jnp.stack(lat_p), jnp.stack(kpe_p), jnp.stack(gla_p), jnp.stack(hgrn_p),
            jnp.stack(lat_s), jnp.stack(kpe_s), jnp.stack(gla_s), jnp.stack(hgrn_s))
```

```python
import functools
import math

import jax
import jax.numpy as jnp
from jax import lax
from jax.experimental import pallas as pl
from jax.experimental.pallas import tpu as pltpu

F32 = jnp.float32
BF16 = jnp.bfloat16
I32 = jnp.int32

EPS = 1e-6
CHUNK = 64
LANE = 128

H_A, DK_A, DV_A, GATE_RANK, GATE_TAU = 4, 64, 128, 16, 16.0
H_B, Q_LORA, KV_LORA, NOPE, ROPE, V_B = 4, 384, 256, 128, 64, 128
ROPE_BASE = 10000.0
MLA_SCALE = (NOPE + ROPE) ** -0.5
KCAT = KV_LORA + LANE
H_C, DK_C, DV_C = 8, 128, 128
TOP_K = 4
SWIGLU_LIMIT = 7.0
SWIGLU_ALPHA = 1.702
MOE_BLOCK = 512
ISSUE_UNROLL = 8
EXP_CLAMP = 80.0

VMEM_LIMIT = 56 * 1024 * 1024


def _cparams(sem, vmem=None):
    return pltpu.CompilerParams(dimension_semantics=sem, vmem_limit_bytes=vmem)


def _pick_tile(n, pref):
    t = pref
    while n % t:
        t //= 2
    return t


def _rms(x, g):
    return x * lax.rsqrt(jnp.mean(x * x, axis=-1, keepdims=True) + EPS) * g


def _group_affine(x, scale, shift, group):
    tm, d = x.shape
    x3 = x.reshape(tm // group, group, d)
    if scale is not None:
        x3 = x3 * scale
    if shift is not None:
        x3 = x3 + shift
    return x3.reshape(tm, d)


def _cumsum_rows(x):
    c, w = x.shape
    row = lax.broadcasted_iota(I32, (c, 1), 0)
    s = 1
    while s < c:
        if s % 8 == 0:
            shifted = jnp.concatenate([jnp.zeros((s, w), x.dtype), x[:c - s, :]], axis=0)
        else:
            shifted = jnp.where(row >= s, pltpu.roll(x, s, 0), 0.0)
        x = x + shifted
        s *= 2
    return x


def _dot(a, b):
    return jnp.dot(a, b, preferred_element_type=F32)


def _dot_nt(a, b):
    return lax.dot_general(a, b, (((1,), (1,)), ((), ())), preferred_element_type=F32)


def _dot_tn(a, b):
    return lax.dot_general(a, b, (((0,), (0,)), ((), ())), preferred_element_type=F32)


def _ada_kernel(c_ref, w_ref, b_ref, o_ref):
    c = c_ref[...]
    a = (c * jax.nn.sigmoid(c)).astype(BF16)
    o_ref[...] = _dot(a, w_ref[...].astype(BF16)) + b_ref[...]


def _ada(c_all, w_ada, b_ada):
    depth, d, n6 = w_ada.shape
    s = c_all.shape[0]
    tn = _pick_tile(n6, 1536)
    return pl.pallas_call(
        _ada_kernel,
        grid=(depth, n6 // tn),
        in_specs=[
            pl.BlockSpec((s, d), lambda l, j: (0, 0)),
            pl.BlockSpec((None, d, tn), lambda l, j: (l, 0, j)),
            pl.BlockSpec((None, 1, tn), lambda l, j: (l, 0, j)),
        ],
        out_specs=pl.BlockSpec((None, s, tn), lambda l, j: (l, 0, j)),
        out_shape=jax.ShapeDtypeStruct((depth, s, n6), F32),
        compiler_params=_cparams(("arbitrary", "arbitrary")),
        name="ada",
    )(c_all, w_ada, b_ada.reshape(depth, 1, n6))


def _inproj_kernel(x_ref, g_ref, sc_ref, sh_ref, w_ref, *out_refs, splits, group):
    h = _rms(x_ref[...], g_ref[...])
    h = _group_affine(h, 1.0 + sc_ref[...], sh_ref[...], group).astype(BF16)
    for o_ref, (c0, c1) in zip(out_refs, splits):
        o_ref[...] = _dot(h, w_ref[:, c0:c1]).astype(o_ref.dtype)


def _inproj(x, gain, modg, l, w, widths, group):
    n, d = x.shape
    tm = _pick_tile(n, 512)
    tg = tm // group
    splits, c = [], 0
    for wd in widths:
        splits.append((c, c + wd))
        c += wd
    mod_spec = lambda comp: pl.BlockSpec((None, None, tg, 1, d), lambda i: (l, comp, i, 0, 0))
    return pl.pallas_call(
        functools.partial(_inproj_kernel, splits=tuple(splits), group=group),
        grid=(n // tm,),
        in_specs=[
            pl.BlockSpec((tm, d), lambda i: (i, 0)),
            pl.BlockSpec((1, d), lambda i: (0, 0)),
            mod_spec(1),
            mod_spec(0),
            pl.BlockSpec((d, c), lambda i: (0, 0)),
        ],
        out_specs=[pl.BlockSpec((tm, wd), lambda i: (i, 0)) for wd in widths],
        out_shape=[jax.ShapeDtypeStruct((n, wd), F32) for wd in widths],
        compiler_params=_cparams(("arbitrary",), VMEM_LIMIT),
        name="inproj",
    )(x, gain.reshape(1, d), modg, modg, w)


REC_SEQS = 8


def _rec_kernel(*refs, mode, heads, chunk, nchunks, zero_init, nseqs):
    refs = list(refs)
    ntok = 5 if mode == "gla" else 4
    tok_refs = [refs[s * ntok:(s + 1) * ntok] for s in range(nseqs)]
    refs = refs[nseqs * ntok:]
    if mode == "gla":
        wah_ref, wal_ref, ab_ref, gn_ref = refs[:4]
        refs = refs[4:]
    else:
        lb_ref, gn_ref = refs[:2]
        refs = refs[2:]
    s0_ref = None
    if not zero_init:
        s0_ref = refs.pop(0)
    o_ref, sout_ref = refs[:2]
    st_scr = refs[2:]
    ci = pl.program_id(1)

    @pl.when(ci == 0)
    def _():
        for s in range(nseqs):
            for h in range(heads):
                if zero_init:
                    st_scr[s * heads + h][...] = jnp.zeros((LANE, LANE), F32)
                else:
                    st_scr[s * heads + h][...] = s0_ref[s, h].T

    row = lax.broadcasted_iota(I32, (chunk, chunk), 0)
    col = lax.broadcasted_iota(I32, (chunk, chunk), 1)
    causal = row >= col
    mid = chunk // 2 - 1

    for s in range(nseqs):
        if mode == "gla":
            q_ref, k_ref, v_ref, r_ref, a_ref = tok_refs[s]
            a = a_ref[...]
            a_hi = a.astype(BF16)
            a_lo = (a - a_hi.astype(F32)).astype(BF16)
            alog_all = (_dot(a_hi, wah_ref[...]) + _dot(a_lo, wah_ref[...]) + _dot(a_hi, wal_ref[...])
                        + ab_ref[...])
            g_all = jax.nn.log_sigmoid(alog_all) * (1.0 / GATE_TAU)
            fg_all = None
        else:
            q_ref, k_ref, v_ref, r_ref = tok_refs[s]
            lb = lb_ref[...]
            fg_all = lb + (1.0 - lb) * jax.nn.sigmoid(k_ref[...])
            g_all = jnp.log(fg_all)
        b_all = _cumsum_rows(g_all)

        for h in range(heads):
            sl = slice(h * LANE, (h + 1) * LANE)
            if mode == "gla":
                q = q_ref[:, sl] * (DK_A ** -0.5)
                k = k_ref[:, sl]
            else:
                qr = q_ref[:, sl]
                q = qr * jax.nn.sigmoid(qr)
                k = 1.0 - fg_all[:, sl]
            b = b_all[:, sl]
            b_last = b[chunk - 1:chunk, :]
            b_ref_row = b[mid:mid + 1, :]
            qt = (q * jnp.exp(jnp.minimum(b - b_ref_row, EXP_CLAMP))).astype(BF16)
            kt = (k * jnp.exp(jnp.minimum(b_ref_row - b, EXP_CLAMP))).astype(BF16)
            qs = (q * jnp.exp(b)).astype(BF16)
            ks = (k * jnp.exp(b_last - b)).astype(BF16)
            v = v_ref[:, sl].astype(BF16)
            att = jnp.where(causal, _dot_nt(qt, kt), 0.0).astype(BF16)
            st_ref = st_scr[s * heads + h]
            st = st_ref[...]
            o = _dot(att, v) + _dot_nt(qs, st.astype(BF16))
            st_ref[...] = st * jnp.exp(b_last) + _dot_tn(v, ks)
            rg = r_ref[:, sl]
            o = _rms(o, gn_ref[...]) * (rg * jax.nn.sigmoid(rg))
            o_ref[s, :, sl] = o.astype(o_ref.dtype)

    @pl.when(ci == nchunks - 1)
    def _():
        for s in range(nseqs):
            for h in range(heads):
                sout_ref[s, h] = st_scr[s * heads + h][...].T


def _recurrence(mode, q, k, v, r, extras, gnorm, s0, *, heads, nseq, seqlen, chunk, row0):
    cols = q.shape[1]
    nchunks = seqlen // chunk
    blk0 = row0 // chunk
    ns = REC_SEQS if nseq % REC_SEQS == 0 else 1
    full = lambda a: pl.BlockSpec(a.shape, lambda b, c: (0,) * a.ndim)
    in_specs, args = [], []
    for s in range(ns):
        tok = lambda b, c, s=s: (blk0 + (b * ns + s) * nchunks + c, 0)
        toks = [q, k, v, r] + ([extras[0]] if mode == "gla" else [])
        in_specs += [pl.BlockSpec((chunk, a.shape[1]), tok) for a in toks]
        args += toks
    consts = list(extras[1:]) if mode == "gla" else list(extras)
    consts.append(gnorm)
    in_specs += [full(a) for a in consts]
    args += consts
    zero_init = s0 is None
    if not zero_init:
        s0, s0_layer = s0
        in_specs.append(pl.BlockSpec((None, ns, heads, LANE, LANE), lambda b, c: (s0_layer, b, 0, 0, 0)))
        args.append(s0)
    o, st = pl.pallas_call(
        functools.partial(_rec_kernel, mode=mode, heads=heads, chunk=chunk, nchunks=nchunks,
                          zero_init=zero_init, nseqs=ns),
        grid=(nseq // ns, nchunks),
        in_specs=in_specs,
        out_specs=[pl.BlockSpec((ns, chunk, cols), lambda b, c: (b, c, 0)),
                   pl.BlockSpec((ns, heads, LANE, LANE), lambda b, c: (b, 0, 0, 0))],
        out_shape=[jax.ShapeDtypeStruct((nseq, seqlen, cols), BF16),
                   jax.ShapeDtypeStruct((nseq, heads, LANE, LANE), F32)],
        scratch_shapes=[pltpu.VMEM((LANE, LANE), F32) for _ in range(ns * heads)],
        compiler_params=_cparams(("arbitrary", "arbitrary")),
        name="recurrence_" + mode,
    )(*args)
    return o.reshape(nseq * seqlen, cols), st


def _mla_pre_kernel(cq_ref, ckv_ref, kpe_ref, cosq_ref, sinq_ref, cosk_ref, sink_ref, qg_ref, kvg_ref,
                    wuq_ref, wuk_ref, perm_ref, qcat_ref, kcat_ref, lat_ref, kpeo_ref, qt_ref, latt_ref, *, tq):
    tm = cq_ref.shape[0]
    cqn = _rms(cq_ref[...], qg_ref[...]).astype(BF16)
    qf = _dot(cqn, wuq_ref[...])
    off = H_B * NOPE
    x1 = qf[:, off:off + LANE]
    x2 = qf[:, off + LANE:off + 2 * LANE]
    cq, sq = cosq_ref[...], sinq_ref[...]
    o1 = (x1 * cq - x2 * sq) * MLA_SCALE
    o2 = (x2 * cq + x1 * sq) * MLA_SCALE
    pe = _dot(o1.astype(BF16), perm_ref[0:LANE, :]) + _dot(o2.astype(BF16), perm_ref[LANE:2 * LANE, :])
    for h in range(H_B):
        ql = _dot(qf[:, h * NOPE:(h + 1) * NOPE].astype(BF16), wuk_ref[h]) * MLA_SCALE
        peh = pe[:, h * LANE:(h + 1) * LANE]
        qcat_ref[h, :, 0:KV_LORA] = ql.astype(BF16)
        qcat_ref[h, :, KV_LORA:KCAT] = peh.astype(BF16)
        for jb in range(tm // tq):
            rows = slice(jb * tq, (jb + 1) * tq)
            cols = slice(h * tq, (h + 1) * tq)
            qt_ref[jb, 0:KV_LORA, cols] = ql[rows, :].T.astype(BF16)
            qt_ref[jb, KV_LORA:KCAT, cols] = peh[rows, :].T.astype(BF16)
    latn = _rms(ckv_ref[...], kvg_ref[...])
    lat_ref[...] = latn
    latt_ref[...] = latn.T.astype(BF16)
    x = kpe_ref[...]
    half = ROPE // 2
    lane = lax.broadcasted_iota(I32, x.shape, 1)
    rot = jnp.where(lane < half, -pltpu.roll(x, LANE - half, 1), pltpu.roll(x, half, 1))
    kro = x * cosk_ref[...] + rot * sink_ref[...]
    kpeo_ref[...] = kro[:, 0:ROPE]
    kcat_ref[:, 0:KV_LORA] = latn.astype(BF16)
    kcat_ref[:, KV_LORA:KCAT] = kro.astype(BF16)


ATTN_TQ = 8 * CHUNK


def _mla_pre(zcq, zckv, zkpe, tabs, qg, kvg, wuq, wuk, perm):
    n = zcq.shape[0]
    tm = _pick_tile(n, 512)
    tq = ATTN_TQ
    assert tm % tq == 0
    tokspec = lambda wd: pl.BlockSpec((tm, wd), lambda i: (i, 0))
    full = lambda a: pl.BlockSpec(a.shape, lambda i: (0,) * a.ndim)
    return pl.pallas_call(
        functools.partial(_mla_pre_kernel, tq=tq),
        grid=(n // tm,),
        in_specs=[tokspec(Q_LORA), tokspec(KV_LORA), tokspec(LANE)] + [tokspec(LANE)] * 4
                 + [full(qg), full(kvg), full(wuq), full(wuk), full(perm)],
        out_specs=[pl.BlockSpec((H_B, tm, KCAT), lambda i: (0, i, 0)), tokspec(KCAT),
                   tokspec(KV_LORA), tokspec(ROPE),
                   pl.BlockSpec((tm // tq, KCAT, H_B * tq), lambda i: (i, 0, 0)),
                   pl.BlockSpec((KV_LORA, tm), lambda i: (0, i))],
        out_shape=[jax.ShapeDtypeStruct((H_B, n, KCAT), BF16), jax.ShapeDtypeStruct((n, KCAT), BF16),
                   jax.ShapeDtypeStruct((n, KV_LORA), F32), jax.ShapeDtypeStruct((n, ROPE), F32),
                   jax.ShapeDtypeStruct((n // tq, KCAT, H_B * tq), BF16),
                   jax.ShapeDtypeStruct((KV_LORA, n), BF16)],
        compiler_params=_cparams(("arbitrary",)),
        name="mla_pre",
    )(zcq, zckv, zkpe, *tabs, qg, kvg, wuq, wuk, perm)


def _softmax_update(s, vals, m_scr, l_scr, acc_scr):
    m_prev = m_scr[...]
    m_new = jnp.maximum(m_prev, jnp.max(s, axis=1, keepdims=True))
    alpha = jnp.exp(m_prev - m_new)
    p = jnp.exp(s - m_new)
    l_scr[...] = alpha * l_scr[...] + jnp.sum(p, axis=1, keepdims=True)
    acc_scr[...] = alpha * acc_scr[...] + _dot(p.astype(BF16), vals)
    m_scr[...] = m_new


def _softmax_init(m_scr, l_scr, acc_scr):
    m_scr[...] = jnp.full(m_scr.shape, -jnp.inf, F32)
    l_scr[...] = jnp.zeros(l_scr.shape, F32)
    acc_scr[...] = jnp.zeros(acc_scr.shape, F32)


def _attn_finish(o_ref, wuv_ref, l_scr, acc_scr, tq):
    inv = 1.0 / l_scr[...]
    for h in range(H_B):
        rows = slice(h * tq, (h + 1) * tq)
        oh = (acc_scr[rows, :] * inv[rows, :]).astype(BF16)
        o_ref[:, h * V_B:(h + 1) * V_B] = _dot(oh, wuv_ref[h]).astype(o_ref.dtype)


def _attn_prompt_kernel(qt_ref, k_ref, latt_ref, wuvt_ref, o_ref, m_scr, l_scr, acc_scr, p_scr, *, tq, tk):
    qi = pl.program_id(1)
    cols = H_B * tq
    qt = qt_ref[...]
    m_scr[...] = jnp.full(m_scr.shape, -jnp.inf, F32)
    l_scr[...] = jnp.zeros(l_scr.shape, F32)
    acc_scr[...] = jnp.zeros(acc_scr.shape, F32)

    def block(start, masked):
        s = _dot(k_ref[pl.ds(start, tk), :], qt)
        if masked:
            tok = qi * tq + (lax.broadcasted_iota(I32, (1, cols), 1) & (tq - 1))
            limit = (tok // CHUNK + 1) * CHUNK
            key = start + lax.broadcasted_iota(I32, (tk, 1), 0)
            s = jnp.where(key < limit, s, -jnp.inf)
        m_prev = m_scr[...]
        m_new = jnp.maximum(m_prev, jnp.max(s, axis=0, keepdims=True))
        alpha = jnp.exp(m_prev - m_new)
        p = jnp.exp(s - m_new)
        l_scr[...] = alpha * l_scr[...] + jnp.sum(p, axis=0, keepdims=True)
        m_scr[...] = m_new
        p_scr[...] = p.astype(BF16)
        acc_scr[...] = alpha * acc_scr[...] + _dot(latt_ref[:, pl.ds(start, tk)], p_scr[...])

    nfull = (qi * tq) // tk

    def full(j, carry):
        block(pl.multiple_of(j * tk, tk), False)
        return carry

    lax.fori_loop(0, nfull, full, 0)
    block(pl.multiple_of(nfull * tk, tk), True)

    ot =(acc_scr[...] * (1.0 / l_scr[...])).astype(BF16)
    for h in range(H_B):
        oh = _dot(wuvt_ref[h], ot[:, h * tq:(h + 1) * tq])
        o_ref[:, h * V_B:(h + 1) * V_B] = oh.T.astype(o_ref.dtype)


def _attn_prompt(qt, kcat, latt, wuvt, nseq, seqlen):
    n = nseq * seqlen
    tq = ATTN_TQ
    tk = 512
    assert seqlen % tk == 0 and tk % tq == 0
    nq = seqlen // tq
    return pl.pallas_call(
        functools.partial(_attn_prompt_kernel, tq=tq, tk=tk),
        grid=(nseq, nq),
        in_specs=[
            pl.BlockSpec((None, KCAT, H_B * tq), lambda b, qi: (b * nq + qi, 0, 0)),
            pl.BlockSpec((seqlen, KCAT), lambda b, qi: (b, 0)),
            pl.BlockSpec((KV_LORA, seqlen), lambda b, qi: (0, b)),
            pl.BlockSpec(wuvt.shape, lambda b, qi: (0, 0, 0)),
        ],
        out_specs=pl.BlockSpec((tq, H_B * V_B), lambda b, qi: (b * nq + qi, 0)),
        out_shape=jax.ShapeDtypeStruct((n, H_B * V_B), BF16),
        scratch_shapes=[pltpu.VMEM((1, H_B * tq), F32), pltpu.VMEM((1, H_B * tq), F32),
                        pltpu.VMEM((KV_LORA, H_B * tq), F32), pltpu.VMEM((tk, H_B * tq), BF16)],
        compiler_params=_cparams(("arbitrary", "arbitrary")),
        name="attn_prompt",
    )(qt, kcat, latt, wuvt)


def _attn_sample_kernel(q_ref, plat_ref, pkpe_ref, knew_ref, wuv_ref, o_ref,
                        m_scr, l_scr, acc_scr, *, tq, nkp):
    ki = pl.program_id(1)

    @pl.when(ki == 0)
    def _():
        _softmax_init(m_scr, l_scr, acc_scr)

    q = q_ref[...].reshape(H_B * tq, KCAT)
    lat = plat_ref[...].astype(BF16)
    kpe = pkpe_ref[...].astype(BF16)
    s = _dot_nt(q[:, 0:KV_LORA], lat) + _dot_nt(q[:, KV_LORA:KV_LORA + ROPE], kpe)
    _softmax_update(s, lat, m_scr, l_scr, acc_scr)

    @pl.when(ki == nkp - 1)
    def _():
        kn = knew_ref[...]
        _softmax_update(_dot_nt(q, kn), kn[:, 0:KV_LORA], m_scr, l_scr, acc_scr)
        _attn_finish(o_ref, wuv_ref, l_scr, acc_scr, tq)


def _attn_sample(qcat, kcat, past_lat, past_kpe, layer, wuv, row0):
    _, nseq, past, _ = past_lat.shape
    tq = (kcat.shape[0] - row0) // nseq
    tkp = _pick_tile(past, 1024)
    nkp = past // tkp
    blk0 = row0 // tq
    return pl.pallas_call(
        functools.partial(_attn_sample_kernel, tq=tq, nkp=nkp),
        grid=(nseq, nkp),
        in_specs=[
            pl.BlockSpec((H_B, tq, KCAT), lambda b, ki: (0, blk0 + b, 0)),
            pl.BlockSpec((None, None, tkp, KV_LORA), lambda b, ki: (layer, b, ki, 0)),
            pl.BlockSpec((None, None, tkp, ROPE), lambda b, ki: (layer, b, ki, 0)),
            pl.BlockSpec((tq, KCAT), lambda b, ki: (blk0 + b, 0)),
            pl.BlockSpec(wuv.shape, lambda b, ki: (0, 0, 0)),
        ],
        out_specs=pl.BlockSpec((tq, H_B * V_B), lambda b, ki: (b, 0)),
        out_shape=jax.ShapeDtypeStruct((nseq * tq, H_B * V_B), BF16),
        scratch_shapes=[pltpu.VMEM((H_B * tq, 1), F32), pltpu.VMEM((H_B * tq, 1), F32),
                        pltpu.VMEM((H_B * tq, KV_LORA), F32)],
        compiler_params=_cparams(("arbitrary", "arbitrary")),
        name="attn_sample",
    )(qcat, past_lat, past_kpe, kcat, wuv)


def _outproj_kernel(*refs, nlhs, group, ptiles):
    x_ref, g_ref = refs[0], refs[1]
    lhs_p = refs[2:2 + nlhs]
    lhs_s = refs[2 + nlhs:2 + 2 * nlhs]
    ws = refs[2 + 2 * nlhs:2 + 3 * nlhs]
    o_ref = refs[2 + 3 * nlhs]
    is_prompt = pl.program_id(0) < ptiles
    acc = None
    for ap, asm, w in zip(lhs_p, lhs_s, ws):
        a = jnp.where(is_prompt, ap[...], asm[...])
        t = _dot(a, w[...])
        acc = t if acc is None else acc + t
    o_ref[...] = x_ref[...] + _group_affine(acc, g_ref[...], None, group)


def _outproj(x, modg, l, lhs_p, lhs_s, ws, group):
    n, d = x.shape
    n_p, n_s = lhs_p[0].shape[0], lhs_s[0].shape[0]
    tm = _pick_tile(math.gcd(n_p, n_s), 512)
    tg = tm // group
    ptiles = n_p // tm
    return pl.pallas_call(
        functools.partial(_outproj_kernel, nlhs=len(ws), group=group, ptiles=ptiles),
        grid=(n // tm,),
        in_specs=[pl.BlockSpec((tm, d), lambda i: (i, 0)),
                  pl.BlockSpec((None, None, tg, 1, d), lambda i: (l, 2, i, 0, 0))]
                 + [pl.BlockSpec((tm, a.shape[1]), lambda i: (jnp.minimum(i, ptiles - 1), 0)) for a in lhs_p]
                 + [pl.BlockSpec((tm, a.shape[1]), lambda i: (jnp.maximum(i - ptiles, 0), 0)) for a in lhs_s]
                 + [pl.BlockSpec(w.shape, lambda i: (0, 0)) for w in ws],
        out_specs=pl.BlockSpec((tm, d), lambda i: (i, 0)),
        out_shape=jax.ShapeDtypeStruct((n, d), F32),
        compiler_params=_cparams(("arbitrary",)),
        name="outproj",
    )(x, modg, *lhs_p, *lhs_s, *ws)


def _router_kernel(x_ref, g_ref, sc_ref, sh_ref, wh_ref, wl_ref, br_ref, h_ref, ti_ref, gt_ref,
                   *, group, nexp):
    h = _rms(x_ref[...], g_ref[...])
    h = _group_affine(h, 1.0 + sc_ref[...], sh_ref[...], group)
    h_ref[...] = h
    h_hi = h.astype(BF16)
    h_lo = (h - h_hi.astype(F32)).astype(BF16)
    logits = _dot(h_hi, wh_ref[...]) + _dot(h_lo, wh_ref[...]) + _dot(h_hi, wl_ref[...]) + br_ref[...]
    lane = lax.broadcasted_iota(I32, logits.shape, 1)
    lane_f = lane.astype(F32)
    cur = jnp.where(lane < nexp, logits, -jnp.inf)
    tops, idxs = [], []
    for _ in range(TOP_K):
        m = jnp.max(cur, axis=1, keepdims=True)
        i = jnp.min(jnp.where(cur == m, lane_f, float(LANE)), axis=1, keepdims=True)
        cur = jnp.where(lane_f == i, -jnp.inf, cur)
        tops.append(m)
        idxs.append(i.astype(I32))
    es = [jnp.exp(t - tops[0]) for t in tops]
    inv = 1.0 / (es[0] + es[1] + es[2] + es[3])
    ti = jnp.zeros(logits.shape, I32)
    gt = jnp.zeros(logits.shape, F32)
    for k in range(TOP_K):
        ti = jnp.where(lane == k, idxs[k], ti)
        gt = jnp.where(lane == k, es[k] * inv, gt)
    ti_ref[...] = ti
    gt_ref[...] = gt


def _router(x, gain, modg, l, wr_hi, wr_lo, br, group, nexp):
    n, d = x.shape
    tm = _pick_tile(n, 512)
    tg = tm // group
    mod_spec = lambda comp: pl.BlockSpec((None, None, tg, 1, d), lambda i: (l, comp, i, 0, 0))
    full = lambda a: pl.BlockSpec(a.shape, lambda i: (0,) * a.ndim)
    tok = lambda wd: pl.BlockSpec((tm, wd), lambda i: (i, 0))
    return pl.pallas_call(
        functools.partial(_router_kernel, group=group, nexp=nexp),
        grid=(n // tm,),
        in_specs=[tok(d), pl.BlockSpec((1, d), lambda i: (0, 0)), mod_spec(4), mod_spec(3),
                  full(wr_hi), full(wr_lo), full(br)],
        out_specs=[tok(d), tok(LANE), tok(LANE)],
        out_shape=[jax.ShapeDtypeStruct((n, d), F32), jax.ShapeDtypeStruct((n, LANE), I32),
                   jax.ShapeDtypeStruct((n, LANE), F32)],
        compiler_params=_cparams(("arbitrary",)),
        name="router",
    )(x, gain.reshape(1, d), modg, modg, wr_hi, wr_lo, br)


def _rank_kernel(ti_ref, rank_ref, cnt_ref, carry_scr, *, tm):
    i = pl.program_id(0)

    @pl.when(i == 0)
    def _():
        carry_scr[...] = jnp.zeros(carry_scr.shape, F32)

    ti = ti_ref[...]
    lane = lax.broadcasted_iota(I32, ti.shape, 1)
    sel = [lane == ti[:, k:k + 1] for k in range(TOP_K)]
    hot = jnp.zeros(ti.shape, F32)
    for s in sel:
        hot = hot + jnp.where(s, 1.0, 0.0)
    row = lax.broadcasted_iota(I32, (tm, tm), 0)
    col = lax.broadcasted_iota(I32, (tm, tm), 1)
    strict = jnp.where(row > col, 1.0, 0.0).astype(BF16)
    before = _dot(strict, hot.astype(BF16)) + carry_scr[0:1, :]
    rank = jnp.zeros(ti.shape, F32)
    for k in range(TOP_K):
        rk = jnp.sum(jnp.where(sel[k], before, 0.0), axis=1, keepdims=True)
        rank = jnp.where(lane == k, rk, rank)
    rank_ref[...] = rank.astype(I32)
    carry_scr[...] = carry_scr[...] + jnp.sum(hot, axis=0, keepdims=True)
    cnt_ref[...] = carry_scr[...]


def _rank(topi):
    n = topi.shape[0]
    tm = _pick_tile(n, 512)
    return pl.pallas_call(
        functools.partial(_rank_kernel, tm=tm),
        grid=(n // tm,),
        in_specs=[pl.BlockSpec((tm, LANE), lambda i: (i, 0))],
        out_specs=[pl.BlockSpec((tm, LANE), lambda i: (i, 0)), pl.BlockSpec((8, LANE), lambda i: (0, 0))],
        out_shape=[jax.ShapeDtypeStruct((n, LANE), I32), jax.ShapeDtypeStruct((8, LANE), F32)],
        scratch_shapes=[pltpu.VMEM((8, LANE), F32)],
        compiler_params=_cparams(("arbitrary",)),
        name="rank",
    )(topi)


def _dest_kernel(ti_ref, rank_ref, ps_ref, d_ref):
    ti = ti_ref[...]
    lane = lax.broadcasted_iota(I32, ti.shape, 1)
    ps = ps_ref[...]
    dest = rank_ref[...]
    for k in range(TOP_K):
        base = jnp.sum(jnp.where(lane == ti[:, k:k + 1], ps, 0.0), axis=1, keepdims=True).astype(I32)
        dest = dest + jnp.where(lane == k, base, 0)
    d_ref[...] = dest


def _dest(topi, rank, pstarts_row):
    n = topi.shape[0]
    tm = _pick_tile(n, 512)
    tok = pl.BlockSpec((tm, LANE), lambda i: (i, 0))
    return pl.pallas_call(
        _dest_kernel,
        grid=(n // tm,),
        in_specs=[tok, tok, pl.BlockSpec((1, LANE), lambda i: (0, 0))],
        out_specs=tok,
        out_shape=jax.ShapeDtypeStruct((n, LANE), I32),
        compiler_params=_cparams(("arbitrary",)),
        name="dest",
    )(topi, rank, pstarts_row)


def _dispatch_kernel(dest_ref, pad_ref, h_ref, xs_ref, hbuf, zbuf, sem, zsem, *, tm, nexp, nblocks):
    i = pl.program_id(0)
    slot = lax.rem(i, 2)
    hbuf[slot] = h_ref[...]

    def issue(it, carry):
        for u in range(ISSUE_UNROLL):
            r = it * ISSUE_UNROLL + u
            for k in range(TOP_K):
                d = dest_ref[r * TOP_K + k]
                pltpu.make_async_copy(hbuf.at[slot, pl.ds(r, 1)], xs_ref.at[pl.ds(d, 1)], sem.at[slot]).start()
        return carry

    lax.fori_loop(0, tm // ISSUE_UNROLL, issue, 0)

    def retire(s):
        for _ in range(TOP_K):
            pltpu.make_async_copy(hbuf.at[s], xs_ref.at[pl.ds(0, tm)], sem.at[s]).wait()

    @pl.when(i >= 1)
    def _():
        retire(1 - slot)

    @pl.when(i == pl.num_programs(0) - 1)
    def _():
        retire(slot)
        zbuf[...] = jnp.zeros(zbuf.shape, F32)

        def fill_expert(e, carry):
            end = pad_ref[0, e]
            npad = pad_ref[1, e]
            p = MOE_BLOCK // 2
            while p >= 1:
                bit = npad & p
                end = end - bit

                @pl.when(bit != 0)
                def _(end=end, p=p):
                    if p >= 8:
                        start = pl.multiple_of(end, 8)
                        pltpu.make_async_copy(zbuf.at[pl.ds(0, p)], xs_ref.at[pl.ds(start, p)], zsem).start()
                    else:
                        for q in range(p):
                            pltpu.make_async_copy(zbuf.at[pl.ds(0, 1)], xs_ref.at[pl.ds(end + q, 1)],
                                                  zsem).start()

                p //= 2
            return carry

        lax.fori_loop(0, nexp, fill_expert, 0)
        nused = pad_ref[2, 0]

        def fill_tail(b, carry):
            @pl.when(b >= nused)
            def _():
                start = pl.multiple_of(b * MOE_BLOCK, MOE_BLOCK)
                pltpu.make_async_copy(zbuf, xs_ref.at[pl.ds(start, MOE_BLOCK)], zsem).start()

            return carry

        lax.fori_loop(0, nblocks, fill_tail, 0)
        for _ in range(nexp):
            pltpu.make_async_copy(zbuf, xs_ref.at[pl.ds(0, MOE_BLOCK)], zsem).wait()


def _dispatch(dest1d, padinfo, h, rows, nexp):
    n, d = h.shape
    tm = _pick_tile(n, 256)
    return pl.pallas_call(
        functools.partial(_dispatch_kernel, tm=tm, nexp=nexp, nblocks=rows // MOE_BLOCK),
        grid=(n // tm,),
        in_specs=[pl.BlockSpec((tm * TOP_K,), lambda i: (i,), memory_space=pltpu.SMEM),
                  pl.BlockSpec(memory_space=pltpu.SMEM),
                  pl.BlockSpec((tm, d), lambda i: (i, 0))],
        out_specs=pl.BlockSpec(memory_space=pl.ANY),
        out_shape=jax.ShapeDtypeStruct((rows, d), F32),
        scratch_shapes=[pltpu.VMEM((2, tm, d), F32), pltpu.VMEM((MOE_BLOCK, d), F32),
                        pltpu.SemaphoreType.DMA((2,)), pltpu.SemaphoreType.DMA(())],
        compiler_params=pltpu.CompilerParams(dimension_semantics=("arbitrary",), has_side_effects=True),
        name="dispatch",
    )(dest1d, padinfo, h)


def _experts_kernel(bexp_ref, nused_ref, xs_ref, w1_ref, b1_ref, w2_ref, b2_ref, y_ref,
                    w1b, w2b, act_scr, prev_scr, *, dff, dm):
    i = pl.program_id(0)
    nu = nused_ref[0]
    e = bexp_ref[jnp.minimum(i, nu - 1)]

    @pl.when(i == 0)
    def _():
        prev_scr[0] = -1

    @pl.when(i >= nu)
    def _():
        y_ref[...] = jnp.zeros(y_ref.shape, F32)

    @pl.when(i < nu)
    def _():
        @pl.when(e != prev_scr[0])
        def _():
            rows = 128

            def cast1(c, carry):
                r0 = pl.multiple_of(c * rows, rows)
                w1b[pl.ds(r0, rows), :] = w1_ref[pl.ds(r0, rows), :].astype(BF16)
                return carry

            def cast2(c, carry):
                r0 = pl.multiple_of(c * rows, rows)
                w2b[pl.ds(r0, rows), :] = w2_ref[pl.ds(r0, rows), :].astype(BF16)
                return carry

            lax.fori_loop(0, dm // rows, cast1, 0)
            lax.fori_loop(0, dff // rows, cast2, 0)
            prev_scr[0] = e

        x = xs_ref[...].astype(BF16)
        cw = 256
        for c in range(dff // cw):
            gt = _dot(x, w1b[:, c * cw:(c + 1) * cw]) + b1_ref[:, c * cw:(c + 1) * cw]
            up = _dot(x, w1b[:, dff + c * cw:dff + (c + 1) * cw]) + b1_ref[:, dff + c * cw:dff + (c + 1) * cw]
            gt = jnp.minimum(gt, SWIGLU_LIMIT)
            up = jnp.clip(up, -SWIGLU_LIMIT, SWIGLU_LIMIT)
            act = gt * jax.nn.sigmoid(gt * SWIGLU_ALPHA) * (up + 1.0)
            act_scr[:, c * cw:(c + 1) * cw] = act.astype(BF16)
        a = act_scr[...]
        for c in range(dm // cw):
            y_ref[:, c * cw:(c + 1) * cw] = _dot(a, w2b[:, c * cw:(c + 1) * cw]) + b2_ref[:, c * cw:(c + 1) * cw]


def _experts(bexp, nused, xs, w1, b1, w2, b2, l):
    rows, dm = xs.shape
    dff = w2.shape[2]
    nb = rows // MOE_BLOCK
    blk = lambda i, be, nu: (jnp.minimum(i, nu[0] - 1), 0)
    wsel = lambda i, be, nu: (l, be[jnp.minimum(i, nu[0] - 1)], 0, 0)
    grid_spec = pltpu.PrefetchScalarGridSpec(
        num_scalar_prefetch=2,
        grid=(nb,),
        in_specs=[
            pl.BlockSpec((MOE_BLOCK, dm), blk),
            pl.BlockSpec((None, None, dm, 2 * dff), wsel),
            pl.BlockSpec((None, None, 1, 2 * dff), wsel),
            pl.BlockSpec((None, None, dff, dm), wsel),
            pl.BlockSpec((None, None, 1, dm), wsel),
        ],
        out_specs=pl.BlockSpec((MOE_BLOCK, dm), lambda i, be, nu: (i, 0)),
        scratch_shapes=[pltpu.VMEM((dm, 2 * dff), BF16), pltpu.VMEM((dff, dm), BF16),
                        pltpu.VMEM((MOE_BLOCK, dff), BF16), pltpu.SMEM((1,), I32)],
    )
    return pl.pallas_call(
        functools.partial(_experts_kernel, dff=dff, dm=dm),
        grid_spec=grid_spec,
        out_shape=jax.ShapeDtypeStruct((rows, dm), F32),
        compiler_params=_cparams(("arbitrary",), VMEM_LIMIT),
        name="experts",
    )(bexp, nused, xs, w1, b1, w2, b2)


def _combine_kernel(dest_ref, dnext_ref, gate_ref, x_ref, g_ref, y_ref, o_ref, buf, sem, *, tm, group):
    i = pl.program_id(0)
    slot = lax.rem(i, 2)

    def gather(idx_ref, s):
        def issue(it, carry):
            for u in range(ISSUE_UNROLL):
                r = it * ISSUE_UNROLL + u
                for k in range(TOP_K):
                    d = idx_ref[r * TOP_K + k]
                    pltpu.make_async_copy(y_ref.at[pl.ds(d, 1)], buf.at[s, k, pl.ds(r, 1)], sem.at[s]).start()
            return carry

        lax.fori_loop(0, tm // ISSUE_UNROLL, issue, 0)

    @pl.when(i == 0)
    def _():
        gather(dest_ref, 0)

    @pl.when(i + 1 < pl.num_programs(0))
    def _():
        gather(dnext_ref, 1 - slot)

    for k in range(TOP_K):
        pltpu.make_async_copy(y_ref.at[pl.ds(0, tm)], buf.at[slot, k], sem.at[slot]).wait()
    gate = gate_ref[...]
    moe = gate[:, 0:1] * buf[slot, 0]
    for k in range(1, TOP_K):
        moe = moe + gate[:, k:k + 1] * buf[slot, k]
    o_ref[...] = x_ref[...] + _group_affine(moe, g_ref[...], None, group)


def _combine(dest1d, gate, x, modg, l, y, group):
    n, d = x.shape
    tm = _pick_tile(n, 256)
    tg = tm // group
    nt = n // tm
    return pl.pallas_call(
        functools.partial(_combine_kernel, tm=tm, group=group),
        grid=(nt,),
        in_specs=[pl.BlockSpec((tm * TOP_K,), lambda i: (i,), memory_space=pltpu.SMEM),
                  pl.BlockSpec((tm * TOP_K,), lambda i: (jnp.minimum(i + 1, nt - 1),), memory_space=pltpu.SMEM),
                  pl.BlockSpec((tm, LANE), lambda i: (i, 0)),
                  pl.BlockSpec((tm, d), lambda i: (i, 0)),
                  pl.BlockSpec((None, None, tg, 1, d), lambda i: (l, 5, i, 0, 0)),
                  pl.BlockSpec(memory_space=pl.ANY)],
        out_specs=pl.BlockSpec((tm, d), lambda i: (i, 0)),
        out_shape=jax.ShapeDtypeStruct((n, d), F32),
        scratch_shapes=[pltpu.VMEM((2, TOP_K, tm, d), F32), pltpu.SemaphoreType.DMA((2,))],
        compiler_params=_cparams(("arbitrary",), VMEM_LIMIT),
        name="combine",
    )(dest1d, dest1d, gate, x, modg, y)


def _final_norm_kernel(x_ref, g_ref, o_ref):
    o_ref[...] = _rms(x_ref[...], g_ref[...])


def _final_norm(x, g, row0, nrows):
    d = x.shape[1]
    tm = _pick_tile(math.gcd(row0, nrows) if row0 else nrows, 512)
    blk0 = row0 // tm
    return pl.pallas_call(
        _final_norm_kernel,
        grid=(nrows // tm,),
        in_specs=[pl.BlockSpec((tm, d), lambda i: (blk0 + i, 0)), pl.BlockSpec((1, d), lambda i: (0, 0))],
        out_specs=pl.BlockSpec((tm, d), lambda i: (i, 0)),
        out_shape=jax.ShapeDtypeStruct((nrows, d), F32),
        compiler_params=_cparams(("arbitrary",)),
        name="final_norm",
    )(x, g.reshape(1, d))


def _pad_heads(w, heads, dk):
    lead = w.shape[:-1]
    w = w.reshape(*lead, heads, dk)
    w = jnp.pad(w, [(0, 0)] * len(lead) + [(0, 0), (0, LANE - dk)])
    return w.reshape(*lead, heads * LANE)


def _pad_cols(w, width):
    return jnp.pad(w, [(0, 0)] * (w.ndim - 1) + [(0, width - w.shape[-1])])


EVEN_WIDTHS = (H_A * LANE, H_A * LANE, H_A * DV_A, H_A * DV_A, LANE, Q_LORA, KV_LORA, LANE)


def _even_weight(w):
    a_qk, a_v = H_A * DK_A, H_A * DV_A
    c = [0, a_qk, 2 * a_qk, 2 * a_qk + a_v, 2 * a_qk + 2 * a_v, 2 * a_qk + 2 * a_v + GATE_RANK]
    c.append(c[-1] + Q_LORA)
    c.append(c[-1] + KV_LORA)
    c.append(c[-1] + ROPE)
    parts = [
        _pad_heads(w[:, c[0]:c[1]], H_A, DK_A),
        _pad_heads(w[:, c[1]:c[2]], H_A, DK_A),
        w[:, c[2]:c[3]],
        w[:, c[3]:c[4]],
        _pad_cols(w[:, c[4]:c[5]], LANE),
        w[:, c[5]:c[6]],
        w[:, c[6]:c[7]],
        _pad_cols(w[:, c[7]:c[8]], LANE),
    ]
    return jnp.concatenate(parts, axis=1).astype(BF16)


def _uq_weight(w):
    w = w.reshape(Q_LORA, H_B, NOPE + ROPE)
    half = ROPE // 2
    nope = w[:, :, :NOPE].reshape(Q_LORA, H_B * NOPE)
    r1 = w[:, :, NOPE:NOPE + half].reshape(Q_LORA, H_B * half)
    r2 = w[:, :, NOPE + half:].reshape(Q_LORA, H_B * half)
    return jnp.concatenate([nope, r1, r2], axis=1).astype(BF16)


def _rope_perm():
    half = ROPE // 2
    r = jnp.arange(2 * LANE)
    second = r // LANE
    h = (r % LANE) // half
    i = r % half
    col = h * LANE + second * half + i
    return (col[:, None] == jnp.arange(H_B * LANE)[None, :]).astype(BF16)


def _rope_tables(segments):
    half = ROPE // 2
    inv_freq = jnp.exp(-math.log(ROPE_BASE) * jnp.arange(half, dtype=F32) / half)
    cos, sin = [], []
    for pos0, length, reps in segments:
        ang = (jnp.arange(length) + pos0).astype(F32)[:, None] * inv_freq[None, :]
        cos.append(jnp.tile(jnp.cos(ang), (reps, 1)))
        sin.append(jnp.tile(jnp.sin(ang), (reps, 1)))
    cos, sin = jnp.concatenate(cos, axis=0), jnp.concatenate(sin, axis=0)
    z = jnp.zeros((cos.shape[0], LANE - ROPE), F32)
    cosq, sinq = jnp.tile(cos, (1, LANE // half)), jnp.tile(sin, (1, LANE // half))
    cosk = jnp.concatenate([cos, cos, z], axis=1)
    sink = jnp.concatenate([sin, sin, z], axis=1)
    return cosq, sinq, cosk, sink


def kernel(x_prompt, x_sample, cache_mla_latent, cache_mla_krope, state_gla, state_hgrn, c_prompt, c_sample,
           w_ada, b_ada, norm1_g, norm2_g, w_in_even, w_gla_a2, b_gla_a, gla_norm_g, mla_q_norm_g, w_mla_uq,
           mla_kv_norm_g, w_mla_uk, w_mla_uv, w_out_even, w_in_odd, hgrn_lb, hgrn_norm_g, w_out_odd,
           w_router, b_router, w_e1, b_e1, w_e2, b_e2, final_norm_g):
    bp, tp, d = x_prompt.shape
    bs, ts, _ = x_sample.shape
    past = cache_mla_latent.shape[2]
    depth = w_ada.shape[0]
    nexp = w_router.shape[2]
    n_p, n_s = bp * tp, bs * ts
    n = n_p + n_s
    group = ts
    assert tp % group == 0 and group % 8 == 0 and tp % CHUNK == 0 and ts <= CHUNK
    assert (n * TOP_K) % MOE_BLOCK == 0
    n_even, n_odd = (depth + 1) // 2, depth // 2

    x = jnp.concatenate([x_prompt.reshape(n_p, d), x_sample.reshape(n_s, d)], axis=0)

    mod = _ada(jnp.concatenate([c_prompt, c_sample], axis=0), w_ada, b_ada)
    mod = mod.reshape(depth, bp + bs, 6, d).transpose(0, 2, 1, 3)
    modg = jnp.concatenate([jnp.repeat(mod[:, :, :bp], tp // group, axis=2), mod[:, :, bp:]], axis=2)
    modg = modg.reshape(depth, 6, n // group, 1, d)

    tabs = _rope_tables([(0, tp, bp), (past, ts, bs)])
    perm = _rope_perm()

    lb_soft = jax.nn.softmax(hgrn_lb.astype(F32), axis=0)
    lb_all = jnp.cumsum(lb_soft, axis=0) - lb_soft[0]
    gla_s0 = jnp.pad(state_gla, ((0, 0), (0, 0), (0, 0), (0, LANE - DK_A), (0, 0)))

    lat_p, kpe_p, gla_p, hgrn_p, lat_s, kpe_s, gla_s, hgrn_s = [], [], [], [], [], [], [], []
    rows = (-(-(n * TOP_K) // MOE_BLOCK)) * MOE_BLOCK + nexp * MOE_BLOCK
    nblocks = rows // MOE_BLOCK

    for l in range(depth):
        j = l // 2
        if l % 2 == 0:
            zq, zk, zv, zr, za, zcq, zckv, zkpe = _inproj(
                x, norm1_g[l], modg, l, _even_weight(w_in_even[j]), EVEN_WIDTHS, group)
            wa = _pad_heads(jnp.pad(w_gla_a2[j], ((0, LANE - GATE_RANK), (0, 0))), H_A, DK_A)
            wa_hi = wa.astype(BF16)
            wa_lo = (wa - wa_hi.astype(F32)).astype(BF16)
            ab = _pad_heads(b_gla_a[j].reshape(1, -1), H_A, DK_A)
            gn = gla_norm_g[j].reshape(1, DV_A)
            extras = (za, wa_hi, wa_lo, ab)
            oa_p, sp = _recurrence("gla", zq, zk, zv, zr, extras, gn, None, heads=H_A, nseq=bp,
                                   seqlen=tp, chunk=CHUNK, row0=0)
            oa_s, ss = _recurrence("gla", zq, zk, zv, zr, extras, gn, (gla_s0, j), heads=H_A, nseq=bs,
                                   seqlen=ts, chunk=ts, row0=n_p)
            gla_p.append(sp[:, :, :DK_A, :])
            gla_s.append(ss[:, :, :DK_A, :])

            wuk = w_mla_uk[j].transpose(1, 2, 0).astype(BF16)
            wuv = w_mla_uv[j].transpose(1, 0, 2).astype(BF16)
            wuvt = w_mla_uv[j].transpose(1, 2, 0).astype(BF16)
            qcat, kcat, lat, kpe, qt, latt = _mla_pre(
                zcq, zckv, zkpe, tabs, mla_q_norm_g[j].reshape(1, -1), mla_kv_norm_g[j].reshape(1, -1),
                _uq_weight(w_mla_uq[j]), wuk, perm)
            ob_p = _attn_prompt(qt, kcat, latt, wuvt, bp, tp)
            ob_s = _attn_sample(qcat, kcat, cache_mla_latent, cache_mla_krope, j, wuv, n_p)
            lat_p.append(lat[:n_p].reshape(bp, tp, KV_LORA))
            lat_s.append(lat[n_p:].reshape(bs, ts, KV_LORA))
            kpe_p.append(kpe[:n_p].reshape(bp, tp, ROPE))
            kpe_s.append(kpe[n_p:].reshape(bs, ts, ROPE))
            wo = w_out_even[j].astype(BF16)
            x = _outproj(x, modg, l, [oa_p, ob_p], [oa_s, ob_s], [wo[:H_A * DV_A], wo[H_A * DV_A:]], group)
        else:
            zq, zf, zi, zg = _inproj(x, norm1_g[l], modg, l, w_in_odd[j].astype(BF16),
                                     (H_C * DK_C,) * 2 + (H_C * DV_C,) * 2, group)
            extras = (lb_all[l].reshape(1, -1),)
            gn = hgrn_norm_g[j].reshape(1, DV_C)
            oc_p, sp = _recurrence("hgrn", zq, zf, zi, zg, extras, gn, None, heads=H_C, nseq=bp,
                                   seqlen=tp, chunk=CHUNK, row0=0)
            oc_s, ss = _recurrence("hgrn", zq, zf, zi, zg, extras, gn, (state_hgrn, j), heads=H_C, nseq=bs,
                                   seqlen=ts, chunk=ts, row0=n_p)
            hgrn_p.append(sp)
            hgrn_s.append(ss)
            x = _outproj(x, modg, l, [oc_p], [oc_s], [w_out_odd[j].astype(BF16)], group)

        wr = _pad_cols(w_router[l], LANE)
        wr_hi = wr.astype(BF16)
        wr_lo = (wr - wr_hi.astype(F32)).astype(BF16)
        br = _pad_cols(b_router[l].reshape(1, -1), LANE)
        h2, topi, gate = _router(x, norm2_g[l], modg, l, wr_hi, wr_lo, br, group, nexp)
        rank, cnt = _rank(topi)
        counts = cnt[0, :nexp].astype(I32)
        padded = (counts + MOE_BLOCK - 1) // MOE_BLOCK * MOE_BLOCK
        pends = jnp.cumsum(padded)
        pstarts = _pad_cols((pends - padded).astype(F32).reshape(1, -1), LANE)
        dest = _dest(topi, rank, pstarts)
        dest1d = dest[:, :TOP_K].reshape(n * TOP_K)
        bexp = jnp.clip(jnp.sum(pends[None, :] <= (jnp.arange(nblocks) * MOE_BLOCK)[:, None], axis=1),
                        0, nexp - 1).astype(I32)
        nused = (pends[-1:] // MOE_BLOCK).astype(I32)
        padinfo = jnp.stack([_pad_cols(pends, LANE), _pad_cols(padded - counts, LANE),
                             _pad_cols(nused, LANE)]).astype(I32)
        xs = _dispatch(dest1d, padinfo, h2, rows, nexp)
        y = _experts(bexp, nused, xs, w_e1, b_e1.reshape(depth, nexp, 1, -1), w_e2,
                     b_e2.reshape(depth, nexp, 1, -1), l)
        x = _combine(dest1d, gate, x, modg, l, y, group)

    y_p = _final_norm(x, final_norm_g, 0, n_p)
    y_s = _final_norm(x, final_norm_g, n_p, n_s)
    return (y_p.reshape(bp, tp, d), y_s.reshape(bs, ts, d),
            jnp.stack(lat_p), jnp.stack(kpe_p), jnp.stack(gla_p), jnp.stack(hgrn_p),
            jnp.stack(lat_s), jnp.stack(kpe_s), jnp.stack(gla_s), jnp.stack(hgrn_s))
```

```python
import functools
import math

import jax
import jax.numpy as jnp
from jax import lax
from jax.experimental import pallas as pl
from jax.experimental.pallas import tpu as pltpu

F32 = jnp.float32
BF16 = jnp.bfloat16
I32 = jnp.int32

EPS = 1e-6
CHUNK = 64
LANE = 128

H_A, DK_A, DV_A, GATE_RANK, GATE_TAU = 4, 64, 128, 16, 16.0
H_B, Q_LORA, KV_LORA, NOPE, ROPE, V_B = 4, 384, 256, 128, 64, 128
ROPE_BASE = 10000.0
MLA_SCALE = (NOPE + ROPE) ** -0.5
KCAT = KV_LORA + LANE
H_C, DK_C, DV_C = 8, 128, 128
TOP_K = 4
SWIGLU_LIMIT = 7.0
SWIGLU_ALPHA = 1.702
MOE_BLOCK = 512
ISSUE_UNROLL = 8
EXP_CLAMP = 80.0

VMEM_LIMIT = 56 * 1024 * 1024


def _cparams(sem, vmem=None):
    return pltpu.CompilerParams(dimension_semantics=sem, vmem_limit_bytes=vmem)


def _pick_tile(n, pref):
    t = pref
    while n % t:
        t //= 2
    return t


def _rms(x, g):
    return x * lax.rsqrt(jnp.mean(x * x, axis=-1, keepdims=True) + EPS) * g


def _group_affine(x, scale, shift, group):
    tm = x.shape[0]
    pieces = []
    for gi in range(tm // group):
        xg = x[gi * group:(gi + 1) * group, :]
        if scale is not None:
            xg = xg * scale[gi:gi + 1, :]
        if shift is not None:
            xg = xg + shift[gi:gi + 1, :]
        pieces.append(xg)
    return jnp.concatenate(pieces, axis=0)


def _cumsum_rows(x):
    c, w = x.shape
    row = lax.broadcasted_iota(I32, (c, 1), 0)
    s = 1
    while s < c:
        if s % 8 == 0:
            shifted = jnp.concatenate([jnp.zeros((s, w), x.dtype), x[:c - s, :]], axis=0)
        else:
            shifted = jnp.where(row >= s, pltpu.roll(x, s, 0), 0.0)
        x = x + shifted
        s *= 2
    return x


def _dot(a, b):
    return jnp.dot(a, b, preferred_element_type=F32)


def _dot_nt(a, b):
    return lax.dot_general(a, b, (((1,), (1,)), ((), ())), preferred_element_type=F32)


def _dot_tn(a, b):
    return lax.dot_general(a, b, (((0,), (0,)), ((), ())), preferred_element_type=F32)


def _ada_kernel(c_ref, w_ref, b_ref, o_ref):
    c = c_ref[...]
    a = (c * jax.nn.sigmoid(c)).astype(BF16)
    o_ref[...] = _dot(a, w_ref[...].astype(BF16)) + b_ref[...]


def _ada(c_all, w_ada, b_ada):
    depth, d, n6 = w_ada.shape
    s = c_all.shape[0]
    tn = _pick_tile(n6, 1536)
    return pl.pallas_call(
        _ada_kernel,
        grid=(depth, n6 // tn),
        in_specs=[
            pl.BlockSpec((s, d), lambda l, j: (0, 0)),
            pl.BlockSpec((None, d, tn), lambda l, j: (l, 0, j)),
            pl.BlockSpec((None, 1, tn), lambda l, j: (l, 0, j)),
        ],
        out_specs=pl.BlockSpec((None, s, tn), lambda l, j: (l, 0, j)),
        out_shape=jax.ShapeDtypeStruct((depth, s, n6), F32),
        compiler_params=_cparams(("arbitrary", "arbitrary")),
        name="ada",
    )(c_all, w_ada, b_ada.reshape(depth, 1, n6))


def _inproj_kernel(x_ref, g_ref, sc_ref, sh_ref, w_ref, *out_refs, splits, group):
    h = _rms(x_ref[...], g_ref[...])
    h = _group_affine(h, 1.0 + sc_ref[...], sh_ref[...], group).astype(BF16)
    for o_ref, (c0, c1) in zip(out_refs, splits):
        o_ref[...] = _dot(h, w_ref[:, c0:c1]).astype(o_ref.dtype)


def _inproj(x, gain, modg, l, w, widths, group):
    n, d = x.shape
    tm = _pick_tile(n, 512)
    tg = tm // group
    splits, c = [], 0
    for wd in widths:
        splits.append((c, c + wd))
        c += wd
    mod_spec = lambda comp: pl.BlockSpec((None, None, tg, d), lambda i: (l, comp, i, 0))
    return pl.pallas_call(
        functools.partial(_inproj_kernel, splits=tuple(splits), group=group),
        grid=(n // tm,),
        in_specs=[
            pl.BlockSpec((tm, d), lambda i: (i, 0)),
            pl.BlockSpec((1, d), lambda i: (0, 0)),
            mod_spec(1),
            mod_spec(0),
            pl.BlockSpec((d, c), lambda i: (0, 0)),
        ],
        out_specs=[pl.BlockSpec((tm, wd), lambda i: (i, 0)) for wd in widths],
        out_shape=[jax.ShapeDtypeStruct((n, wd), F32) for wd in widths],
        compiler_params=_cparams(("arbitrary",), VMEM_LIMIT),
        name="inproj",
    )(x, gain.reshape(1, d), modg, modg, w)


REC_SEQS = 8


def _rec_kernel(*refs, mode, heads, chunk, nchunks, zero_init, nseqs):
    refs = list(refs)
    ntok = 5 if mode == "gla" else 4
    tok_refs = [refs[s * ntok:(s + 1) * ntok] for s in range(nseqs)]
    refs = refs[nseqs * ntok:]
    if mode == "gla":
        wah_ref, wal_ref, ab_ref, gn_ref = refs[:4]
        refs = refs[4:]
    else:
        lb_ref, gn_ref = refs[:2]
        refs = refs[2:]
    s0_ref = None
    if not zero_init:
        s0_ref = refs.pop(0)
    o_ref, sout_ref = refs[:2]
    st_scr = refs[2:]
    ci = pl.program_id(1)

    @pl.when(ci == 0)
    def _():
        for s in range(nseqs):
            for h in range(heads):
                if zero_init:
                    st_scr[s * heads + h][...] = jnp.zeros((LANE, LANE), F32)
                else:
                    st_scr[s * heads + h][...] = s0_ref[s, h].T

    row = lax.broadcasted_iota(I32, (chunk, chunk), 0)
    col = lax.broadcasted_iota(I32, (chunk, chunk), 1)
    causal = row >= col
    mid = chunk // 2 - 1

    for s in range(nseqs):
        if mode == "gla":
            q_ref, k_ref, v_ref, r_ref, a_ref = tok_refs[s]
            a = a_ref[...]
            a_hi = a.astype(BF16)
            a_lo = (a - a_hi.astype(F32)).astype(BF16)
            alog_all = (_dot(a_hi, wah_ref[...]) + _dot(a_lo, wah_ref[...]) + _dot(a_hi, wal_ref[...])
                        + ab_ref[...])
            g_all = jax.nn.log_sigmoid(alog_all) * (1.0 / GATE_TAU)
            fg_all = None
        else:
            q_ref, k_ref, v_ref, r_ref = tok_refs[s]
            lb = lb_ref[...]
            fg_all = lb + (1.0 - lb) * jax.nn.sigmoid(k_ref[...])
            g_all = jnp.log(fg_all)
        b_all = _cumsum_rows(g_all)

        for h in range(heads):
            sl = slice(h * LANE, (h + 1) * LANE)
            if mode == "gla":
                q = q_ref[:, sl] * (DK_A ** -0.5)
                k = k_ref[:, sl]
            else:
                qr = q_ref[:, sl]
                q = qr * jax.nn.sigmoid(qr)
                k = 1.0 - fg_all[:, sl]
            b = b_all[:, sl]
            b_last = b[chunk - 1:chunk, :]
            b_ref_row = b[mid:mid + 1, :]
            qt = (q * jnp.exp(jnp.minimum(b - b_ref_row, EXP_CLAMP))).astype(BF16)
            kt = (k * jnp.exp(jnp.minimum(b_ref_row - b, EXP_CLAMP))).astype(BF16)
            qs = (q * jnp.exp(b)).astype(BF16)
            ks = (k * jnp.exp(b_last - b)).astype(BF16)
            v = v_ref[:, sl].astype(BF16)
            att = jnp.where(causal, _dot_nt(qt, kt), 0.0).astype(BF16)
            st_ref = st_scr[s * heads + h]
            st = st_ref[...]
            o = _dot(att, v) + _dot_nt(qs, st.astype(BF16))
            st_ref[...] = st * jnp.exp(b_last) + _dot_tn(v, ks)
            rg = r_ref[:, sl]
            o = _rms(o, gn_ref[...]) * (rg * jax.nn.sigmoid(rg))
            o_ref[s, :, sl] = o.astype(o_ref.dtype)

    @pl.when(ci == nchunks - 1)
    def _():
        for s in range(nseqs):
            for h in range(heads):
                sout_ref[s, h] = st_scr[s * heads + h][...].T


def _recurrence(mode, q, k, v, r, extras, gnorm, s0, *, heads, nseq, seqlen, chunk, row0):
    cols = q.shape[1]
    nchunks = seqlen // chunk
    blk0 = row0 // chunk
    ns = REC_SEQS if nseq % REC_SEQS == 0 else 1
    full = lambda a: pl.BlockSpec(a.shape, lambda b, c: (0,) * a.ndim)
    in_specs, args = [], []
    for s in range(ns):
        tok = lambda b, c, s=s: (blk0 + (b * ns + s) * nchunks + c, 0)
        toks = [q, k, v, r] + ([extras[0]] if mode == "gla" else [])
        in_specs += [pl.BlockSpec((chunk, a.shape[1]), tok) for a in toks]
        args += toks
    consts = list(extras[1:]) if mode == "gla" else list(extras)
    consts.append(gnorm)
    in_specs += [full(a) for a in consts]
    args += consts
    zero_init = s0 is None
    if not zero_init:
        s0, s0_layer = s0
        in_specs.append(pl.BlockSpec((None, ns, heads, LANE, LANE), lambda b, c: (s0_layer, b, 0, 0, 0)))
        args.append(s0)
    o, st = pl.pallas_call(
        functools.partial(_rec_kernel, mode=mode, heads=heads, chunk=chunk, nchunks=nchunks,
                          zero_init=zero_init, nseqs=ns),
        grid=(nseq // ns, nchunks),
        in_specs=in_specs,
        out_specs=[pl.BlockSpec((ns, chunk, cols), lambda b, c: (b, c, 0)),
                   pl.BlockSpec((ns, heads, LANE, LANE), lambda b, c: (b, 0, 0, 0))],
        out_shape=[jax.ShapeDtypeStruct((nseq, seqlen, cols), BF16),
                   jax.ShapeDtypeStruct((nseq, heads, LANE, LANE), F32)],
        scratch_shapes=[pltpu.VMEM((LANE, LANE), F32) for _ in range(ns * heads)],
        compiler_params=_cparams(("arbitrary", "arbitrary")),
        name="recurrence_" + mode,
    )(*args)
    return o.reshape(nseq * seqlen, cols), st


def _mla_pre_kernel(cq_ref, ckv_ref, kpe_ref, cosq_ref, sinq_ref, cosk_ref, sink_ref, qg_ref, kvg_ref,
                    wuq_ref, wuk_ref, perm_ref, qcat_ref, kcat_ref, lat_ref, kpeo_ref, qt_ref, latt_ref, *, tq):
    tm = cq_ref.shape[0]
    cqn = _rms(cq_ref[...], qg_ref[...]).astype(BF16)
    qf = _dot(cqn, wuq_ref[...])
    off = H_B * NOPE
    x1 = qf[:, off:off + LANE]
    x2 = qf[:, off + LANE:off + 2 * LANE]
    cq, sq = cosq_ref[...], sinq_ref[...]
    o1 = (x1 * cq - x2 * sq) * MLA_SCALE
    o2 = (x2 * cq + x1 * sq) * MLA_SCALE
    pe = _dot(o1.astype(BF16), perm_ref[0:LANE, :]) + _dot(o2.astype(BF16), perm_ref[LANE:2 * LANE, :])
    for h in range(H_B):
        ql = _dot(qf[:, h * NOPE:(h + 1) * NOPE].astype(BF16), wuk_ref[h]) * MLA_SCALE
        peh = pe[:, h * LANE:(h + 1) * LANE]
        qcat_ref[h, :, 0:KV_LORA] = ql.astype(BF16)
        qcat_ref[h, :, KV_LORA:KCAT] = peh.astype(BF16)
        for jb in range(tm // tq):
            rows = slice(jb * tq, (jb + 1) * tq)
            cols = slice(h * tq, (h + 1) * tq)
            qt_ref[jb, 0:KV_LORA, cols] = ql[rows, :].T.astype(BF16)
            qt_ref[jb, KV_LORA:KCAT, cols] = peh[rows, :].T.astype(BF16)
    latn = _rms(ckv_ref[...], kvg_ref[...])
    lat_ref[...] = latn
    latt_ref[...] = latn.T.astype(BF16)
    x = kpe_ref[...]
    half = ROPE // 2
    lane = lax.broadcasted_iota(I32, x.shape, 1)
    rot = jnp.where(lane < half, -pltpu.roll(x, LANE - half, 1), pltpu.roll(x, half, 1))
    kro = x * cosk_ref[...] + rot * sink_ref[...]
    kpeo_ref[...] = kro[:, 0:ROPE]
    kcat_ref[:, 0:KV_LORA] = latn.astype(BF16)
    kcat_ref[:, KV_LORA:KCAT] = kro.astype(BF16)


ATTN_TQ = 8 * CHUNK


def _mla_pre(zcq, zckv, zkpe, tabs, qg, kvg, wuq, wuk, perm):
    n = zcq.shape[0]
    tm = _pick_tile(n, 512)
    tq = ATTN_TQ
    assert tm % tq == 0
    tokspec = lambda wd: pl.BlockSpec((tm, wd), lambda i: (i, 0))
    full = lambda a: pl.BlockSpec(a.shape, lambda i: (0,) * a.ndim)
    return pl.pallas_call(
        functools.partial(_mla_pre_kernel, tq=tq),
        grid=(n // tm,),
        in_specs=[tokspec(Q_LORA), tokspec(KV_LORA), tokspec(LANE)] + [tokspec(LANE)] * 4
                 + [full(qg), full(kvg), full(wuq), full(wuk), full(perm)],
        out_specs=[pl.BlockSpec((H_B, tm, KCAT), lambda i: (0, i, 0)), tokspec(KCAT),
                   tokspec(KV_LORA), tokspec(ROPE),
                   pl.BlockSpec((tm // tq, KCAT, H_B * tq), lambda i: (i, 0, 0)),
                   pl.BlockSpec((KV_LORA, tm), lambda i: (0, i))],
        out_shape=[jax.ShapeDtypeStruct((H_B, n, KCAT), BF16), jax.ShapeDtypeStruct((n, KCAT), BF16),
                   jax.ShapeDtypeStruct((n, KV_LORA), F32), jax.ShapeDtypeStruct((n, ROPE), F32),
                   jax.ShapeDtypeStruct((n // tq, KCAT, H_B * tq), BF16),
                   jax.ShapeDtypeStruct((KV_LORA, n), BF16)],
        compiler_params=_cparams(("arbitrary",)),
        name="mla_pre",
    )(zcq, zckv, zkpe, *tabs, qg, kvg, wuq, wuk, perm)


def _softmax_update(s, vals, m_scr, l_scr, acc_scr):
    m_prev = m_scr[...]
    m_new = jnp.maximum(m_prev, jnp.max(s, axis=1, keepdims=True))
    alpha = jnp.exp(m_prev - m_new)
    p = jnp.exp(s - m_new)
    l_scr[...] = alpha * l_scr[...] + jnp.sum(p, axis=1, keepdims=True)
    acc_scr[...] = alpha * acc_scr[...] + _dot(p.astype(BF16), vals)
    m_scr[...] = m_new


def _softmax_init(m_scr, l_scr, acc_scr):
    m_scr[...] = jnp.full(m_scr.shape, -jnp.inf, F32)
    l_scr[...] = jnp.zeros(l_scr.shape, F32)
    acc_scr[...] = jnp.zeros(acc_scr.shape, F32)


def _attn_finish(o_ref, wuv_ref, l_scr, acc_scr, tq):
    inv = 1.0 / l_scr[...]
    for h in range(H_B):
        rows = slice(h * tq, (h + 1) * tq)
        oh = (acc_scr[rows, :] * inv[rows, :]).astype(BF16)
        o_ref[:, h * V_B:(h + 1) * V_B] = _dot(oh, wuv_ref[h]).astype(o_ref.dtype)


def _attn_prompt_kernel(qt_ref, k_ref, latt_ref, wuvt_ref, o_ref, m_scr, l_scr, acc_scr, p_scr, *, tq, tk):
    qi = pl.program_id(1)
    cols = H_B * tq
    qt = qt_ref[...]
    m_scr[...] = jnp.full(m_scr.shape, -jnp.inf, F32)
    l_scr[...] = jnp.zeros(l_scr.shape, F32)
    acc_scr[...] = jnp.zeros(acc_scr.shape, F32)

    def block(start, masked):
        s = _dot(k_ref[pl.ds(start, tk), :], qt)
        if masked:
            tok = qi * tq + (lax.broadcasted_iota(I32, (1, cols), 1) & (tq - 1))
            limit = (tok // CHUNK + 1) * CHUNK
            key = start + lax.broadcasted_iota(I32, (tk, 1), 0)
            s = jnp.where(key < limit, s, -jnp.inf)
        m_prev = m_scr[...]
        m_new = jnp.maximum(m_prev, jnp.max(s, axis=0, keepdims=True))
        alpha = jnp.exp(m_prev - m_new)
        p = jnp.exp(s - m_new)
        l_scr[...] = alpha * l_scr[...] + jnp.sum(p, axis=0, keepdims=True)
        m_scr[...] = m_new
        p_scr[...] = p.astype(BF16)
        acc_scr[...] = alpha * acc_scr[...] + _dot(latt_ref[:, pl.ds(start, tk)], p_scr[...])

    nfull = (qi * tq) // tk

    def full(j, carry):
        block(pl.multiple_of(j * tk, tk), False)
        return carry

    lax.fori_loop(0, nfull, full, 0)
    block(pl.multiple_of(nfull * tk, tk), True)

    ot =(acc_scr[...] * (1.0 / l_scr[...])).astype(BF16)
    for h in range(H_B):
        oh = _dot(wuvt_ref[h], ot[:, h * tq:(h + 1) * tq])
        o_ref[:, h * V_B:(h + 1) * V_B] = oh.T.astype(o_ref.dtype)


def _attn_prompt(qt, kcat, latt, wuvt, nseq, seqlen):
    n = nseq * seqlen
    tq = ATTN_TQ
    tk = 512
    assert seqlen % tk == 0 and tk % tq == 0
    nq = seqlen // tq
    return pl.pallas_call(
        functools.partial(_attn_prompt_kernel, tq=tq, tk=tk),
        grid=(nseq, nq),
        in_specs=[
            pl.BlockSpec((None, KCAT, H_B * tq), lambda b, qi: (b * nq + qi, 0, 0)),
            pl.BlockSpec((seqlen, KCAT), lambda b, qi: (b, 0)),
            pl.BlockSpec((KV_LORA, seqlen), lambda b, qi: (0, b)),
            pl.BlockSpec(wuvt.shape, lambda b, qi: (0, 0, 0)),
        ],
        out_specs=pl.BlockSpec((tq, H_B * V_B), lambda b, qi: (b * nq + qi, 0)),
        out_shape=jax.ShapeDtypeStruct((n, H_B * V_B), BF16),
        scratch_shapes=[pltpu.VMEM((1, H_B * tq), F32), pltpu.VMEM((1, H_B * tq), F32),
                        pltpu.VMEM((KV_LORA, H_B * tq), F32), pltpu.VMEM((tk, H_B * tq), BF16)],
        compiler_params=_cparams(("arbitrary", "arbitrary")),
        name="attn_prompt",
    )(qt, kcat, latt, wuvt)


def _attn_sample_kernel(q_ref, plat_ref, pkpe_ref, knew_ref, wuv_ref, o_ref,
                        m_scr, l_scr, acc_scr, *, tq, nkp):
    ki = pl.program_id(1)

    @pl.when(ki == 0)
    def _():
        _softmax_init(m_scr, l_scr, acc_scr)

    q = q_ref[...].reshape(H_B * tq, KCAT)
    lat = plat_ref[...].astype(BF16)
    kpe = pkpe_ref[...].astype(BF16)
    s = _dot_nt(q[:, 0:KV_LORA], lat) + _dot_nt(q[:, KV_LORA:KV_LORA + ROPE], kpe)
    _softmax_update(s, lat, m_scr, l_scr, acc_scr)

    @pl.when(ki == nkp - 1)
    def _():
        kn = knew_ref[...]
        _softmax_update(_dot_nt(q, kn), kn[:, 0:KV_LORA], m_scr, l_scr, acc_scr)
        _attn_finish(o_ref, wuv_ref, l_scr, acc_scr, tq)


def _attn_sample(qcat, kcat, past_lat, past_kpe, layer, wuv, row0):
    _, nseq, past, _ = past_lat.shape
    tq = (kcat.shape[0] - row0) // nseq
    tkp = _pick_tile(past, 2048)
    nkp = past // tkp
    blk0 = row0 // tq
    return pl.pallas_call(
        functools.partial(_attn_sample_kernel, tq=tq, nkp=nkp),
        grid=(nseq, nkp),
        in_specs=[
            pl.BlockSpec((H_B, tq, KCAT), lambda b, ki: (0, blk0 + b, 0)),
            pl.BlockSpec((None, None, tkp, KV_LORA), lambda b, ki: (layer, b, ki, 0)),
            pl.BlockSpec((None, None, tkp, ROPE), lambda b, ki: (layer, b, ki, 0)),
            pl.BlockSpec((tq, KCAT), lambda b, ki: (blk0 + b, 0)),
            pl.BlockSpec(wuv.shape, lambda b, ki: (0, 0, 0)),
        ],
        out_specs=pl.BlockSpec((tq, H_B * V_B), lambda b, ki: (b, 0)),
        out_shape=jax.ShapeDtypeStruct((nseq * tq, H_B * V_B), BF16),
        scratch_shapes=[pltpu.VMEM((H_B * tq, 1), F32), pltpu.VMEM((H_B * tq, 1), F32),
                        pltpu.VMEM((H_B * tq, KV_LORA), F32)],
        compiler_params=_cparams(("arbitrary", "arbitrary")),
        name="attn_sample",
    )(qcat, past_lat, past_kpe, kcat, wuv)


def _outproj_router_kernel(*refs, nlhs, group, ptiles, nexp):
    x_ref, g_ref = refs[0], refs[1]
    lhs_p = refs[2:2 + nlhs]
    lhs_s = refs[2 + nlhs:2 + 2 * nlhs]
    ws = refs[2 + 2 * nlhs:2 + 3 * nlhs]
    n2_ref, sc_ref, sh_ref, wh_ref, wl_ref, br_ref, o_ref, h_ref, ti_ref, gt_ref = refs[2 + 3 * nlhs:]
    is_prompt = pl.program_id(0) < ptiles
    acc = None
    for ap, asm, w in zip(lhs_p, lhs_s, ws):
        a = jnp.where(is_prompt, ap[...], asm[...])
        t = _dot(a, w[...])
        acc = t if acc is None else acc + t
    x = x_ref[...] + _group_affine(acc, g_ref[...], None, group)
    o_ref[...] = x
    _route(x, n2_ref, sc_ref, sh_ref, wh_ref, wl_ref, br_ref, h_ref, ti_ref, gt_ref, group, nexp)


def _outproj_router(x, modg, l, lhs_p, lhs_s, ws, gain2, wr_hi, wr_lo, br, group, nexp):
    n, d = x.shape
    n_p, n_s = lhs_p[0].shape[0], lhs_s[0].shape[0]
    tm = _pick_tile(math.gcd(n_p, n_s), 512)
    tg = tm // group
    ptiles = n_p // tm
    mod_spec = lambda comp: pl.BlockSpec((None, None, tg, d), lambda i: (l, comp, i, 0))
    full = lambda a: pl.BlockSpec(a.shape, lambda i: (0,) * a.ndim)
    tok = lambda wd: pl.BlockSpec((tm, wd), lambda i: (i, 0))
    return pl.pallas_call(
        functools.partial(_outproj_router_kernel, nlhs=len(ws), group=group, ptiles=ptiles, nexp=nexp),
        grid=(n // tm,),
        in_specs=[tok(d), mod_spec(2)]
                 + [pl.BlockSpec((tm, a.shape[1]), lambda i: (jnp.minimum(i, ptiles - 1), 0)) for a in lhs_p]
                 + [pl.BlockSpec((tm, a.shape[1]), lambda i: (jnp.maximum(i - ptiles, 0), 0)) for a in lhs_s]
                 + [full(w) for w in ws]
                 + [pl.BlockSpec((1, d), lambda i: (0, 0)), mod_spec(4), mod_spec(3),
                    full(wr_hi), full(wr_lo), full(br)],
        out_specs=[tok(d), tok(d), tok(LANE), tok(LANE)],
        out_shape=[jax.ShapeDtypeStruct((n, d), F32), jax.ShapeDtypeStruct((n, d), F32),
                   jax.ShapeDtypeStruct((n, LANE), I32), jax.ShapeDtypeStruct((n, LANE), F32)],
        compiler_params=_cparams(("arbitrary",), VMEM_LIMIT),
        name="outproj_router",
    )(x, modg, *lhs_p, *lhs_s, *ws, gain2.reshape(1, d), modg, modg, wr_hi, wr_lo, br)


def _route(x, g_ref, sc_ref, sh_ref, wh_ref, wl_ref, br_ref, h_ref, ti_ref, gt_ref, group, nexp):
    h = _rms(x, g_ref[...])
    h = _group_affine(h, 1.0 + sc_ref[...], sh_ref[...], group)
    h_ref[...] = h
    h_hi = h.astype(BF16)
    h_lo = (h - h_hi.astype(F32)).astype(BF16)
    logits = _dot(h_hi, wh_ref[...]) + _dot(h_lo, wh_ref[...]) + _dot(h_hi, wl_ref[...]) + br_ref[...]
    lane = lax.broadcasted_iota(I32, logits.shape, 1)
    lane_f = lane.astype(F32)
    cur = jnp.where(lane < nexp, logits, -jnp.inf)
    tops, idxs = [], []
    for _ in range(TOP_K):
        m = jnp.max(cur, axis=1, keepdims=True)
        i = jnp.min(jnp.where(cur == m, lane_f, float(LANE)), axis=1, keepdims=True)
        cur = jnp.where(lane_f == i, -jnp.inf, cur)
        tops.append(m)
        idxs.append(i.astype(I32))
    es = [jnp.exp(t - tops[0]) for t in tops]
    inv = 1.0 / (es[0] + es[1] + es[2] + es[3])
    ti = jnp.zeros(logits.shape, I32)
    gt = jnp.zeros(logits.shape, F32)
    for k in range(TOP_K):
        ti = jnp.where(lane == k, idxs[k], ti)
        gt = jnp.where(lane == k, es[k] * inv, gt)
    ti_ref[...] = ti
    gt_ref[...] = gt


def _rank_kernel(ti_ref, rank_ref, cnt_ref, carry_scr, *, tm):
    i = pl.program_id(0)

    @pl.when(i == 0)
    def _():
        carry_scr[...] = jnp.zeros(carry_scr.shape, F32)

    ti = ti_ref[...]
    lane = lax.broadcasted_iota(I32, ti.shape, 1)
    sel = [lane == ti[:, k:k + 1] for k in range(TOP_K)]
    hot = jnp.zeros(ti.shape, F32)
    for s in sel:
        hot = hot + jnp.where(s, 1.0, 0.0)
    row = lax.broadcasted_iota(I32, (tm, tm), 0)
    col = lax.broadcasted_iota(I32, (tm, tm), 1)
    strict = jnp.where(row > col, 1.0, 0.0).astype(BF16)
    before = _dot(strict, hot.astype(BF16)) + carry_scr[0:1, :]
    rank = jnp.zeros(ti.shape, F32)
    for k in range(TOP_K):
        rk = jnp.sum(jnp.where(sel[k], before, 0.0), axis=1, keepdims=True)
        rank = jnp.where(lane == k, rk, rank)
    rank_ref[...] = rank.astype(I32)
    carry_scr[...] = carry_scr[...] + jnp.sum(hot, axis=0, keepdims=True)
    cnt_ref[...] = carry_scr[...]


def _rank(topi):
    n = topi.shape[0]
    tm = _pick_tile(n, 512)
    return pl.pallas_call(
        functools.partial(_rank_kernel, tm=tm),
        grid=(n // tm,),
        in_specs=[pl.BlockSpec((tm, LANE), lambda i: (i, 0))],
        out_specs=[pl.BlockSpec((tm, LANE), lambda i: (i, 0)), pl.BlockSpec((8, LANE), lambda i: (0, 0))],
        out_shape=[jax.ShapeDtypeStruct((n, LANE), I32), jax.ShapeDtypeStruct((8, LANE), F32)],
        scratch_shapes=[pltpu.VMEM((8, LANE), F32)],
        compiler_params=_cparams(("arbitrary",)),
        name="rank",
    )(topi)


def _dest_kernel(ti_ref, rank_ref, ps_ref, d_ref):
    ti = ti_ref[...]
    lane = lax.broadcasted_iota(I32, ti.shape, 1)
    ps = ps_ref[...]
    dest = rank_ref[...]
    for k in range(TOP_K):
        base = jnp.sum(jnp.where(lane == ti[:, k:k + 1], ps, 0.0), axis=1, keepdims=True).astype(I32)
        dest = dest + jnp.where(lane == k, base, 0)
    d_ref[...] = dest


def _dest(topi, rank, pstarts_row):
    n = topi.shape[0]
    tm = _pick_tile(n, 512)
    tok = pl.BlockSpec((tm, LANE), lambda i: (i, 0))
    return pl.pallas_call(
        _dest_kernel,
        grid=(n // tm,),
        in_specs=[tok, tok, pl.BlockSpec((1, LANE), lambda i: (0, 0))],
        out_specs=tok,
        out_shape=jax.ShapeDtypeStruct((n, LANE), I32),
        compiler_params=_cparams(("arbitrary",)),
        name="dest",
    )(topi, rank, pstarts_row)


def _dispatch_kernel(dest_ref, pad_ref, h_ref, xs_ref, hbuf, zbuf, sem, zsem, *, tm, nexp, nblocks):
    i = pl.program_id(0)
    slot = lax.rem(i, 2)
    hbuf[slot] = h_ref[...]

    def issue(it, carry):
        for u in range(ISSUE_UNROLL):
            r = it * ISSUE_UNROLL + u
            for k in range(TOP_K):
                d = dest_ref[r * TOP_K + k]
                pltpu.make_async_copy(hbuf.at[slot, pl.ds(r, 1)], xs_ref.at[pl.ds(d, 1)], sem.at[slot]).start()
        return carry

    lax.fori_loop(0, tm // ISSUE_UNROLL, issue, 0)

    def retire(s):
        for _ in range(TOP_K):
            pltpu.make_async_copy(hbuf.at[s], xs_ref.at[pl.ds(0, tm)], sem.at[s]).wait()

    @pl.when(i >= 1)
    def _():
        retire(1 - slot)

    @pl.when(i == pl.num_programs(0) - 1)
    def _():
        retire(slot)
        zbuf[...] = jnp.zeros(zbuf.shape, F32)

        def fill_expert(e, carry):
            end = pad_ref[0, e]
            npad = pad_ref[1, e]
            p = MOE_BLOCK // 2
            while p >= 1:
                bit = npad & p
                end = end - bit

                @pl.when(bit != 0)
                def _(end=end, p=p):
                    if p >= 8:
                        start = pl.multiple_of(end, 8)
                        pltpu.make_async_copy(zbuf.at[pl.ds(0, p)], xs_ref.at[pl.ds(start, p)], zsem).start()
                    else:
                        for q in range(p):
                            pltpu.make_async_copy(zbuf.at[pl.ds(0, 1)], xs_ref.at[pl.ds(end + q, 1)],
                                                  zsem).start()

                p //= 2
            return carry

        lax.fori_loop(0, nexp, fill_expert, 0)
        nused = pad_ref[2, 0]

        def fill_tail(b, carry):
            @pl.when(b >= nused)
            def _():
                start = pl.multiple_of(b * MOE_BLOCK, MOE_BLOCK)
                pltpu.make_async_copy(zbuf, xs_ref.at[pl.ds(start, MOE_BLOCK)], zsem).start()

            return carry

        lax.fori_loop(0, nblocks, fill_tail, 0)
        for _ in range(nexp):
            pltpu.make_async_copy(zbuf, xs_ref.at[pl.ds(0, MOE_BLOCK)], zsem).wait()


def _dispatch(dest1d, padinfo, h, rows, nexp):
    n, d = h.shape
    tm = _pick_tile(n, 256)
    return pl.pallas_call(
        functools.partial(_dispatch_kernel, tm=tm, nexp=nexp, nblocks=rows // MOE_BLOCK),
        grid=(n // tm,),
        in_specs=[pl.BlockSpec((tm * TOP_K,), lambda i: (i,), memory_space=pltpu.SMEM),
                  pl.BlockSpec(memory_space=pltpu.SMEM),
                  pl.BlockSpec((tm, d), lambda i: (i, 0))],
        out_specs=pl.BlockSpec(memory_space=pl.ANY),
        out_shape=jax.ShapeDtypeStruct((rows, d), F32),
        scratch_shapes=[pltpu.VMEM((2, tm, d), F32), pltpu.VMEM((MOE_BLOCK, d), F32),
                        pltpu.SemaphoreType.DMA((2,)), pltpu.SemaphoreType.DMA(())],
        compiler_params=pltpu.CompilerParams(dimension_semantics=("arbitrary",), has_side_effects=True),
        name="dispatch",
    )(dest1d, padinfo, h)


def _experts_kernel(bexp_ref, nused_ref, xs_ref, w1_ref, b1_ref, w2_ref, b2_ref, y_ref,
                    w1b, w2b, act_scr, prev_scr, *, dff, dm):
    i = pl.program_id(0)
    nu = nused_ref[0]
    e = bexp_ref[jnp.minimum(i, nu - 1)]

    @pl.when(i == 0)
    def _():
        prev_scr[0] = -1

    @pl.when(i >= nu)
    def _():
        y_ref[...] = jnp.zeros(y_ref.shape, F32)

    @pl.when(i < nu)
    def _():
        @pl.when(e != prev_scr[0])
        def _():
            rows = 128

            def cast1(c, carry):
                r0 = pl.multiple_of(c * rows, rows)
                w1b[pl.ds(r0, rows), :] = w1_ref[pl.ds(r0, rows), :].astype(BF16)
                return carry

            def cast2(c, carry):
                r0 = pl.multiple_of(c * rows, rows)
                w2b[pl.ds(r0, rows), :] = w2_ref[pl.ds(r0, rows), :].astype(BF16)
                return carry

            lax.fori_loop(0, dm // rows, cast1, 0)
            lax.fori_loop(0, dff // rows, cast2, 0)
            prev_scr[0] = e

        x = xs_ref[...].astype(BF16)
        cw = 256
        for c in range(dff // cw):
            gt = _dot(x, w1b[:, c * cw:(c + 1) * cw]) + b1_ref[:, c * cw:(c + 1) * cw]
            up = _dot(x, w1b[:, dff + c * cw:dff + (c + 1) * cw]) + b1_ref[:, dff + c * cw:dff + (c + 1) * cw]
            gt = jnp.minimum(gt, SWIGLU_LIMIT)
            up = jnp.clip(up, -SWIGLU_LIMIT, SWIGLU_LIMIT)
            act = gt * jax.nn.sigmoid(gt * SWIGLU_ALPHA) * (up + 1.0)
            act_scr[:, c * cw:(c + 1) * cw] = act.astype(BF16)
        a = act_scr[...]
        for c in range(dm // cw):
            y_ref[:, c * cw:(c + 1) * cw] = _dot(a, w2b[:, c * cw:(c + 1) * cw]) + b2_ref[:, c * cw:(c + 1) * cw]


def _experts(bexp, nused, xs, w1, b1, w2, b2, l):
    rows, dm = xs.shape
    dff = w2.shape[2]
    nb = rows // MOE_BLOCK
    blk = lambda i, be, nu: (jnp.minimum(i, nu[0] - 1), 0)
    wsel = lambda i, be, nu: (l, be[jnp.minimum(i, nu[0] - 1)], 0, 0)
    grid_spec = pltpu.PrefetchScalarGridSpec(
        num_scalar_prefetch=2,
        grid=(nb,),
        in_specs=[
            pl.BlockSpec((MOE_BLOCK, dm), blk),
            pl.BlockSpec((None, None, dm, 2 * dff), wsel),
            pl.BlockSpec((None, None, 1, 2 * dff), wsel),
            pl.BlockSpec((None, None, dff, dm), wsel),
            pl.BlockSpec((None, None, 1, dm), wsel),
        ],
        out_specs=pl.BlockSpec((MOE_BLOCK, dm), lambda i, be, nu: (i, 0)),
        scratch_shapes=[pltpu.VMEM((dm, 2 * dff), BF16), pltpu.VMEM((dff, dm), BF16),
                        pltpu.VMEM((MOE_BLOCK, dff), BF16), pltpu.SMEM((1,), I32)],
    )
    return pl.pallas_call(
        functools.partial(_experts_kernel, dff=dff, dm=dm),
        grid_spec=grid_spec,
        out_shape=jax.ShapeDtypeStruct((rows, dm), F32),
        compiler_params=_cparams(("arbitrary",), VMEM_LIMIT),
        name="experts",
    )(bexp, nused, xs, w1, b1, w2, b2)


def _combine_kernel(dest_ref, dnext_ref, gate_ref, x_ref, g_ref, y_ref, o_ref, buf, sem, *, tm, group):
    i = pl.program_id(0)
    slot = lax.rem(i, 2)

    def gather(idx_ref, s):
        def issue(it, carry):
            for u in range(ISSUE_UNROLL):
                r = it * ISSUE_UNROLL + u
                for k in range(TOP_K):
                    d = idx_ref[r * TOP_K + k]
                    pltpu.make_async_copy(y_ref.at[pl.ds(d, 1)], buf.at[s, k, pl.ds(r, 1)], sem.at[s]).start()
            return carry

        lax.fori_loop(0, tm // ISSUE_UNROLL, issue, 0)

    @pl.when(i == 0)
    def _():
        gather(dest_ref, 0)

    @pl.when(i + 1 < pl.num_programs(0))
    def _():
        gather(dnext_ref, 1 - slot)

    for k in range(TOP_K):
        pltpu.make_async_copy(y_ref.at[pl.ds(0, tm)], buf.at[slot, k], sem.at[slot]).wait()
    gate = gate_ref[...]
    moe = gate[:, 0:1] * buf[slot, 0]
    for k in range(1, TOP_K):
        moe = moe + gate[:, k:k + 1] * buf[slot, k]
    o_ref[...] = x_ref[...] + _group_affine(moe, g_ref[...], None, group)


def _combine(dest1d, gate, x, modg, l, y, group):
    n, d = x.shape
    tm = _pick_tile(n, 256)
    tg = tm // group
    nt = n // tm
    return pl.pallas_call(
        functools.partial(_combine_kernel, tm=tm, group=group),
        grid=(nt,),
        in_specs=[pl.BlockSpec((tm * TOP_K,), lambda i: (i,), memory_space=pltpu.SMEM),
                  pl.BlockSpec((tm * TOP_K,), lambda i: (jnp.minimum(i + 1, nt - 1),), memory_space=pltpu.SMEM),
                  pl.BlockSpec((tm, LANE), lambda i: (i, 0)),
                  pl.BlockSpec((tm, d), lambda i: (i, 0)),
                  pl.BlockSpec((None, None, tg, d), lambda i: (l, 5, i, 0)),
                  pl.BlockSpec(memory_space=pl.ANY)],
        out_specs=pl.BlockSpec((tm, d), lambda i: (i, 0)),
        out_shape=jax.ShapeDtypeStruct((n, d), F32),
        scratch_shapes=[pltpu.VMEM((2, TOP_K, tm, d), F32), pltpu.SemaphoreType.DMA((2,))],
        compiler_params=_cparams(("arbitrary",), VMEM_LIMIT),
        name="combine",
    )(dest1d, dest1d, gate, x, modg, y)


def _final_norm_kernel(x_ref, g_ref, o_ref):
    o_ref[...] = _rms(x_ref[...], g_ref[...])


def _final_norm(x, g, row0, nrows):
    d = x.shape[1]
    tm = _pick_tile(math.gcd(row0, nrows) if row0 else nrows, 512)
    blk0 = row0 // tm
    return pl.pallas_call(
        _final_norm_kernel,
        grid=(nrows // tm,),
        in_specs=[pl.BlockSpec((tm, d), lambda i: (blk0 + i, 0)), pl.BlockSpec((1, d), lambda i: (0, 0))],
        out_specs=pl.BlockSpec((tm, d), lambda i: (i, 0)),
        out_shape=jax.ShapeDtypeStruct((nrows, d), F32),
        compiler_params=_cparams(("arbitrary",)),
        name="final_norm",
    )(x, g.reshape(1, d))


def _pad_heads(w, heads, dk):
    lead = w.shape[:-1]
    w = w.reshape(*lead, heads, dk)
    w = jnp.pad(w, [(0, 0)] * len(lead) + [(0, 0), (0, LANE - dk)])
    return w.reshape(*lead, heads * LANE)


def _pad_cols(w, width):
    return jnp.pad(w, [(0, 0)] * (w.ndim - 1) + [(0, width - w.shape[-1])])


EVEN_WIDTHS = (H_A * LANE, H_A * LANE, H_A * DV_A, H_A * DV_A, LANE, Q_LORA, KV_LORA, LANE)


def _even_weight(w):
    a_qk, a_v = H_A * DK_A, H_A * DV_A
    c = [0, a_qk, 2 * a_qk, 2 * a_qk + a_v, 2 * a_qk + 2 * a_v, 2 * a_qk + 2 * a_v + GATE_RANK]
    c.append(c[-1] + Q_LORA)
    c.append(c[-1] + KV_LORA)
    c.append(c[-1] + ROPE)
    parts = [
        _pad_heads(w[:, c[0]:c[1]], H_A, DK_A),
        _pad_heads(w[:, c[1]:c[2]], H_A, DK_A),
        w[:, c[2]:c[3]],
        w[:, c[3]:c[4]],
        _pad_cols(w[:, c[4]:c[5]], LANE),
        w[:, c[5]:c[6]],
        w[:, c[6]:c[7]],
        _pad_cols(w[:, c[7]:c[8]], LANE),
    ]
    return jnp.concatenate(parts, axis=1).astype(BF16)


def _uq_weight(w):
    w = w.reshape(Q_LORA, H_B, NOPE + ROPE)
    half = ROPE // 2
    nope = w[:, :, :NOPE].reshape(Q_LORA, H_B * NOPE)
    r1 = w[:, :, NOPE:NOPE + half].reshape(Q_LORA, H_B * half)
    r2 = w[:, :, NOPE + half:].reshape(Q_LORA, H_B * half)
    return jnp.concatenate([nope, r1, r2], axis=1).astype(BF16)


def _rope_perm():
    half = ROPE // 2
    r = jnp.arange(2 * LANE)
    second = r // LANE
    h = (r % LANE) // half
    i = r % half
    col = h * LANE + second * half + i
    return (col[:, None] == jnp.arange(H_B * LANE)[None, :]).astype(BF16)


def _rope_tables(segments):
    half = ROPE // 2
    inv_freq = jnp.exp(-math.log(ROPE_BASE) * jnp.arange(half, dtype=F32) / half)
    cos, sin = [], []
    for pos0, length, reps in segments:
        ang = (jnp.arange(length) + pos0).astype(F32)[:, None] * inv_freq[None, :]
        cos.append(jnp.tile(jnp.cos(ang), (reps, 1)))
        sin.append(jnp.tile(jnp.sin(ang), (reps, 1)))
    cos, sin = jnp.concatenate(cos, axis=0), jnp.concatenate(sin, axis=0)
    z = jnp.zeros((cos.shape[0], LANE - ROPE), F32)
    cosq, sinq = jnp.tile(cos, (1, LANE // half)), jnp.tile(sin, (1, LANE // half))
    cosk = jnp.concatenate([cos, cos, z], axis=1)
    sink = jnp.concatenate([sin, sin, z], axis=1)
    return cosq, sinq, cosk, sink


def kernel(x_prompt, x_sample, cache_mla_latent, cache_mla_krope, state_gla, state_hgrn, c_prompt, c_sample,
           w_ada, b_ada, norm1_g, norm2_g, w_in_even, w_gla_a2, b_gla_a, gla_norm_g, mla_q_norm_g, w_mla_uq,
           mla_kv_norm_g, w_mla_uk, w_mla_uv, w_out_even, w_in_odd, hgrn_lb, hgrn_norm_g, w_out_odd,
           w_router, b_router, w_e1, b_e1, w_e2, b_e2, final_norm_g):
    bp, tp, d = x_prompt.shape
    bs, ts, _ = x_sample.shape
    past = cache_mla_latent.shape[2]
    depth = w_ada.shape[0]
    nexp = w_router.shape[2]
    n_p, n_s = bp * tp, bs * ts
    n = n_p + n_s
    group = ts
    assert tp % group == 0 and group % 8 == 0 and tp % CHUNK == 0 and ts <= CHUNK
    assert (n * TOP_K) % MOE_BLOCK == 0
    n_even, n_odd = (depth + 1) // 2, depth // 2

    x = jnp.concatenate([x_prompt.reshape(n_p, d), x_sample.reshape(n_s, d)], axis=0)

    mod = _ada(jnp.concatenate([c_prompt, c_sample], axis=0), w_ada, b_ada)
    mod = mod.reshape(depth, bp + bs, 6, d).transpose(0, 2, 1, 3)
    modg = jnp.concatenate([jnp.repeat(mod[:, :, :bp], tp // group, axis=2), mod[:, :, bp:]], axis=2)

    tabs = _rope_tables([(0, tp, bp), (past, ts, bs)])
    perm = _rope_perm()

    lb_soft = jax.nn.softmax(hgrn_lb.astype(F32), axis=0)
    lb_all = jnp.cumsum(lb_soft, axis=0) - lb_soft[0]
    gla_s0 = jnp.pad(state_gla, ((0, 0), (0, 0), (0, 0), (0, LANE - DK_A), (0, 0)))

    lat_p, kpe_p, gla_p, hgrn_p, lat_s, kpe_s, gla_s, hgrn_s = [], [], [], [], [], [], [], []
    rows = (-(-(n * TOP_K) // MOE_BLOCK)) * MOE_BLOCK + nexp * MOE_BLOCK
    nblocks = rows // MOE_BLOCK

    for l in range(depth):
        j = l // 2
        wr = _pad_cols(w_router[l], LANE)
        wr_hi = wr.astype(BF16)
        wr_lo = (wr - wr_hi.astype(F32)).astype(BF16)
        br = _pad_cols(b_router[l].reshape(1, -1), LANE)
        route = (norm2_g[l], wr_hi, wr_lo, br, group, nexp)
        if l % 2 == 0:
            zq, zk, zv, zr, za, zcq, zckv, zkpe = _inproj(
                x, norm1_g[l], modg, l, _even_weight(w_in_even[j]), EVEN_WIDTHS, group)
            wa = _pad_heads(jnp.pad(w_gla_a2[j], ((0, LANE - GATE_RANK), (0, 0))), H_A, DK_A)
            wa_hi = wa.astype(BF16)
            wa_lo = (wa - wa_hi.astype(F32)).astype(BF16)
            ab = _pad_heads(b_gla_a[j].reshape(1, -1), H_A, DK_A)
            gn = gla_norm_g[j].reshape(1, DV_A)
            extras = (za, wa_hi, wa_lo, ab)
            oa_p, sp = _recurrence("gla", zq, zk, zv, zr, extras, gn, None, heads=H_A, nseq=bp,
                                   seqlen=tp, chunk=CHUNK, row0=0)
            oa_s, ss = _recurrence("gla", zq, zk, zv, zr, extras, gn, (gla_s0, j), heads=H_A, nseq=bs,
                                   seqlen=ts, chunk=ts, row0=n_p)
            gla_p.append(sp[:, :, :DK_A, :])
            gla_s.append(ss[:, :, :DK_A, :])

            wuk = w_mla_uk[j].transpose(1, 2, 0).astype(BF16)
            wuv = w_mla_uv[j].transpose(1, 0, 2).astype(BF16)
            wuvt = w_mla_uv[j].transpose(1, 2, 0).astype(BF16)
            qcat, kcat, lat, kpe, qt, latt = _mla_pre(
                zcq, zckv, zkpe, tabs, mla_q_norm_g[j].reshape(1, -1), mla_kv_norm_g[j].reshape(1, -1),
                _uq_weight(w_mla_uq[j]), wuk, perm)
            ob_p = _attn_prompt(qt, kcat, latt, wuvt, bp, tp)
            ob_s = _attn_sample(qcat, kcat, cache_mla_latent, cache_mla_krope, j, wuv, n_p)
            lat_p.append(lat[:n_p].reshape(bp, tp, KV_LORA))
            lat_s.append(lat[n_p:].reshape(bs, ts, KV_LORA))
            kpe_p.append(kpe[:n_p].reshape(bp, tp, ROPE))
            kpe_s.append(kpe[n_p:].reshape(bs, ts, ROPE))
            wo = w_out_even[j].astype(BF16)
            x, h2, topi, gate = _outproj_router(x, modg, l, [oa_p, ob_p], [oa_s, ob_s],
                                                [wo[:H_A * DV_A], wo[H_A * DV_A:]], *route)
        else:
            zq, zf, zi, zg = _inproj(x, norm1_g[l], modg, l, w_in_odd[j].astype(BF16),
                                     (H_C * DK_C,) * 2 + (H_C * DV_C,) * 2, group)
            extras = (lb_all[l].reshape(1, -1),)
            gn = hgrn_norm_g[j].reshape(1, DV_C)
            oc_p, sp = _recurrence("hgrn", zq, zf, zi, zg, extras, gn, None, heads=H_C, nseq=bp,
                                   seqlen=tp, chunk=CHUNK, row0=0)
            oc_s, ss = _recurrence("hgrn", zq, zf, zi, zg, extras, gn, (state_hgrn, j), heads=H_C, nseq=bs,
                                   seqlen=ts, chunk=ts, row0=n_p)
            hgrn_p.append(sp)
            hgrn_s.append(ss)
            x, h2, topi, gate = _outproj_router(x, modg, l, [oc_p], [oc_s], [w_out_odd[j].astype(BF16)],
                                                *route)

        rank, cnt = _rank(topi)
        counts = cnt[0, :nexp].astype(I32)
        padded = (counts + MOE_BLOCK - 1) // MOE_BLOCK * MOE_BLOCK
        pends = jnp.cumsum(padded)
        pstarts = _pad_cols((pends - padded).astype(F32).reshape(1, -1), LANE)
        dest = _dest(topi, rank, pstarts)
        dest1d = dest[:, :TOP_K].reshape(n * TOP_K)
        bexp = jnp.clip(jnp.sum(pends[None, :] <= (jnp.arange(nblocks) * MOE_BLOCK)[:, None], axis=1),
                        0, nexp - 1).astype(I32)
        nused = (pends[-1:] // MOE_BLOCK).astype(I32)
        padinfo = jnp.stack([_pad_cols(pends, LANE), _pad_cols(padded - counts, LANE),
                             _pad_cols(nused, LANE)]).astype(I32)
        xs = _dispatch(dest1d, padinfo, h2, rows, nexp)
        y = _experts(bexp, nused, xs, w_e1, b_e1.reshape(depth, nexp, 1, -1), w_e2,
                     b_e2.reshape(depth, nexp, 1, -1), l)
        x = _combine(dest1d, gate, x, modg, l, y, group)

    y_p = _final_norm(x, final_norm_g, 0, n_p)
    y_s = _final_norm(x, final_norm_g, n_p, n_s)
    return (y_p.reshape(bp, tp, d), y_s.reshape(bs, ts, d),
            jnp.stack(lat_p), jnp.stack(kpe_p), jnp.stack(gla_p), jnp.stack(hgrn_p),
            jnp.stack(lat_s), jnp.stack(kpe_s), jnp.stack(gla_s), jnp.stack(hgrn_s))
```

```python
import functools
import math

import jax
import jax.numpy as jnp
from jax import lax
from jax.experimental import pallas as pl
from jax.experimental.pallas import tpu as pltpu

F32 = jnp.float32
BF16 = jnp.bfloat16
I32 = jnp.int32

EPS = 1e-6
CHUNK = 64
LANE = 128

H_A, DK_A, DV_A, GATE_RANK, GATE_TAU = 4, 64, 128, 16, 16.0
H_B, Q_LORA, KV_LORA, NOPE, ROPE, V_B = 4, 384, 256, 128, 64, 128
ROPE_BASE = 10000.0
MLA_SCALE = (NOPE + ROPE) ** -0.5
KCAT = KV_LORA + LANE
H_C, DK_C, DV_C = 8, 128, 128
TOP_K = 4
SWIGLU_LIMIT = 7.0
SWIGLU_ALPHA = 1.702
MOE_BLOCK = 512
ISSUE_UNROLL = 8
EXP_CLAMP = 80.0

VMEM_LIMIT = 56 * 1024 * 1024


def _cparams(sem, vmem=None):
    return pltpu.CompilerParams(dimension_semantics=sem, vmem_limit_bytes=vmem)


def _pick_tile(n, pref):
    t = pref
    while n % t:
        t //= 2
    return t


def _rms(x, g):
    return x * lax.rsqrt(jnp.mean(x * x, axis=-1, keepdims=True) + EPS) * g


def _group_affine(x, scale, shift, group):
    tm = x.shape[0]
    pieces = []
    for gi in range(tm // group):
        xg = x[gi * group:(gi + 1) * group, :]
        if scale is not None:
            xg = xg * scale[gi:gi + 1, :]
        if shift is not None:
            xg = xg + shift[gi:gi + 1, :]
        pieces.append(xg)
    return jnp.concatenate(pieces, axis=0)


def _cumsum_rows(x):
    c, w = x.shape
    row = lax.broadcasted_iota(I32, (c, 1), 0)
    s = 1
    while s < c:
        if s % 8 == 0:
            shifted = jnp.concatenate([jnp.zeros((s, w), x.dtype), x[:c - s, :]], axis=0)
        else:
            shifted = jnp.where(row >= s, pltpu.roll(x, s, 0), 0.0)
        x = x + shifted
        s *= 2
    return x


def _dot(a, b):
    return jnp.dot(a, b, preferred_element_type=F32)


def _dot_nt(a, b):
    return lax.dot_general(a, b, (((1,), (1,)), ((), ())), preferred_element_type=F32)


def _dot_tn(a, b):
    return lax.dot_general(a, b, (((0,), (0,)), ((), ())), preferred_element_type=F32)


def _ada_kernel(c_ref, w_ref, b_ref, o_ref):
    c = c_ref[...]
    a = (c * jax.nn.sigmoid(c)).astype(BF16)
    o_ref[...] = _dot(a, w_ref[...].astype(BF16)) + b_ref[...]


def _ada(c_all, w_ada, b_ada):
    depth, d, n6 = w_ada.shape
    s = c_all.shape[0]
    tn = _pick_tile(n6, 1536)
    return pl.pallas_call(
        _ada_kernel,
        grid=(depth, n6 // tn),
        in_specs=[
            pl.BlockSpec((s, d), lambda l, j: (0, 0)),
            pl.BlockSpec((None, d, tn), lambda l, j: (l, 0, j)),
            pl.BlockSpec((None, 1, tn), lambda l, j: (l, 0, j)),
        ],
        out_specs=pl.BlockSpec((None, s, tn), lambda l, j: (l, 0, j)),
        out_shape=jax.ShapeDtypeStruct((depth, s, n6), F32),
        compiler_params=_cparams(("arbitrary", "arbitrary")),
        name="ada",
    )(c_all, w_ada, b_ada.reshape(depth, 1, n6))


def _inproj_kernel(x_ref, g_ref, sc_ref, sh_ref, w_ref, *out_refs, splits, group):
    h = _rms(x_ref[...], g_ref[...])
    h = _group_affine(h, 1.0 + sc_ref[...], sh_ref[...], group).astype(BF16)
    for o_ref, (c0, c1) in zip(out_refs, splits):
        o_ref[...] = _dot(h, w_ref[:, c0:c1]).astype(o_ref.dtype)


def _inproj(x, gain, modg, l, w, widths, group):
    n, d = x.shape
    tm = _pick_tile(n, 512)
    tg = tm // group
    splits, c = [], 0
    for wd in widths:
        splits.append((c, c + wd))
        c += wd
    mod_spec = lambda comp: pl.BlockSpec((None, None, tg, d), lambda i: (l, comp, i, 0))
    return pl.pallas_call(
        functools.partial(_inproj_kernel, splits=tuple(splits), group=group),
        grid=(n // tm,),
        in_specs=[
            pl.BlockSpec((tm, d), lambda i: (i, 0)),
            pl.BlockSpec((1, d), lambda i: (0, 0)),
            mod_spec(1),
            mod_spec(0),
            pl.BlockSpec((d, c), lambda i: (0, 0)),
        ],
        out_specs=[pl.BlockSpec((tm, wd), lambda i: (i, 0)) for wd in widths],
        out_shape=[jax.ShapeDtypeStruct((n, wd), F32) for wd in widths],
        compiler_params=_cparams(("arbitrary",), VMEM_LIMIT),
        name="inproj",
    )(x, gain.reshape(1, d), modg, modg, w)


REC_SEQS = 8


def _rec_kernel(*refs, mode, heads, chunk, nchunks, zero_init, nseqs):
    refs = list(refs)
    ntok = 5 if mode == "gla" else 4
    tok_refs = [refs[s * ntok:(s + 1) * ntok] for s in range(nseqs)]
    refs = refs[nseqs * ntok:]
    if mode == "gla":
        wah_ref, wal_ref, ab_ref, gn_ref = refs[:4]
        refs = refs[4:]
    else:
        lb_ref, gn_ref = refs[:2]
        refs = refs[2:]
    s0_ref = None
    if not zero_init:
        s0_ref = refs.pop(0)
    o_ref, sout_ref = refs[:2]
    st_scr = refs[2:]
    ci = pl.program_id(1)

    @pl.when(ci == 0)
    def _():
        for s in range(nseqs):
            for h in range(heads):
                if zero_init:
                    st_scr[s * heads + h][...] = jnp.zeros((LANE, LANE), F32)
                else:
                    st_scr[s * heads + h][...] = s0_ref[s, h].T

    row = lax.broadcasted_iota(I32, (chunk, chunk), 0)
    col = lax.broadcasted_iota(I32, (chunk, chunk), 1)
    causal = row >= col
    mid = chunk // 2 - 1

    for s in range(nseqs):
        if mode == "gla":
            q_ref, k_ref, v_ref, r_ref, a_ref = tok_refs[s]
            a = a_ref[...]
            a_hi = a.astype(BF16)
            a_lo = (a - a_hi.astype(F32)).astype(BF16)
            alog_all = (_dot(a_hi, wah_ref[...]) + _dot(a_lo, wah_ref[...]) + _dot(a_hi, wal_ref[...])
                        + ab_ref[...])
            g_all = jax.nn.log_sigmoid(alog_all) * (1.0 / GATE_TAU)
            fg_all = None
        else:
            q_ref, k_ref, v_ref, r_ref = tok_refs[s]
            lb = lb_ref[...]
            fg_all = lb + (1.0 - lb) * jax.nn.sigmoid(k_ref[...])
            g_all = jnp.log(fg_all)
        b_all = _cumsum_rows(g_all)

        for h in range(heads):
            sl = slice(h * LANE, (h + 1) * LANE)
            if mode == "gla":
                q = q_ref[:, sl] * (DK_A ** -0.5)
                k = k_ref[:, sl]
            else:
                qr = q_ref[:, sl]
                q = qr * jax.nn.sigmoid(qr)
                k = 1.0 - fg_all[:, sl]
            b = b_all[:, sl]
            b_last = b[chunk - 1:chunk, :]
            b_ref_row = b[mid:mid + 1, :]
            qt = (q * jnp.exp(jnp.minimum(b - b_ref_row, EXP_CLAMP))).astype(BF16)
            kt = (k * jnp.exp(jnp.minimum(b_ref_row - b, EXP_CLAMP))).astype(BF16)
            qs = (q * jnp.exp(b)).astype(BF16)
            ks = (k * jnp.exp(b_last - b)).astype(BF16)
            v = v_ref[:, sl].astype(BF16)
            att = jnp.where(causal, _dot_nt(qt, kt), 0.0).astype(BF16)
            st_ref = st_scr[s * heads + h]
            st = st_ref[...]
            o = _dot(att, v) + _dot_nt(qs, st.astype(BF16))
            st_ref[...] = st * jnp.exp(b_last) + _dot_tn(v, ks)
            rg = r_ref[:, sl]
            o = _rms(o, gn_ref[...]) * (rg * jax.nn.sigmoid(rg))
            o_ref[s, :, sl] = o.astype(o_ref.dtype)

    @pl.when(ci == nchunks - 1)
    def _():
        for s in range(nseqs):
            for h in range(heads):
                sout_ref[s, h] = st_scr[s * heads + h][...].T


def _recurrence(mode, q, k, v, r, extras, gnorm, s0, *, heads, nseq, seqlen, chunk, row0):
    cols = q.shape[1]
    nchunks = seqlen // chunk
    blk0 = row0 // chunk
    ns = REC_SEQS if nseq % REC_SEQS == 0 else 1
    full = lambda a: pl.BlockSpec(a.shape, lambda b, c: (0,) * a.ndim)
    in_specs, args = [], []
    for s in range(ns):
        tok = lambda b, c, s=s: (blk0 + (b * ns + s) * nchunks + c, 0)
        toks = [q, k, v, r] + ([extras[0]] if mode == "gla" else [])
        in_specs += [pl.BlockSpec((chunk, a.shape[1]), tok) for a in toks]
        args += toks
    consts = list(extras[1:]) if mode == "gla" else list(extras)
    consts.append(gnorm)
    in_specs += [full(a) for a in consts]
    args += consts
    zero_init = s0 is None
    if not zero_init:
        s0, s0_layer = s0
        in_specs.append(pl.BlockSpec((None, ns, heads, LANE, LANE), lambda b, c: (s0_layer, b, 0, 0, 0)))
        args.append(s0)
    o, st = pl.pallas_call(
        functools.partial(_rec_kernel, mode=mode, heads=heads, chunk=chunk, nchunks=nchunks,
                          zero_init=zero_init, nseqs=ns),
        grid=(nseq // ns, nchunks),
        in_specs=in_specs,
        out_specs=[pl.BlockSpec((ns, chunk, cols), lambda b, c: (b, c, 0)),
                   pl.BlockSpec((ns, heads, LANE, LANE), lambda b, c: (b, 0, 0, 0))],
        out_shape=[jax.ShapeDtypeStruct((nseq, seqlen, cols), BF16),
                   jax.ShapeDtypeStruct((nseq, heads, LANE, LANE), F32)],
        scratch_shapes=[pltpu.VMEM((LANE, LANE), F32) for _ in range(ns * heads)],
        compiler_params=_cparams(("arbitrary", "arbitrary")),
        name="recurrence_" + mode,
    )(*args)
    return o.reshape(nseq * seqlen, cols), st


def _mla_pre_kernel(cq_ref, ckv_ref, kpe_ref, cosq_ref, sinq_ref, cosk_ref, sink_ref, qg_ref, kvg_ref,
                    wuq_ref, wuk_ref, perm_ref, qcat_ref, kcat_ref, lat_ref, kpeo_ref, qt_ref, latt_ref, *, tq):
    tm = cq_ref.shape[0]
    cqn = _rms(cq_ref[...], qg_ref[...]).astype(BF16)
    qf = _dot(cqn, wuq_ref[...])
    off = H_B * NOPE
    x1 = qf[:, off:off + LANE]
    x2 = qf[:, off + LANE:off + 2 * LANE]
    cq, sq = cosq_ref[...], sinq_ref[...]
    o1 = (x1 * cq - x2 * sq) * MLA_SCALE
    o2 = (x2 * cq + x1 * sq) * MLA_SCALE
    pe = _dot(o1.astype(BF16), perm_ref[0:LANE, :]) + _dot(o2.astype(BF16), perm_ref[LANE:2 * LANE, :])
    for h in range(H_B):
        ql = _dot(qf[:, h * NOPE:(h + 1) * NOPE].astype(BF16), wuk_ref[h]) * MLA_SCALE
        peh = pe[:, h * LANE:(h + 1) * LANE]
        qcat_ref[h, :, 0:KV_LORA] = ql.astype(BF16)
        qcat_ref[h, :, KV_LORA:KCAT] = peh.astype(BF16)
        for jb in range(tm // tq):
            rows = slice(jb * tq, (jb + 1) * tq)
            cols = slice(h * tq, (h + 1) * tq)
            qt_ref[jb, 0:KV_LORA, cols] = ql[rows, :].T.astype(BF16)
            qt_ref[jb, KV_LORA:KCAT, cols] = peh[rows, :].T.astype(BF16)
    latn = _rms(ckv_ref[...], kvg_ref[...])
    lat_ref[...] = latn
    latt_ref[...] = latn.T.astype(BF16)
    x = kpe_ref[...]
    half = ROPE // 2
    lane = lax.broadcasted_iota(I32, x.shape, 1)
    rot = jnp.where(lane < half, -pltpu.roll(x, LANE - half, 1), pltpu.roll(x, half, 1))
    kro = x * cosk_ref[...] + rot * sink_ref[...]
    kpeo_ref[...] = kro[:, 0:ROPE]
    kcat_ref[:, 0:KV_LORA] = latn.astype(BF16)
    kcat_ref[:, KV_LORA:KCAT] = kro.astype(BF16)


ATTN_TQ = 8 * CHUNK


def _mla_pre(zcq, zckv, zkpe, tabs, qg, kvg, wuq, wuk, perm):
    n = zcq.shape[0]
    tm = _pick_tile(n, 512)
    tq = ATTN_TQ
    assert tm % tq == 0
    tokspec = lambda wd: pl.BlockSpec((tm, wd), lambda i: (i, 0))
    full = lambda a: pl.BlockSpec(a.shape, lambda i: (0,) * a.ndim)
    return pl.pallas_call(
        functools.partial(_mla_pre_kernel, tq=tq),
        grid=(n // tm,),
        in_specs=[tokspec(Q_LORA), tokspec(KV_LORA), tokspec(LANE)] + [tokspec(LANE)] * 4
                 + [full(qg), full(kvg), full(wuq), full(wuk), full(perm)],
        out_specs=[pl.BlockSpec((H_B, tm, KCAT), lambda i: (0, i, 0)), tokspec(KCAT),
                   tokspec(KV_LORA), tokspec(ROPE),
                   pl.BlockSpec((tm // tq, KCAT, H_B * tq), lambda i: (i, 0, 0)),
                   pl.BlockSpec((KV_LORA, tm), lambda i: (0, i))],
        out_shape=[jax.ShapeDtypeStruct((H_B, n, KCAT), BF16), jax.ShapeDtypeStruct((n, KCAT), BF16),
                   jax.ShapeDtypeStruct((n, KV_LORA), F32), jax.ShapeDtypeStruct((n, ROPE), F32),
                   jax.ShapeDtypeStruct((n // tq, KCAT, H_B * tq), BF16),
                   jax.ShapeDtypeStruct((KV_LORA, n), BF16)],
        compiler_params=_cparams(("arbitrary",)),
        name="mla_pre",
    )(zcq, zckv, zkpe, *tabs, qg, kvg, wuq, wuk, perm)


def _softmax_update(s, vals, m_scr, l_scr, acc_scr):
    m_prev = m_scr[...]
    m_new = jnp.maximum(m_prev, jnp.max(s, axis=1, keepdims=True))
    alpha = jnp.exp(m_prev - m_new)
    p = jnp.exp(s - m_new)
    l_scr[...] = alpha * l_scr[...] + jnp.sum(p, axis=1, keepdims=True)
    acc_scr[...] = alpha * acc_scr[...] + _dot(p.astype(BF16), vals)
    m_scr[...] = m_new


def _softmax_init(m_scr, l_scr, acc_scr):
    m_scr[...] = jnp.full(m_scr.shape, -jnp.inf, F32)
    l_scr[...] = jnp.zeros(l_scr.shape, F32)
    acc_scr[...] = jnp.zeros(acc_scr.shape, F32)


def _attn_finish(o_ref, wuv_ref, l_scr, acc_scr, tq):
    inv = 1.0 / l_scr[...]
    for h in range(H_B):
        rows = slice(h * tq, (h + 1) * tq)
        oh = (acc_scr[rows, :] * inv[rows, :]).astype(BF16)
        o_ref[:, h * V_B:(h + 1) * V_B] = _dot(oh, wuv_ref[h]).astype(o_ref.dtype)


def _attn_prompt_kernel(qt_ref, k_ref, latt_ref, wuvt_ref, o_ref, m_scr, l_scr, acc_scr, p_scr, *, tq, tk):
    qi = pl.program_id(1)
    cols = H_B * tq
    qt = qt_ref[...]
    m_scr[...] = jnp.full(m_scr.shape, -jnp.inf, F32)
    l_scr[...] = jnp.zeros(l_scr.shape, F32)
    acc_scr[...] = jnp.zeros(acc_scr.shape, F32)

    def block(start, masked):
        s = _dot(k_ref[pl.ds(start, tk), :], qt)
        if masked:
            tok = qi * tq + (lax.broadcasted_iota(I32, (1, cols), 1) & (tq - 1))
            limit = (tok // CHUNK + 1) * CHUNK
            key = start + lax.broadcasted_iota(I32, (tk, 1), 0)
            s = jnp.where(key < limit, s, -jnp.inf)
        m_prev = m_scr[...]
        m_new = jnp.maximum(m_prev, jnp.max(s, axis=0, keepdims=True))
        alpha = jnp.exp(m_prev - m_new)
        p = jnp.exp(s - m_new)
        l_scr[...] = alpha * l_scr[...] + jnp.sum(p, axis=0, keepdims=True)
        m_scr[...] = m_new
        p_scr[...] = p.astype(BF16)
        acc_scr[...] = alpha * acc_scr[...] + _dot(latt_ref[:, pl.ds(start, tk)], p_scr[...])

    nfull = (qi * tq) // tk

    def full(j, carry):
        block(pl.multiple_of(j * tk, tk), False)
        return carry

    lax.fori_loop(0, nfull, full, 0)
    block(pl.multiple_of(nfull * tk, tk), True)

    ot =(acc_scr[...] * (1.0 / l_scr[...])).astype(BF16)
    for h in range(H_B):
        oh = _dot(wuvt_ref[h], ot[:, h * tq:(h + 1) * tq])
        o_ref[:, h * V_B:(h + 1) * V_B] = oh.T.astype(o_ref.dtype)


def _attn_prompt(qt, kcat, latt, wuvt, nseq, seqlen):
    n = nseq * seqlen
    tq = ATTN_TQ
    tk = 512
    assert seqlen % tk == 0 and tk % tq == 0
    nq = seqlen // tq
    return pl.pallas_call(
        functools.partial(_attn_prompt_kernel, tq=tq, tk=tk),
        grid=(nseq, nq),
        in_specs=[
            pl.BlockSpec((None, KCAT, H_B * tq), lambda b, qi: (b * nq + qi, 0, 0)),
            pl.BlockSpec((seqlen, KCAT), lambda b, qi: (b, 0)),
            pl.BlockSpec((KV_LORA, seqlen), lambda b, qi: (0, b)),
            pl.BlockSpec(wuvt.shape, lambda b, qi: (0, 0, 0)),
        ],
        out_specs=pl.BlockSpec((tq, H_B * V_B), lambda b, qi: (b * nq + qi, 0)),
        out_shape=jax.ShapeDtypeStruct((n, H_B * V_B), BF16),
        scratch_shapes=[pltpu.VMEM((1, H_B * tq), F32), pltpu.VMEM((1, H_B * tq), F32),
                        pltpu.VMEM((KV_LORA, H_B * tq), F32), pltpu.VMEM((tk, H_B * tq), BF16)],
        compiler_params=_cparams(("arbitrary", "arbitrary")),
        name="attn_prompt",
    )(qt, kcat, latt, wuvt)


def _attn_sample_kernel(q_ref, plat_ref, pkpe_ref, knew_ref, wuv_ref, o_ref,
                        m_scr, l_scr, acc_scr, *, tq, nkp):
    ki = pl.program_id(1)

    @pl.when(ki == 0)
    def _():
        _softmax_init(m_scr, l_scr, acc_scr)

    q = q_ref[...].reshape(H_B * tq, KCAT)
    lat = plat_ref[...].astype(BF16)
    kpe = pkpe_ref[...].astype(BF16)
    s = _dot_nt(q[:, 0:KV_LORA], lat) + _dot_nt(q[:, KV_LORA:KV_LORA + ROPE], kpe)
    _softmax_update(s, lat, m_scr, l_scr, acc_scr)

    @pl.when(ki == nkp - 1)
    def _():
        kn = knew_ref[...]
        _softmax_update(_dot_nt(q, kn), kn[:, 0:KV_LORA], m_scr, l_scr, acc_scr)
        _attn_finish(o_ref, wuv_ref, l_scr, acc_scr, tq)


def _attn_sample(qcat, kcat, past_lat, past_kpe, layer, wuv, row0):
    _, nseq, past, _ = past_lat.shape
    tq = (kcat.shape[0] - row0) // nseq
    tkp = _pick_tile(past, 2048)
    nkp = past // tkp
    blk0 = row0 // tq
    return pl.pallas_call(
        functools.partial(_attn_sample_kernel, tq=tq, nkp=nkp),
        grid=(nseq, nkp),
        in_specs=[
            pl.BlockSpec((H_B, tq, KCAT), lambda b, ki: (0, blk0 + b, 0)),
            pl.BlockSpec((None, None, tkp, KV_LORA), lambda b, ki: (layer, b, ki, 0)),
            pl.BlockSpec((None, None, tkp, ROPE), lambda b, ki: (layer, b, ki, 0)),
            pl.BlockSpec((tq, KCAT), lambda b, ki: (blk0 + b, 0)),
            pl.BlockSpec(wuv.shape, lambda b, ki: (0, 0, 0)),
        ],
        out_specs=pl.BlockSpec((tq, H_B * V_B), lambda b, ki: (b, 0)),
        out_shape=jax.ShapeDtypeStruct((nseq * tq, H_B * V_B), BF16),
        scratch_shapes=[pltpu.VMEM((H_B * tq, 1), F32), pltpu.VMEM((H_B * tq, 1), F32),
                        pltpu.VMEM((H_B * tq, KV_LORA), F32)],
        compiler_params=_cparams(("arbitrary", "arbitrary")),
        name="attn_sample",
    )(qcat, past_lat, past_kpe, kcat, wuv)


def _outproj_router_kernel(*refs, nlhs, group, ptiles, nexp):
    x_ref, g_ref = refs[0], refs[1]
    lhs_p = refs[2:2 + nlhs]
    lhs_s = refs[2 + nlhs:2 + 2 * nlhs]
    ws = refs[2 + 2 * nlhs:2 + 3 * nlhs]
    n2_ref, sc_ref, sh_ref, wh_ref, wl_ref, br_ref, o_ref, h_ref, ti_ref, gt_ref = refs[2 + 3 * nlhs:]
    is_prompt = pl.program_id(0) < ptiles
    acc = None
    for ap, asm, w in zip(lhs_p, lhs_s, ws):
        a = jnp.where(is_prompt, ap[...], asm[...])
        t = _dot(a, w[...])
        acc = t if acc is None else acc + t
    x = x_ref[...] + _group_affine(acc, g_ref[...], None, group)
    o_ref[...] = x
    _route(x, n2_ref, sc_ref, sh_ref, wh_ref, wl_ref, br_ref, h_ref, ti_ref, gt_ref, group, nexp)


def _outproj_router(x, modg, l, lhs_p, lhs_s, ws, gain2, wr_hi, wr_lo, br, group, nexp):
    n, d = x.shape
    n_p, n_s = lhs_p[0].shape[0], lhs_s[0].shape[0]
    tm = _pick_tile(math.gcd(n_p, n_s), 512)
    tg = tm // group
    ptiles = n_p // tm
    mod_spec = lambda comp: pl.BlockSpec((None, None, tg, d), lambda i: (l, comp, i, 0))
    full = lambda a: pl.BlockSpec(a.shape, lambda i: (0,) * a.ndim)
    tok = lambda wd: pl.BlockSpec((tm, wd), lambda i: (i, 0))
    return pl.pallas_call(
        functools.partial(_outproj_router_kernel, nlhs=len(ws), group=group, ptiles=ptiles, nexp=nexp),
        grid=(n // tm,),
        in_specs=[tok(d), mod_spec(2)]
                 + [pl.BlockSpec((tm, a.shape[1]), lambda i: (jnp.minimum(i, ptiles - 1), 0)) for a in lhs_p]
                 + [pl.BlockSpec((tm, a.shape[1]), lambda i: (jnp.maximum(i - ptiles, 0), 0)) for a in lhs_s]
                 + [full(w) for w in ws]
                 + [pl.BlockSpec((1, d), lambda i: (0, 0)), mod_spec(4), mod_spec(3),
                    full(wr_hi), full(wr_lo), full(br)],
        out_specs=[tok(d), tok(d), tok(LANE), tok(LANE)],
        out_shape=[jax.ShapeDtypeStruct((n, d), F32), jax.ShapeDtypeStruct((n, d), F32),
                   jax.ShapeDtypeStruct((n, LANE), I32), jax.ShapeDtypeStruct((n, LANE), F32)],
        compiler_params=_cparams(("arbitrary",), VMEM_LIMIT),
        name="outproj_router",
    )(x, modg, *lhs_p, *lhs_s, *ws, gain2.reshape(1, d), modg, modg, wr_hi, wr_lo, br)


def _route(x, g_ref, sc_ref, sh_ref, wh_ref, wl_ref, br_ref, h_ref, ti_ref, gt_ref, group, nexp):
    h = _rms(x, g_ref[...])
    h = _group_affine(h, 1.0 + sc_ref[...], sh_ref[...], group)
    h_ref[...] = h
    h_hi = h.astype(BF16)
    h_lo = (h - h_hi.astype(F32)).astype(BF16)
    logits = _dot(h_hi, wh_ref[...]) + _dot(h_lo, wh_ref[...]) + _dot(h_hi, wl_ref[...]) + br_ref[...]
    lane = lax.broadcasted_iota(I32, logits.shape, 1)
    lane_f = lane.astype(F32)
    cur = jnp.where(lane < nexp, logits, -jnp.inf)
    tops, idxs = [], []
    for _ in range(TOP_K):
        m = jnp.max(cur, axis=1, keepdims=True)
        i = jnp.min(jnp.where(cur == m, lane_f, float(LANE)), axis=1, keepdims=True)
        cur = jnp.where(lane_f == i, -jnp.inf, cur)
        tops.append(m)
        idxs.append(i.astype(I32))
    es = [jnp.exp(t - tops[0]) for t in tops]
    inv = 1.0 / (es[0] + es[1] + es[2] + es[3])
    ti = jnp.zeros(logits.shape, I32)
    gt = jnp.zeros(logits.shape, F32)
    for k in range(TOP_K):
        ti = jnp.where(lane == k, idxs[k], ti)
        gt = jnp.where(lane == k, es[k] * inv, gt)
    ti_ref[...] = ti
    gt_ref[...] = gt


def _rank_kernel(ti_ref, rank_ref, cnt_ref, carry_scr, *, tm):
    i = pl.program_id(0)

    @pl.when(i == 0)
    def _():
        carry_scr[...] = jnp.zeros(carry_scr.shape, F32)

    ti = ti_ref[...]
    lane = lax.broadcasted_iota(I32, ti.shape, 1)
    sel = [lane == ti[:, k:k + 1] for k in range(TOP_K)]
    hot = jnp.zeros(ti.shape, F32)
    for s in sel:
        hot = hot + jnp.where(s, 1.0, 0.0)
    row = lax.broadcasted_iota(I32, (tm, tm), 0)
    col = lax.broadcasted_iota(I32, (tm, tm), 1)
    strict = jnp.where(row > col, 1.0, 0.0).astype(BF16)
    before = _dot(strict, hot.astype(BF16)) + carry_scr[0:1, :]
    rank = jnp.zeros(ti.shape, F32)
    for k in range(TOP_K):
        rk = jnp.sum(jnp.where(sel[k], before, 0.0), axis=1, keepdims=True)
        rank = jnp.where(lane == k, rk, rank)
    rank_ref[...] = rank.astype(I32)
    carry_scr[...] = carry_scr[...] + jnp.sum(hot, axis=0, keepdims=True)
    cnt_ref[...] = carry_scr[...]


def _rank(topi):
    n = topi.shape[0]
    tm = _pick_tile(n, 512)
    return pl.pallas_call(
        functools.partial(_rank_kernel, tm=tm),
        grid=(n // tm,),
        in_specs=[pl.BlockSpec((tm, LANE), lambda i: (i, 0))],
        out_specs=[pl.BlockSpec((tm, LANE), lambda i: (i, 0)), pl.BlockSpec((8, LANE), lambda i: (0, 0))],
        out_shape=[jax.ShapeDtypeStruct((n, LANE), I32), jax.ShapeDtypeStruct((8, LANE), F32)],
        scratch_shapes=[pltpu.VMEM((8, LANE), F32)],
        compiler_params=_cparams(("arbitrary",)),
        name="rank",
    )(topi)


def _dest_kernel(ti_ref, rank_ref, ps_ref, d_ref):
    ti = ti_ref[...]
    lane = lax.broadcasted_iota(I32, ti.shape, 1)
    ps = ps_ref[...]
    dest = rank_ref[...]
    for k in range(TOP_K):
        base = jnp.sum(jnp.where(lane == ti[:, k:k + 1], ps, 0.0), axis=1, keepdims=True).astype(I32)
        dest = dest + jnp.where(lane == k, base, 0)
    d_ref[...] = dest


def _dest(topi, rank, pstarts_row):
    n = topi.shape[0]
    tm = _pick_tile(n, 512)
    tok = pl.BlockSpec((tm, LANE), lambda i: (i, 0))
    return pl.pallas_call(
        _dest_kernel,
        grid=(n // tm,),
        in_specs=[tok, tok, pl.BlockSpec((1, LANE), lambda i: (0, 0))],
        out_specs=tok,
        out_shape=jax.ShapeDtypeStruct((n, LANE), I32),
        compiler_params=_cparams(("arbitrary",)),
        name="dest",
    )(topi, rank, pstarts_row)


def _dispatch_kernel(dest_ref, pad_ref, h_ref, xs_ref, hbuf, zbuf, sem, zsem, *, tm, nexp, nblocks):
    i = pl.program_id(0)
    slot = lax.rem(i, 2)
    hbuf[slot] = h_ref[...]

    def issue(it, carry):
        for u in range(ISSUE_UNROLL):
            r = it * ISSUE_UNROLL + u
            for k in range(TOP_K):
                d = dest_ref[r * TOP_K + k]
                pltpu.make_async_copy(hbuf.at[slot, pl.ds(r, 1)], xs_ref.at[pl.ds(d, 1)], sem.at[slot]).start()
        return carry

    lax.fori_loop(0, tm // ISSUE_UNROLL, issue, 0)

    def retire(s):
        for _ in range(TOP_K):
            pltpu.make_async_copy(hbuf.at[s], xs_ref.at[pl.ds(0, tm)], sem.at[s]).wait()

    @pl.when(i >= 1)
    def _():
        retire(1 - slot)

    @pl.when(i == pl.num_programs(0) - 1)
    def _():
        retire(slot)
        zbuf[...] = jnp.zeros(zbuf.shape, F32)

        def fill_expert(e, carry):
            end = pad_ref[0, e]
            npad = pad_ref[1, e]
            p = MOE_BLOCK // 2
            while p >= 1:
                bit = npad & p
                end = end - bit

                @pl.when(bit != 0)
                def _(end=end, p=p):
                    if p >= 8:
                        start = pl.multiple_of(end, 8)
                        pltpu.make_async_copy(zbuf.at[pl.ds(0, p)], xs_ref.at[pl.ds(start, p)], zsem).start()
                    else:
                        for q in range(p):
                            pltpu.make_async_copy(zbuf.at[pl.ds(0, 1)], xs_ref.at[pl.ds(end + q, 1)],
                                                  zsem).start()

                p //= 2
            return carry

        lax.fori_loop(0, nexp, fill_expert, 0)
        nused = pad_ref[2, 0]

        def fill_tail(b, carry):
            @pl.when(b >= nused)
            def _():
                start = pl.multiple_of(b * MOE_BLOCK, MOE_BLOCK)
                pltpu.make_async_copy(zbuf, xs_ref.at[pl.ds(start, MOE_BLOCK)], zsem).start()

            return carry

        lax.fori_loop(0, nblocks, fill_tail, 0)
        for _ in range(nexp):
            pltpu.make_async_copy(zbuf, xs_ref.at[pl.ds(0, MOE_BLOCK)], zsem).wait()


def _dispatch(dest1d, padinfo, h, rows, nexp):
    n, d = h.shape
    tm = _pick_tile(n, 256)
    return pl.pallas_call(
        functools.partial(_dispatch_kernel, tm=tm, nexp=nexp, nblocks=rows // MOE_BLOCK),
        grid=(n // tm,),
        in_specs=[pl.BlockSpec((tm * TOP_K,), lambda i: (i,), memory_space=pltpu.SMEM),
                  pl.BlockSpec(memory_space=pltpu.SMEM),
                  pl.BlockSpec((tm, d), lambda i: (i, 0))],
        out_specs=pl.BlockSpec(memory_space=pl.ANY),
        out_shape=jax.ShapeDtypeStruct((rows, d), F32),
        scratch_shapes=[pltpu.VMEM((2, tm, d), F32), pltpu.VMEM((MOE_BLOCK, d), F32),
                        pltpu.SemaphoreType.DMA((2,)), pltpu.SemaphoreType.DMA(())],
        compiler_params=pltpu.CompilerParams(dimension_semantics=("arbitrary",), has_side_effects=True),
        name="dispatch",
    )(dest1d, padinfo, h)


def _experts_kernel(bexp_ref, nused_ref, xs_ref, w1_ref, b1_ref, w2_ref, b2_ref, y_ref,
                    w1b, w2b, act_scr, prev_scr, *, dff, dm):
    i = pl.program_id(0)
    nu = nused_ref[0]
    e = bexp_ref[jnp.minimum(i, nu - 1)]

    @pl.when(i == 0)
    def _():
        prev_scr[0] = -1

    @pl.when(i >= nu)
    def _():
        y_ref[...] = jnp.zeros(y_ref.shape, F32)

    @pl.when(i < nu)
    def _():
        @pl.when(e != prev_scr[0])
        def _():
            rows = 128

            def cast1(c, carry):
                r0 = pl.multiple_of(c * rows, rows)
                w1b[pl.ds(r0, rows), :] = w1_ref[pl.ds(r0, rows), :].astype(BF16)
                return carry

            def cast2(c, carry):
                r0 = pl.multiple_of(c * rows, rows)
                w2b[pl.ds(r0, rows), :] = w2_ref[pl.ds(r0, rows), :].astype(BF16)
                return carry

            lax.fori_loop(0, dm // rows, cast1, 0)
            lax.fori_loop(0, dff // rows, cast2, 0)
            prev_scr[0] = e

        x = xs_ref[...].astype(BF16)
        cw = 256
        for c in range(dff // cw):
            gt = _dot(x, w1b[:, c * cw:(c + 1) * cw]) + b1_ref[:, c * cw:(c + 1) * cw]
            up = _dot(x, w1b[:, dff + c * cw:dff + (c + 1) * cw]) + b1_ref[:, dff + c * cw:dff + (c + 1) * cw]
            gt = jnp.minimum(gt, SWIGLU_LIMIT)
            up = jnp.clip(up, -SWIGLU_LIMIT, SWIGLU_LIMIT)
            act = gt * jax.nn.sigmoid(gt * SWIGLU_ALPHA) * (up + 1.0)
            act_scr[:, c * cw:(c + 1) * cw] = act.astype(BF16)
        a = act_scr[...]
        for c in range(dm // cw):
            y_ref[:, c * cw:(c + 1) * cw] = _dot(a, w2b[:, c * cw:(c + 1) * cw]) + b2_ref[:, c * cw:(c + 1) * cw]


def _experts(bexp, nused, xs, w1, b1, w2, b2, l):
    rows, dm = xs.shape
    dff = w2.shape[2]
    nb = rows // MOE_BLOCK
    blk = lambda i, be, nu: (jnp.minimum(i, nu[0] - 1), 0)
    wsel = lambda i, be, nu: (l, be[jnp.minimum(i, nu[0] - 1)], 0, 0)
    grid_spec = pltpu.PrefetchScalarGridSpec(
        num_scalar_prefetch=2,
        grid=(nb,),
        in_specs=[
            pl.BlockSpec((MOE_BLOCK, dm), blk),
            pl.BlockSpec((None, None, dm, 2 * dff), wsel),
            pl.BlockSpec((None, None, 1, 2 * dff), wsel),
            pl.BlockSpec((None, None, dff, dm), wsel),
            pl.BlockSpec((None, None, 1, dm), wsel),
        ],
        out_specs=pl.BlockSpec((MOE_BLOCK, dm), lambda i, be, nu: (i, 0)),
        scratch_shapes=[pltpu.VMEM((dm, 2 * dff), BF16), pltpu.VMEM((dff, dm), BF16),
                        pltpu.VMEM((MOE_BLOCK, dff), BF16), pltpu.SMEM((1,), I32)],
    )
    return pl.pallas_call(
        functools.partial(_experts_kernel, dff=dff, dm=dm),
        grid_spec=grid_spec,
        out_shape=jax.ShapeDtypeStruct((rows, dm), F32),
        compiler_params=_cparams(("arbitrary",), VMEM_LIMIT),
        name="experts",
    )(bexp, nused, xs, w1, b1, w2, b2)


def _combine_kernel(dest_ref, dnext_ref, gate_ref, x_ref, g_ref, y_ref, o_ref, buf, sem, *, tm, group):
    i = pl.program_id(0)
    slot = lax.rem(i, 2)

    def gather(idx_ref, s):
        def issue(it, carry):
            for u in range(ISSUE_UNROLL):
                r = it * ISSUE_UNROLL + u
                for k in range(TOP_K):
                    d = idx_ref[r * TOP_K + k]
                    pltpu.make_async_copy(y_ref.at[pl.ds(d, 1)], buf.at[s, k, pl.ds(r, 1)], sem.at[s]).start()
            return carry

        lax.fori_loop(0, tm // ISSUE_UNROLL, issue, 0)

    @pl.when(i == 0)
    def _():
        gather(dest_ref, 0)

    @pl.when(i + 1 < pl.num_programs(0))
    def _():
        gather(dnext_ref, 1 - slot)

    for k in range(TOP_K):
        pltpu.make_async_copy(y_ref.at[pl.ds(0, tm)], buf.at[slot, k], sem.at[slot]).wait()
    gate = gate_ref[...]
    moe = gate[:, 0:1] * buf[slot, 0]
    for k in range(1, TOP_K):
        moe = moe + gate[:, k:k + 1] * buf[slot, k]
    o_ref[...] = x_ref[...] + _group_affine(moe, g_ref[...], None, group)


def _combine(dest1d, gate, x, modg, l, y, group):
    n, d = x.shape
    tm = _pick_tile(n, 256)
    tg = tm // group
    nt = n // tm
    return pl.pallas_call(
        functools.partial(_combine_kernel, tm=tm, group=group),
        grid=(nt,),
        in_specs=[pl.BlockSpec((tm * TOP_K,), lambda i: (i,), memory_space=pltpu.SMEM),
                  pl.BlockSpec((tm * TOP_K,), lambda i: (jnp.minimum(i + 1, nt - 1),), memory_space=pltpu.SMEM),
                  pl.BlockSpec((tm, LANE), lambda i: (i, 0)),
                  pl.BlockSpec((tm, d), lambda i: (i, 0)),
                  pl.BlockSpec((None, None, tg, d), lambda i: (l, 5, i, 0)),
                  pl.BlockSpec(memory_space=pl.ANY)],
        out_specs=pl.BlockSpec((tm, d), lambda i: (i, 0)),
        out_shape=jax.ShapeDtypeStruct((n, d), F32),
        scratch_shapes=[pltpu.VMEM((2, TOP_K, tm, d), F32), pltpu.SemaphoreType.DMA((2,))],
        compiler_params=_cparams(("arbitrary",), VMEM_LIMIT),
        name="combine",
    )(dest1d, dest1d, gate, x, modg, y)


def _combine_inproj_kernel(dest_ref, dnext_ref, gate_ref, x_ref, g2_ref, y_ref, n1_ref, sc_ref, sh_ref, w_ref,
                           xo_ref, *rest, tm, group, splits):
    out_refs = rest[:len(splits)]
    buf0, buf1, sem = rest[len(splits):]
    bufs = (buf0, buf1)
    i = pl.program_id(0)
    last = pl.num_programs(0) - 1

    def issue_rows(idx_ref, s, r0, r1):
        for r in range(r0, r1):
            for k in range(TOP_K):
                d = idx_ref[r * TOP_K + k]
                pltpu.make_async_copy(y_ref.at[pl.ds(d, 1)], bufs[s].at[k, pl.ds(r, 1)], sem.at[s]).start()

    def retire(s):
        for k in range(TOP_K):
            pltpu.make_async_copy(y_ref.at[pl.ds(0, tm)], bufs[s].at[k], sem.at[s]).wait()

    @pl.when(i == 0)
    def _():
        def first(it, carry):
            for u in range(ISSUE_UNROLL):
                r = it * ISSUE_UNROLL + u
                for k in range(TOP_K):
                    d = dest_ref[r * TOP_K + k]
                    pltpu.make_async_copy(y_ref.at[pl.ds(d, 1)], buf0.at[k, pl.ds(r, 1)], sem.at[0]).start()
            return carry

        lax.fori_loop(0, tm // ISSUE_UNROLL, first, 0)

    def step(s):
        o = 1 - s
        retire(s)
        per = tm // len(splits)
        issue_rows(dnext_ref, o, 0, per)
        gate = gate_ref[...]
        moe = gate[:, 0:1] * bufs[s][0]
        for k in range(1, TOP_K):
            moe = moe + gate[:, k:k + 1] * bufs[s][k]
        x = x_ref[...] + _group_affine(moe, g2_ref[...], None, group)
        xo_ref[...] = x
        h = _rms(x, n1_ref[...])
        h = _group_affine(h, 1.0 + sc_ref[...], sh_ref[...], group).astype(BF16)
        for gi, (o_ref, (c0, c1)) in enumerate(zip(out_refs, splits)):
            if gi:
                issue_rows(dnext_ref, o, gi * per, (gi + 1) * per)
            o_ref[...] = _dot(h, w_ref[:, c0:c1]).astype(o_ref.dtype)

        @pl.when(i == last)
        def _():
            retire(o)

    parity = lax.rem(i, 2)

    @pl.when(parity == 0)
    def _():
        step(0)

    @pl.when(parity == 1)
    def _():
        step(1)


def _combine_inproj(dest1d, gate, x, modg, l, y, gain1, w, widths, group):
    n, d = x.shape
    tm = _pick_tile(n, 256)
    assert tm % len(widths) == 0
    tg = tm // group
    nt = n // tm
    splits, c = [], 0
    for wd in widths:
        splits.append((c, c + wd))
        c += wd
    mod_spec = lambda layer, comp: pl.BlockSpec((None, None, tg, d), lambda i: (layer, comp, i, 0))
    tok = lambda wd: pl.BlockSpec((tm, wd), lambda i: (i, 0))
    return pl.pallas_call(
        functools.partial(_combine_inproj_kernel, tm=tm, group=group, splits=tuple(splits)),
        grid=(nt,),
        in_specs=[pl.BlockSpec((tm * TOP_K,), lambda i: (i,), memory_space=pltpu.SMEM),
                  pl.BlockSpec((tm * TOP_K,), lambda i: (jnp.minimum(i + 1, nt - 1),), memory_space=pltpu.SMEM),
                  tok(LANE), tok(d), mod_spec(l, 5),
                  pl.BlockSpec(memory_space=pl.ANY),
                  pl.BlockSpec((1, d), lambda i: (0, 0)), mod_spec(l + 1, 1), mod_spec(l + 1, 0),
                  pl.BlockSpec((d, c), lambda i: (0, 0))],
        out_specs=[tok(d)] + [tok(wd) for wd in widths],
        out_shape=[jax.ShapeDtypeStruct((n, d), F32)] + [jax.ShapeDtypeStruct((n, wd), F32) for wd in widths],
        scratch_shapes=[pltpu.VMEM((TOP_K, tm, d), F32), pltpu.VMEM((TOP_K, tm, d), F32),
                        pltpu.SemaphoreType.DMA((2,))],
        compiler_params=_cparams(("arbitrary",), VMEM_LIMIT),
        name="combine_inproj",
    )(dest1d, dest1d, gate, x, modg, y, gain1.reshape(1, d), modg, modg, w)


def _final_norm_kernel(x_ref, g_ref, o_ref):
    o_ref[...] = _rms(x_ref[...], g_ref[...])


def _final_norm(x, g, row0, nrows):
    d = x.shape[1]
    tm = _pick_tile(math.gcd(row0, nrows) if row0 else nrows, 512)
    blk0 = row0 // tm
    return pl.pallas_call(
        _final_norm_kernel,
        grid=(nrows // tm,),
        in_specs=[pl.BlockSpec((tm, d), lambda i: (blk0 + i, 0)), pl.BlockSpec((1, d), lambda i: (0, 0))],
        out_specs=pl.BlockSpec((tm, d), lambda i: (i, 0)),
        out_shape=jax.ShapeDtypeStruct((nrows, d), F32),
        compiler_params=_cparams(("arbitrary",)),
        name="final_norm",
    )(x, g.reshape(1, d))


def _pad_heads(w, heads, dk):
    lead = w.shape[:-1]
    w = w.reshape(*lead, heads, dk)
    w = jnp.pad(w, [(0, 0)] * len(lead) + [(0, 0), (0, LANE - dk)])
    return w.reshape(*lead, heads * LANE)


def _pad_cols(w, width):
    return jnp.pad(w, [(0, 0)] * (w.ndim - 1) + [(0, width - w.shape[-1])])


EVEN_WIDTHS = (H_A * LANE, H_A * LANE, H_A * DV_A, H_A * DV_A, LANE, Q_LORA, KV_LORA, LANE)


def _even_weight(w):
    a_qk, a_v = H_A * DK_A, H_A * DV_A
    c = [0, a_qk, 2 * a_qk, 2 * a_qk + a_v, 2 * a_qk + 2 * a_v, 2 * a_qk + 2 * a_v + GATE_RANK]
    c.append(c[-1] + Q_LORA)
    c.append(c[-1] + KV_LORA)
    c.append(c[-1] + ROPE)
    parts = [
        _pad_heads(w[:, c[0]:c[1]], H_A, DK_A),
        _pad_heads(w[:, c[1]:c[2]], H_A, DK_A),
        w[:, c[2]:c[3]],
        w[:, c[3]:c[4]],
        _pad_cols(w[:, c[4]:c[5]], LANE),
        w[:, c[5]:c[6]],
        w[:, c[6]:c[7]],
        _pad_cols(w[:, c[7]:c[8]], LANE),
    ]
    return jnp.concatenate(parts, axis=1).astype(BF16)


def _uq_weight(w):
    w = w.reshape(Q_LORA, H_B, NOPE + ROPE)
    half = ROPE // 2
    nope = w[:, :, :NOPE].reshape(Q_LORA, H_B * NOPE)
    r1 = w[:, :, NOPE:NOPE + half].reshape(Q_LORA, H_B * half)
    r2 = w[:, :, NOPE + half:].reshape(Q_LORA, H_B * half)
    return jnp.concatenate([nope, r1, r2], axis=1).astype(BF16)


def _rope_perm():
    half = ROPE // 2
    r = jnp.arange(2 * LANE)
    second = r // LANE
    h = (r % LANE) // half
    i = r % half
    col = h * LANE + second * half + i
    return (col[:, None] == jnp.arange(H_B * LANE)[None, :]).astype(BF16)


def _rope_tables(segments):
    half = ROPE // 2
    inv_freq = jnp.exp(-math.log(ROPE_BASE) * jnp.arange(half, dtype=F32) / half)
    cos, sin = [], []
    for pos0, length, reps in segments:
        ang = (jnp.arange(length) + pos0).astype(F32)[:, None] * inv_freq[None, :]
        cos.append(jnp.tile(jnp.cos(ang), (reps, 1)))
        sin.append(jnp.tile(jnp.sin(ang), (reps, 1)))
    cos, sin = jnp.concatenate(cos, axis=0), jnp.concatenate(sin, axis=0)
    z = jnp.zeros((cos.shape[0], LANE - ROPE), F32)
    cosq, sinq = jnp.tile(cos, (1, LANE // half)), jnp.tile(sin, (1, LANE // half))
    cosk = jnp.concatenate([cos, cos, z], axis=1)
    sink = jnp.concatenate([sin, sin, z], axis=1)
    return cosq, sinq, cosk, sink


def kernel(x_prompt, x_sample, cache_mla_latent, cache_mla_krope, state_gla, state_hgrn, c_prompt, c_sample,
           w_ada, b_ada, norm1_g, norm2_g, w_in_even, w_gla_a2, b_gla_a, gla_norm_g, mla_q_norm_g, w_mla_uq,
           mla_kv_norm_g, w_mla_uk, w_mla_uv, w_out_even, w_in_odd, hgrn_lb, hgrn_norm_g, w_out_odd,
           w_router, b_router, w_e1, b_e1, w_e2, b_e2, final_norm_g):
    bp, tp, d = x_prompt.shape
    bs, ts, _ = x_sample.shape
    past = cache_mla_latent.shape[2]
    depth = w_ada.shape[0]
    nexp = w_router.shape[2]
    n_p, n_s = bp * tp, bs * ts
    n = n_p + n_s
    group = ts
    assert tp % group == 0 and group % 8 == 0 and tp % CHUNK == 0 and ts <= CHUNK
    assert (n * TOP_K) % MOE_BLOCK == 0
    n_even, n_odd = (depth + 1) // 2, depth // 2

    x = jnp.concatenate([x_prompt.reshape(n_p, d), x_sample.reshape(n_s, d)], axis=0)

    mod = _ada(jnp.concatenate([c_prompt, c_sample], axis=0), w_ada, b_ada)
    mod = mod.reshape(depth, bp + bs, 6, d).transpose(0, 2, 1, 3)
    modg = jnp.concatenate([jnp.repeat(mod[:, :, :bp], tp // group, axis=2), mod[:, :, bp:]], axis=2)

    tabs = _rope_tables([(0, tp, bp), (past, ts, bs)])
    perm = _rope_perm()

    lb_soft = jax.nn.softmax(hgrn_lb.astype(F32), axis=0)
    lb_all = jnp.cumsum(lb_soft, axis=0) - lb_soft[0]
    gla_s0 = jnp.pad(state_gla, ((0, 0), (0, 0), (0, 0), (0, LANE - DK_A), (0, 0)))

    lat_p, kpe_p, gla_p, hgrn_p, lat_s, kpe_s, gla_s, hgrn_s = [], [], [], [], [], [], [], []
    rows = (-(-(n * TOP_K) // MOE_BLOCK)) * MOE_BLOCK + nexp * MOE_BLOCK
    nblocks = rows // MOE_BLOCK

    odd_widths = (H_C * DK_C,) * 2 + (H_C * DV_C,) * 2

    def in_weights(layer):
        if layer % 2 == 0:
            return _even_weight(w_in_even[layer // 2]), EVEN_WIDTHS
        return w_in_odd[layer // 2].astype(BF16), odd_widths

    zs = _inproj(x, norm1_g[0], modg, 0, *in_weights(0), group)
    for l in range(depth):
        j = l // 2
        wr = _pad_cols(w_router[l], LANE)
        wr_hi = wr.astype(BF16)
        wr_lo = (wr - wr_hi.astype(F32)).astype(BF16)
        br = _pad_cols(b_router[l].reshape(1, -1), LANE)
        route = (norm2_g[l], wr_hi, wr_lo, br, group, nexp)
        if l % 2 == 0:
            zq, zk, zv, zr, za, zcq, zckv, zkpe = zs
            wa = _pad_heads(jnp.pad(w_gla_a2[j], ((0, LANE - GATE_RANK), (0, 0))), H_A, DK_A)
            wa_hi = wa.astype(BF16)
            wa_lo = (wa - wa_hi.astype(F32)).astype(BF16)
            ab = _pad_heads(b_gla_a[j].reshape(1, -1), H_A, DK_A)
            gn = gla_norm_g[j].reshape(1, DV_A)
            extras = (za, wa_hi, wa_lo, ab)
            oa_p, sp = _recurrence("gla", zq, zk, zv, zr, extras, gn, None, heads=H_A, nseq=bp,
                                   seqlen=tp, chunk=CHUNK, row0=0)
            oa_s, ss = _recurrence("gla", zq, zk, zv, zr, extras, gn, (gla_s0, j), heads=H_A, nseq=bs,
                                   seqlen=ts, chunk=ts, row0=n_p)
            gla_p.append(sp[:, :, :DK_A, :])
            gla_s.append(ss[:, :, :DK_A, :])

            wuk = w_mla_uk[j].transpose(1, 2, 0).astype(BF16)
            wuv = w_mla_uv[j].transpose(1, 0, 2).astype(BF16)
            wuvt = w_mla_uv[j].transpose(1, 2, 0).astype(BF16)
            qcat, kcat, lat, kpe, qt, latt = _mla_pre(
                zcq, zckv, zkpe, tabs, mla_q_norm_g[j].reshape(1, -1), mla_kv_norm_g[j].reshape(1, -1),
                _uq_weight(w_mla_uq[j]), wuk, perm)
            ob_p = _attn_prompt(qt, kcat, latt, wuvt, bp, tp)
            ob_s = _attn_sample(qcat, kcat, cache_mla_latent, cache_mla_krope, j, wuv, n_p)
            lat_p.append(lat[:n_p].reshape(bp, tp, KV_LORA))
            lat_s.append(lat[n_p:].reshape(bs, ts, KV_LORA))
            kpe_p.append(kpe[:n_p].reshape(bp, tp, ROPE))
            kpe_s.append(kpe[n_p:].reshape(bs, ts, ROPE))
            wo = w_out_even[j].astype(BF16)
            x, h2, topi, gate = _outproj_router(x, modg, l, [oa_p, ob_p], [oa_s, ob_s],
                                                [wo[:H_A * DV_A], wo[H_A * DV_A:]], *route)
        else:
            zq, zf, zi, zg = zs
            extras = (lb_all[l].reshape(1, -1),)
            gn = hgrn_norm_g[j].reshape(1, DV_C)
            oc_p, sp = _recurrence("hgrn", zq, zf, zi, zg, extras, gn, None, heads=H_C, nseq=bp,
                                   seqlen=tp, chunk=CHUNK, row0=0)
            oc_s, ss = _recurrence("hgrn", zq, zf, zi, zg, extras, gn, (state_hgrn, j), heads=H_C, nseq=bs,
                                   seqlen=ts, chunk=ts, row0=n_p)
            hgrn_p.append(sp)
            hgrn_s.append(ss)
            x, h2, topi, gate = _outproj_router(x, modg, l, [oc_p], [oc_s], [w_out_odd[j].astype(BF16)],
                                                *route)

        rank, cnt = _rank(topi)
        counts = cnt[0, :nexp].astype(I32)
        padded = (counts + MOE_BLOCK - 1) // MOE_BLOCK * MOE_BLOCK
        pends = jnp.cumsum(padded)
        pstarts = _pad_cols((pends - padded).astype(F32).reshape(1, -1), LANE)
        dest = _dest(topi, rank, pstarts)
        dest1d = dest[:, :TOP_K].reshape(n * TOP_K)
        bexp = jnp.clip(jnp.sum(pends[None, :] <= (jnp.arange(nblocks) * MOE_BLOCK)[:, None], axis=1),
                        0, nexp - 1).astype(I32)
        nused = (pends[-1:] // MOE_BLOCK).astype(I32)
        padinfo = jnp.stack([_pad_cols(pends, LANE), _pad_cols(padded - counts, LANE),
                             _pad_cols(nused, LANE)]).astype(I32)
        xs = _dispatch(dest1d, padinfo, h2, rows, nexp)
        y = _experts(bexp, nused, xs, w_e1, b_e1.reshape(depth, nexp, 1, -1), w_e2,
                     b_e2.reshape(depth, nexp, 1, -1), l)
        if l + 1 < depth:
            x, *zs = _combine_inproj(dest1d, gate, x, modg, l, y, norm1_g[l + 1], *in_weights(l + 1), group)
        else:
            x = _combine(dest1d, gate, x, modg, l, y, group)

    y_p =_final_norm(x, final_norm_g, 0, n_p)
    y_s = _final_norm(x, final_norm_g, n_p, n_s)
    return (y_p.reshape(bp, tp, d), y_s.reshape(bs, ts, d),
            jnp.stack(lat_p), jnp.stack(kpe_p), jnp.stack(gla_p), jnp.stack(hgrn_p),
            jnp.stack(lat_s), jnp.stack(kpe_s), jnp.stack(gla_s), jnp.stack(hgrn_s))
```

```python
import functools
import math

import jax
import jax.numpy as jnp
from jax import lax
from jax.experimental import pallas as pl
from jax.experimental.pallas import tpu as pltpu

F32 = jnp.float32
BF16 = jnp.bfloat16
I32 = jnp.int32

EPS = 1e-6
CHUNK = 64
LANE = 128

H_A, DK_A, DV_A, GATE_RANK, GATE_TAU = 4, 64, 128, 16, 16.0
H_B, Q_LORA, KV_LORA, NOPE, ROPE, V_B = 4, 384, 256, 128, 64, 128
ROPE_BASE = 10000.0
MLA_SCALE = (NOPE + ROPE) ** -0.5
KCAT = KV_LORA + LANE
H_C, DK_C, DV_C = 8, 128, 128
TOP_K = 4
SWIGLU_LIMIT = 7.0
SWIGLU_ALPHA = 1.702
MOE_BLOCK = 512
ISSUE_UNROLL = 8
EXP_CLAMP = 80.0

VMEM_LIMIT = 56 * 1024 * 1024


def _cparams(sem, vmem=None):
    return pltpu.CompilerParams(dimension_semantics=sem, vmem_limit_bytes=vmem)


def _pick_tile(n, pref):
    t = pref
    while n % t:
        t //= 2
    return t


def _rms(x, g):
    return x * lax.rsqrt(jnp.mean(x * x, axis=-1, keepdims=True) + EPS) * g


def _group_affine(x, scale, shift, group):
    tm = x.shape[0]
    pieces = []
    for gi in range(tm // group):
        xg = x[gi * group:(gi + 1) * group, :]
        if scale is not None:
            xg = xg * scale[gi:gi + 1, :]
        if shift is not None:
            xg = xg + shift[gi:gi + 1, :]
        pieces.append(xg)
    return jnp.concatenate(pieces, axis=0)


def _cumsum_rows(x):
    c, w = x.shape
    row = lax.broadcasted_iota(I32, (c, 1), 0)
    s = 1
    while s < c:
        if s % 8 == 0:
            shifted = jnp.concatenate([jnp.zeros((s, w), x.dtype), x[:c - s, :]], axis=0)
        else:
            shifted = jnp.where(row >= s, pltpu.roll(x, s, 0), 0.0)
        x = x + shifted
        s *= 2
    return x


def _dot(a, b):
    return jnp.dot(a, b, preferred_element_type=F32)


def _dot_nt(a, b):
    return lax.dot_general(a, b, (((1,), (1,)), ((), ())), preferred_element_type=F32)


def _dot_tn(a, b):
    return lax.dot_general(a, b, (((0,), (0,)), ((), ())), preferred_element_type=F32)


def _ada_kernel(c_ref, w_ref, b_ref, o_ref):
    c = c_ref[...]
    a = (c * jax.nn.sigmoid(c)).astype(BF16)
    o_ref[...] = _dot(a, w_ref[...].astype(BF16)) + b_ref[...]


def _ada(c_all, w_ada, b_ada):
    depth, d, n6 = w_ada.shape
    s = c_all.shape[0]
    tn = _pick_tile(n6, 1536)
    return pl.pallas_call(
        _ada_kernel,
        grid=(depth, n6 // tn),
        in_specs=[
            pl.BlockSpec((s, d), lambda l, j: (0, 0)),
            pl.BlockSpec((None, d, tn), lambda l, j: (l, 0, j)),
            pl.BlockSpec((None, 1, tn), lambda l, j: (l, 0, j)),
        ],
        out_specs=pl.BlockSpec((None, s, tn), lambda l, j: (l, 0, j)),
        out_shape=jax.ShapeDtypeStruct((depth, s, n6), F32),
        compiler_params=_cparams(("arbitrary", "arbitrary")),
        name="ada",
    )(c_all, w_ada, b_ada.reshape(depth, 1, n6))


def _inproj_kernel(x_ref, g_ref, sc_ref, sh_ref, w_ref, *out_refs, splits, group):
    h = _rms(x_ref[...], g_ref[...])
    h = _group_affine(h, 1.0 + sc_ref[...], sh_ref[...], group).astype(BF16)
    for o_ref, (c0, c1) in zip(out_refs, splits):
        o_ref[...] = _dot(h, w_ref[:, c0:c1]).astype(o_ref.dtype)


def _inproj(x, gain, modg, l, w, widths, group):
    n, d = x.shape
    tm = _pick_tile(n, 512)
    tg = tm // group
    splits, c = [], 0
    for wd in widths:
        splits.append((c, c + wd))
        c += wd
    mod_spec = lambda comp: pl.BlockSpec((None, None, tg, d), lambda i: (l, comp, i, 0))
    return pl.pallas_call(
        functools.partial(_inproj_kernel, splits=tuple(splits), group=group),
        grid=(n // tm,),
        in_specs=[
            pl.BlockSpec((tm, d), lambda i: (i, 0)),
            pl.BlockSpec((1, d), lambda i: (0, 0)),
            mod_spec(1),
            mod_spec(0),
            pl.BlockSpec((d, c), lambda i: (0, 0)),
        ],
        out_specs=[pl.BlockSpec((tm, wd), lambda i: (i, 0)) for wd in widths],
        out_shape=[jax.ShapeDtypeStruct((n, wd), F32) for wd in widths],
        compiler_params=_cparams(("arbitrary",), VMEM_LIMIT),
        name="inproj",
    )(x, gain.reshape(1, d), modg, modg, w)


REC_SEQS = 8


def _rec_kernel(*refs, mode, heads, chunk, nchunks, zero_init, nseqs):
    refs = list(refs)
    ntok = 5 if mode == "gla" else 4
    tok_refs = [refs[s * ntok:(s + 1) * ntok] for s in range(nseqs)]
    refs = refs[nseqs * ntok:]
    if mode == "gla":
        wah_ref, wal_ref, ab_ref, gn_ref = refs[:4]
        refs = refs[4:]
    else:
        lb_ref, gn_ref = refs[:2]
        refs = refs[2:]
    s0_ref = None
    if not zero_init:
        s0_ref = refs.pop(0)
    o_ref, sout_ref = refs[:2]
    st_scr = refs[2:]
    ci = pl.program_id(1)

    @pl.when(ci == 0)
    def _():
        for s in range(nseqs):
            for h in range(heads):
                if zero_init:
                    st_scr[s * heads + h][...] = jnp.zeros((LANE, LANE), F32)
                else:
                    st_scr[s * heads + h][...] = s0_ref[s, h].T

    row = lax.broadcasted_iota(I32, (chunk, chunk), 0)
    col = lax.broadcasted_iota(I32, (chunk, chunk), 1)
    causal = row >= col
    mid = chunk // 2 - 1

    for s in range(nseqs):
        if mode == "gla":
            q_ref, k_ref, v_ref, r_ref, a_ref = tok_refs[s]
            a = a_ref[...]
            a_hi = a.astype(BF16)
            a_lo = (a - a_hi.astype(F32)).astype(BF16)
            alog_all = (_dot(a_hi, wah_ref[...]) + _dot(a_lo, wah_ref[...]) + _dot(a_hi, wal_ref[...])
                        + ab_ref[...])
            g_all = jax.nn.log_sigmoid(alog_all) * (1.0 / GATE_TAU)
            fg_all = None
        else:
            q_ref, k_ref, v_ref, r_ref = tok_refs[s]
            lb = lb_ref[...]
            fg_all = lb + (1.0 - lb) * jax.nn.sigmoid(k_ref[...])
            g_all = jnp.log(fg_all)
        b_all = _cumsum_rows(g_all)

        for h in range(heads):
            sl = slice(h * LANE, (h + 1) * LANE)
            if mode == "gla":
                q = q_ref[:, sl] * (DK_A ** -0.5)
                k = k_ref[:, sl]
            else:
                qr = q_ref[:, sl]
                q = qr * jax.nn.sigmoid(qr)
                k = 1.0 - fg_all[:, sl]
            b = b_all[:, sl]
            b_last = b[chunk - 1:chunk, :]
            b_ref_row = b[mid:mid + 1, :]
            qt = (q * jnp.exp(jnp.minimum(b - b_ref_row, EXP_CLAMP))).astype(BF16)
            kt = (k * jnp.exp(jnp.minimum(b_ref_row - b, EXP_CLAMP))).astype(BF16)
            qs = (q * jnp.exp(b)).astype(BF16)
            ks = (k * jnp.exp(b_last - b)).astype(BF16)
            v = v_ref[:, sl].astype(BF16)
            att = jnp.where(causal, _dot_nt(qt, kt), 0.0).astype(BF16)
            st_ref = st_scr[s * heads + h]
            st = st_ref[...]
            o = _dot(att, v) + _dot_nt(qs, st.astype(BF16))
            st_ref[...] = st * jnp.exp(b_last) + _dot_tn(v, ks)
            rg = r_ref[:, sl]
            o = _rms(o, gn_ref[...]) * (rg * jax.nn.sigmoid(rg))
            o_ref[s, :, sl] = o.astype(o_ref.dtype)

    @pl.when(ci == nchunks - 1)
    def _():
        for s in range(nseqs):
            for h in range(heads):
                sout_ref[s, h] = st_scr[s * heads + h][...].T


def _recurrence(mode, q, k, v, r, extras, gnorm, s0, *, heads, nseq, seqlen, chunk, row0):
    cols = q.shape[1]
    nchunks = seqlen // chunk
    blk0 = row0 // chunk
    ns = REC_SEQS if nseq % REC_SEQS == 0 else 1
    full = lambda a: pl.BlockSpec(a.shape, lambda b, c: (0,) * a.ndim)
    in_specs, args = [], []
    for s in range(ns):
        tok = lambda b, c, s=s: (blk0 + (b * ns + s) * nchunks + c, 0)
        toks = [q, k, v, r] + ([extras[0]] if mode == "gla" else [])
        in_specs += [pl.BlockSpec((chunk, a.shape[1]), tok) for a in toks]
        args += toks
    consts = list(extras[1:]) if mode == "gla" else list(extras)
    consts.append(gnorm)
    in_specs += [full(a) for a in consts]
    args += consts
    zero_init = s0 is None
    if not zero_init:
        s0, s0_layer = s0
        in_specs.append(pl.BlockSpec((None, ns, heads, LANE, LANE), lambda b, c: (s0_layer, b, 0, 0, 0)))
        args.append(s0)
    o, st = pl.pallas_call(
        functools.partial(_rec_kernel, mode=mode, heads=heads, chunk=chunk, nchunks=nchunks,
                          zero_init=zero_init, nseqs=ns),
        grid=(nseq // ns, nchunks),
        in_specs=in_specs,
        out_specs=[pl.BlockSpec((ns, chunk, cols), lambda b, c: (b, c, 0)),
                   pl.BlockSpec((ns, heads, LANE, LANE), lambda b, c: (b, 0, 0, 0))],
        out_shape=[jax.ShapeDtypeStruct((nseq, seqlen, cols), BF16),
                   jax.ShapeDtypeStruct((nseq, heads, LANE, LANE), F32)],
        scratch_shapes=[pltpu.VMEM((LANE, LANE), F32) for _ in range(ns * heads)],
        compiler_params=_cparams(("arbitrary", "arbitrary")),
        name="recurrence_" + mode,
    )(*args)
    return o.reshape(nseq * seqlen, cols), st


def _mla_pre_kernel(cq_ref, ckv_ref, kpe_ref, cosq_ref, sinq_ref, cosk_ref, sink_ref, qg_ref, kvg_ref,
                    wuq_ref, wuk_ref, perm_ref, qcat_ref, kcat_ref, lat_ref, kpeo_ref, qt_ref, latt_ref, *, tq):
    tm = cq_ref.shape[0]
    cqn = _rms(cq_ref[...], qg_ref[...]).astype(BF16)
    qf = _dot(cqn, wuq_ref[...])
    off = H_B * NOPE
    x1 = qf[:, off:off + LANE]
    x2 = qf[:, off + LANE:off + 2 * LANE]
    cq, sq = cosq_ref[...], sinq_ref[...]
    o1 = (x1 * cq - x2 * sq) * MLA_SCALE
    o2 = (x2 * cq + x1 * sq) * MLA_SCALE
    pe = _dot(o1.astype(BF16), perm_ref[0:LANE, :]) + _dot(o2.astype(BF16), perm_ref[LANE:2 * LANE, :])
    for h in range(H_B):
        ql = _dot(qf[:, h * NOPE:(h + 1) * NOPE].astype(BF16), wuk_ref[h]) * MLA_SCALE
        peh = pe[:, h * LANE:(h + 1) * LANE]
        qcat_ref[h, :, 0:KV_LORA] = ql.astype(BF16)
        qcat_ref[h, :, KV_LORA:KCAT] = peh.astype(BF16)
        for jb in range(tm // tq):
            rows = slice(jb * tq, (jb + 1) * tq)
            cols = slice(h * tq, (h + 1) * tq)
            qt_ref[jb, 0:KV_LORA, cols] = ql[rows, :].T.astype(BF16)
            qt_ref[jb, KV_LORA:KCAT, cols] = peh[rows, :].T.astype(BF16)
    latn = _rms(ckv_ref[...], kvg_ref[...])
    lat_ref[...] = latn
    latt_ref[...] = latn.T.astype(BF16)
    x = kpe_ref[...]
    half = ROPE // 2
    lane = lax.broadcasted_iota(I32, x.shape, 1)
    rot = jnp.where(lane < half, -pltpu.roll(x, LANE - half, 1), pltpu.roll(x, half, 1))
    kro = x * cosk_ref[...] + rot * sink_ref[...]
    kpeo_ref[...] = kro[:, 0:ROPE]
    kcat_ref[:, 0:KV_LORA] = latn.astype(BF16)
    kcat_ref[:, KV_LORA:KCAT] = kro.astype(BF16)


ATTN_TQ = 8 * CHUNK


def _mla_pre(zcq, zckv, zkpe, tabs, qg, kvg, wuq, wuk, perm):
    n = zcq.shape[0]
    tm = _pick_tile(n, 512)
    tq = ATTN_TQ
    assert tm % tq == 0
    tokspec = lambda wd: pl.BlockSpec((tm, wd), lambda i: (i, 0))
    full = lambda a: pl.BlockSpec(a.shape, lambda i: (0,) * a.ndim)
    return pl.pallas_call(
        functools.partial(_mla_pre_kernel, tq=tq),
        grid=(n // tm,),
        in_specs=[tokspec(Q_LORA), tokspec(KV_LORA), tokspec(LANE)] + [tokspec(LANE)] * 4
                 + [full(qg), full(kvg), full(wuq), full(wuk), full(perm)],
        out_specs=[pl.BlockSpec((H_B, tm, KCAT), lambda i: (0, i, 0)), tokspec(KCAT),
                   tokspec(KV_LORA), tokspec(ROPE),
                   pl.BlockSpec((tm // tq, KCAT, H_B * tq), lambda i: (i, 0, 0)),
                   pl.BlockSpec((KV_LORA, tm), lambda i: (0, i))],
        out_shape=[jax.ShapeDtypeStruct((H_B, n, KCAT), BF16), jax.ShapeDtypeStruct((n, KCAT), BF16),
                   jax.ShapeDtypeStruct((n, KV_LORA), F32), jax.ShapeDtypeStruct((n, ROPE), F32),
                   jax.ShapeDtypeStruct((n // tq, KCAT, H_B * tq), BF16),
                   jax.ShapeDtypeStruct((KV_LORA, n), BF16)],
        compiler_params=_cparams(("arbitrary",)),
        name="mla_pre",
    )(zcq, zckv, zkpe, *tabs, qg, kvg, wuq, wuk, perm)


def _softmax_update(s, vals, m_scr, l_scr, acc_scr):
    m_prev = m_scr[...]
    m_new = jnp.maximum(m_prev, jnp.max(s, axis=1, keepdims=True))
    alpha = jnp.exp(m_prev - m_new)
    p = jnp.exp(s - m_new)
    l_scr[...] = alpha * l_scr[...] + jnp.sum(p, axis=1, keepdims=True)
    acc_scr[...] = alpha * acc_scr[...] + _dot(p.astype(BF16), vals)
    m_scr[...] = m_new


def _softmax_init(m_scr, l_scr, acc_scr):
    m_scr[...] = jnp.full(m_scr.shape, -jnp.inf, F32)
    l_scr[...] = jnp.zeros(l_scr.shape, F32)
    acc_scr[...] = jnp.zeros(acc_scr.shape, F32)


def _attn_finish(o_ref, wuv_ref, l_scr, acc_scr, tq):
    inv = 1.0 / l_scr[...]
    for h in range(H_B):
        rows = slice(h * tq, (h + 1) * tq)
        oh = (acc_scr[rows, :] * inv[rows, :]).astype(BF16)
        o_ref[:, h * V_B:(h + 1) * V_B] = _dot(oh, wuv_ref[h]).astype(o_ref.dtype)


def _attn_prompt_kernel(qt_ref, k_ref, latt_ref, wuvt_ref, o_ref, m_scr, l_scr, acc_scr, p_scr, *, tq, tk):
    qi = pl.program_id(1)
    cols = H_B * tq
    qt = qt_ref[...]
    m_scr[...] = jnp.full(m_scr.shape, -jnp.inf, F32)
    l_scr[...] = jnp.zeros(l_scr.shape, F32)
    acc_scr[...] = jnp.zeros(acc_scr.shape, F32)

    def block(start, masked):
        s = _dot(k_ref[pl.ds(start, tk), :], qt)
        if masked:
            tok = qi * tq + (lax.broadcasted_iota(I32, (1, cols), 1) & (tq - 1))
            limit = (tok // CHUNK + 1) * CHUNK
            key = start + lax.broadcasted_iota(I32, (tk, 1), 0)
            s = jnp.where(key < limit, s, -jnp.inf)
        m_prev = m_scr[...]
        m_new = jnp.maximum(m_prev, jnp.max(s, axis=0, keepdims=True))
        alpha = jnp.exp(m_prev - m_new)
        p = jnp.exp(s - m_new)
        l_scr[...] = alpha * l_scr[...] + jnp.sum(p, axis=0, keepdims=True)
        m_scr[...] = m_new
        p_scr[...] = p.astype(BF16)
        acc_scr[...] = alpha * acc_scr[...] + _dot(latt_ref[:, pl.ds(start, tk)], p_scr[...])

    nfull = (qi * tq) // tk

    def full(j, carry):
        block(pl.multiple_of(j * tk, tk), False)
        return carry

    lax.fori_loop(0, nfull, full, 0)
    block(pl.multiple_of(nfull * tk, tk), True)

    ot =(acc_scr[...] * (1.0 / l_scr[...])).astype(BF16)
    for h in range(H_B):
        oh = _dot(wuvt_ref[h], ot[:, h * tq:(h + 1) * tq])
        o_ref[:, h * V_B:(h + 1) * V_B] = oh.T.astype(o_ref.dtype)


def _attn_prompt(qt, kcat, latt, wuvt, nseq, seqlen):
    n = nseq * seqlen
    tq = ATTN_TQ
    tk = 512
    assert seqlen % tk == 0 and tk % tq == 0
    nq = seqlen // tq
    return pl.pallas_call(
        functools.partial(_attn_prompt_kernel, tq=tq, tk=tk),
        grid=(nseq, nq),
        in_specs=[
            pl.BlockSpec((None, KCAT, H_B * tq), lambda b, qi: (b * nq + qi, 0, 0)),
            pl.BlockSpec((seqlen, KCAT), lambda b, qi: (b, 0)),
            pl.BlockSpec((KV_LORA, seqlen), lambda b, qi: (0, b)),
            pl.BlockSpec(wuvt.shape, lambda b, qi: (0, 0, 0)),
        ],
        out_specs=pl.BlockSpec((tq, H_B * V_B), lambda b, qi: (b * nq + qi, 0)),
        out_shape=jax.ShapeDtypeStruct((n, H_B * V_B), BF16),
        scratch_shapes=[pltpu.VMEM((1, H_B * tq), F32), pltpu.VMEM((1, H_B * tq), F32),
                        pltpu.VMEM((KV_LORA, H_B * tq), F32), pltpu.VMEM((tk, H_B * tq), BF16)],
        compiler_params=_cparams(("arbitrary", "arbitrary")),
        name="attn_prompt",
    )(qt, kcat, latt, wuvt)


def _attn_sample_kernel(q_ref, plat_ref, pkpe_ref, knew_ref, wuv_ref, o_ref,
                        m_scr, l_scr, acc_scr, *, tq, nkp):
    ki = pl.program_id(1)

    @pl.when(ki == 0)
    def _():
        _softmax_init(m_scr, l_scr, acc_scr)

    q = q_ref[...].reshape(H_B * tq, KCAT)
    lat = plat_ref[...].astype(BF16)
    kpe = pkpe_ref[...].astype(BF16)
    s = _dot_nt(q[:, 0:KV_LORA], lat) + _dot_nt(q[:, KV_LORA:KV_LORA + ROPE], kpe)
    _softmax_update(s, lat, m_scr, l_scr, acc_scr)

    @pl.when(ki == nkp - 1)
    def _():
        kn = knew_ref[...]
        _softmax_update(_dot_nt(q, kn), kn[:, 0:KV_LORA], m_scr, l_scr, acc_scr)
        _attn_finish(o_ref, wuv_ref, l_scr, acc_scr, tq)


def _attn_sample(qcat, kcat, past_lat, past_kpe, layer, wuv, row0):
    _, nseq, past, _ = past_lat.shape
    tq = (kcat.shape[0] - row0) // nseq
    tkp = _pick_tile(past, 2048)
    nkp = past // tkp
    blk0 = row0 // tq
    return pl.pallas_call(
        functools.partial(_attn_sample_kernel, tq=tq, nkp=nkp),
        grid=(nseq, nkp),
        in_specs=[
            pl.BlockSpec((H_B, tq, KCAT), lambda b, ki: (0, blk0 + b, 0)),
            pl.BlockSpec((None, None, tkp, KV_LORA), lambda b, ki: (layer, b, ki, 0)),
            pl.BlockSpec((None, None, tkp, ROPE), lambda b, ki: (layer, b, ki, 0)),
            pl.BlockSpec((tq, KCAT), lambda b, ki: (blk0 + b, 0)),
            pl.BlockSpec(wuv.shape, lambda b, ki: (0, 0, 0)),
        ],
        out_specs=pl.BlockSpec((tq, H_B * V_B), lambda b, ki: (b, 0)),
        out_shape=jax.ShapeDtypeStruct((nseq * tq, H_B * V_B), BF16),
        scratch_shapes=[pltpu.VMEM((H_B * tq, 1), F32), pltpu.VMEM((H_B * tq, 1), F32),
                        pltpu.VMEM((H_B * tq, KV_LORA), F32)],
        compiler_params=_cparams(("arbitrary", "arbitrary")),
        name="attn_sample",
    )(qcat, past_lat, past_kpe, kcat, wuv)


def _outproj_router_kernel(*refs, nlhs, group, ptiles, nexp):
    x_ref, g_ref = refs[0], refs[1]
    lhs_p = refs[2:2 + nlhs]
    lhs_s = refs[2 + nlhs:2 + 2 * nlhs]
    ws = refs[2 + 2 * nlhs:2 + 3 * nlhs]
    n2_ref, sc_ref, sh_ref, wh_ref, wl_ref, br_ref, o_ref, h_ref, ti_ref, gt_ref = refs[2 + 3 * nlhs:]
    is_prompt = pl.program_id(0) < ptiles
    acc = None
    for ap, asm, w in zip(lhs_p, lhs_s, ws):
        a = jnp.where(is_prompt, ap[...], asm[...])
        t = _dot(a, w[...])
        acc = t if acc is None else acc + t
    x = x_ref[...] + _group_affine(acc, g_ref[...], None, group)
    o_ref[...] = x
    _route(x, n2_ref, sc_ref, sh_ref, wh_ref, wl_ref, br_ref, h_ref, ti_ref, gt_ref, group, nexp)


def _outproj_router(x, modg, l, lhs_p, lhs_s, ws, gain2, wr_hi, wr_lo, br, group, nexp):
    n, d = x.shape
    n_p, n_s = lhs_p[0].shape[0], lhs_s[0].shape[0]
    tm = _pick_tile(math.gcd(n_p, n_s), 512)
    tg = tm // group
    ptiles = n_p // tm
    mod_spec = lambda comp: pl.BlockSpec((None, None, tg, d), lambda i: (l, comp, i, 0))
    full = lambda a: pl.BlockSpec(a.shape, lambda i: (0,) * a.ndim)
    tok = lambda wd: pl.BlockSpec((tm, wd), lambda i: (i, 0))
    return pl.pallas_call(
        functools.partial(_outproj_router_kernel, nlhs=len(ws), group=group, ptiles=ptiles, nexp=nexp),
        grid=(n // tm,),
        in_specs=[tok(d), mod_spec(2)]
                 + [pl.BlockSpec((tm, a.shape[1]), lambda i: (jnp.minimum(i, ptiles - 1), 0)) for a in lhs_p]
                 + [pl.BlockSpec((tm, a.shape[1]), lambda i: (jnp.maximum(i - ptiles, 0), 0)) for a in lhs_s]
                 + [full(w) for w in ws]
                 + [pl.BlockSpec((1, d), lambda i: (0, 0)), mod_spec(4), mod_spec(3),
                    full(wr_hi), full(wr_lo), full(br)],
        out_specs=[tok(d), tok(d), tok(LANE), tok(LANE)],
        out_shape=[jax.ShapeDtypeStruct((n, d), F32), jax.ShapeDtypeStruct((n, d), F32),
                   jax.ShapeDtypeStruct((n, LANE), I32), jax.ShapeDtypeStruct((n, LANE), F32)],
        compiler_params=_cparams(("arbitrary",), VMEM_LIMIT),
        name="outproj_router",
    )(x, modg, *lhs_p, *lhs_s, *ws, gain2.reshape(1, d), modg, modg, wr_hi, wr_lo, br)


def _route(x, g_ref, sc_ref, sh_ref, wh_ref, wl_ref, br_ref, h_ref, ti_ref, gt_ref, group, nexp):
    h = _rms(x, g_ref[...])
    h = _group_affine(h, 1.0 + sc_ref[...], sh_ref[...], group)
    h_ref[...] = h
    h_hi = h.astype(BF16)
    h_lo = (h - h_hi.astype(F32)).astype(BF16)
    logits = _dot(h_hi, wh_ref[...]) + _dot(h_lo, wh_ref[...]) + _dot(h_hi, wl_ref[...]) + br_ref[...]
    lane = lax.broadcasted_iota(I32, logits.shape, 1)
    lane_f = lane.astype(F32)
    cur = jnp.where(lane < nexp, logits, -jnp.inf)
    tops, idxs = [], []
    for _ in range(TOP_K):
        m = jnp.max(cur, axis=1, keepdims=True)
        i = jnp.min(jnp.where(cur == m, lane_f, float(LANE)), axis=1, keepdims=True)
        cur = jnp.where(lane_f == i, -jnp.inf, cur)
        tops.append(m)
        idxs.append(i.astype(I32))
    es = [jnp.exp(t - tops[0]) for t in tops]
    inv = 1.0 / (es[0] + es[1] + es[2] + es[3])
    ti = jnp.zeros(logits.shape, I32)
    gt = jnp.zeros(logits.shape, F32)
    for k in range(TOP_K):
        ti = jnp.where(lane == k, idxs[k], ti)
        gt = jnp.where(lane == k, es[k] * inv, gt)
    ti_ref[...] = ti
    gt_ref[...] = gt


def _rank_kernel(ti_ref, rank_ref, cnt_ref, carry_scr, *, tm):
    i = pl.program_id(0)

    @pl.when(i == 0)
    def _():
        carry_scr[...] = jnp.zeros(carry_scr.shape, F32)

    ti = ti_ref[...]
    lane = lax.broadcasted_iota(I32, ti.shape, 1)
    sel = [lane == ti[:, k:k + 1] for k in range(TOP_K)]
    hot = jnp.zeros(ti.shape, F32)
    for s in sel:
        hot = hot + jnp.where(s, 1.0, 0.0)
    row = lax.broadcasted_iota(I32, (tm, tm), 0)
    col = lax.broadcasted_iota(I32, (tm, tm), 1)
    strict = jnp.where(row > col, 1.0, 0.0).astype(BF16)
    before = _dot(strict, hot.astype(BF16)) + carry_scr[0:1, :]
    rank = jnp.zeros(ti.shape, F32)
    for k in range(TOP_K):
        rk = jnp.sum(jnp.where(sel[k], before, 0.0), axis=1, keepdims=True)
        rank = jnp.where(lane == k, rk, rank)
    rank_ref[...] = rank.astype(I32)
    carry_scr[...] = carry_scr[...] + jnp.sum(hot, axis=0, keepdims=True)
    cnt_ref[...] = carry_scr[...]


def _rank(topi):
    n = topi.shape[0]
    tm = _pick_tile(n, 512)
    return pl.pallas_call(
        functools.partial(_rank_kernel, tm=tm),
        grid=(n // tm,),
        in_specs=[pl.BlockSpec((tm, LANE), lambda i: (i, 0))],
        out_specs=[pl.BlockSpec((tm, LANE), lambda i: (i, 0)), pl.BlockSpec((8, LANE), lambda i: (0, 0))],
        out_shape=[jax.ShapeDtypeStruct((n, LANE), I32), jax.ShapeDtypeStruct((8, LANE), F32)],
        scratch_shapes=[pltpu.VMEM((8, LANE), F32)],
        compiler_params=_cparams(("arbitrary",)),
        name="rank",
    )(topi)


def _dest_kernel(ti_ref, rank_ref, ps_ref, d_ref):
    ti = ti_ref[...]
    lane = lax.broadcasted_iota(I32, ti.shape, 1)
    ps = ps_ref[...]
    dest = rank_ref[...]
    for k in range(TOP_K):
        base = jnp.sum(jnp.where(lane == ti[:, k:k + 1], ps, 0.0), axis=1, keepdims=True).astype(I32)
        dest = dest + jnp.where(lane == k, base, 0)
    d_ref[...] = dest


def _dest(topi, rank, pstarts_row):
    n = topi.shape[0]
    tm = _pick_tile(n, 1024)
    tok = pl.BlockSpec((tm, LANE), lambda i: (i, 0))
    return pl.pallas_call(
        _dest_kernel,
        grid=(n // tm,),
        in_specs=[tok, tok, pl.BlockSpec((1, LANE), lambda i: (0, 0))],
        out_specs=tok,
        out_shape=jax.ShapeDtypeStruct((n, LANE), I32),
        compiler_params=_cparams(("arbitrary",)),
        name="dest",
    )(topi, rank, pstarts_row)


def _dispatch_kernel(dest_ref, pad_ref, h_ref, xs_ref, hbuf, zbuf, sem, zsem, *, tm, nexp, nblocks):
    i = pl.program_id(0)
    slot = lax.rem(i, 2)
    hbuf[slot] = h_ref[...]

    def issue(it, carry):
        for u in range(ISSUE_UNROLL):
            r = it * ISSUE_UNROLL + u
            for k in range(TOP_K):
                d = dest_ref[r * TOP_K + k]
                pltpu.make_async_copy(hbuf.at[slot, pl.ds(r, 1)], xs_ref.at[pl.ds(d, 1)], sem.at[slot]).start()
        return carry

    lax.fori_loop(0, tm // ISSUE_UNROLL, issue, 0)

    def retire(s):
        for _ in range(TOP_K):
            pltpu.make_async_copy(hbuf.at[s], xs_ref.at[pl.ds(0, tm)], sem.at[s]).wait()

    @pl.when(i >= 1)
    def _():
        retire(1 - slot)

    @pl.when(i == pl.num_programs(0) - 1)
    def _():
        retire(slot)
        zbuf[...] = jnp.zeros(zbuf.shape, F32)

        def fill_expert(e, carry):
            end = pad_ref[0, e]
            npad = pad_ref[1, e]
            p = MOE_BLOCK // 2
            while p >= 1:
                bit = npad & p
                end = end - bit

                @pl.when(bit != 0)
                def _(end=end, p=p):
                    if p >= 8:
                        start = pl.multiple_of(end, 8)
                        pltpu.make_async_copy(zbuf.at[pl.ds(0, p)], xs_ref.at[pl.ds(start, p)], zsem).start()
                    else:
                        for q in range(p):
                            pltpu.make_async_copy(zbuf.at[pl.ds(0, 1)], xs_ref.at[pl.ds(end + q, 1)],
                                                  zsem).start()

                p //= 2
            return carry

        lax.fori_loop(0, nexp, fill_expert, 0)
        nused = pad_ref[2, 0]

        def fill_tail(b, carry):
            @pl.when(b >= nused)
            def _():
                start = pl.multiple_of(b * MOE_BLOCK, MOE_BLOCK)
                pltpu.make_async_copy(zbuf, xs_ref.at[pl.ds(start, MOE_BLOCK)], zsem).start()

            return carry

        lax.fori_loop(0, nblocks, fill_tail, 0)
        for _ in range(nexp):
            pltpu.make_async_copy(zbuf, xs_ref.at[pl.ds(0, MOE_BLOCK)], zsem).wait()


def _dispatch(dest1d, padinfo, h, rows, nexp):
    n, d = h.shape
    tm = _pick_tile(n, 512)
    return pl.pallas_call(
        functools.partial(_dispatch_kernel, tm=tm, nexp=nexp, nblocks=rows // MOE_BLOCK),
        grid=(n // tm,),
        in_specs=[pl.BlockSpec((tm * TOP_K,), lambda i: (i,), memory_space=pltpu.SMEM),
                  pl.BlockSpec(memory_space=pltpu.SMEM),
                  pl.BlockSpec((tm, d), lambda i: (i, 0))],
        out_specs=pl.BlockSpec(memory_space=pl.ANY),
        out_shape=jax.ShapeDtypeStruct((rows, d), F32),
        scratch_shapes=[pltpu.VMEM((2, tm, d), F32), pltpu.VMEM((MOE_BLOCK, d), F32),
                        pltpu.SemaphoreType.DMA((2,)), pltpu.SemaphoreType.DMA(())],
        compiler_params=pltpu.CompilerParams(dimension_semantics=("arbitrary",), has_side_effects=True),
        name="dispatch",
    )(dest1d, padinfo, h)


def _experts_kernel(bexp_ref, nused_ref, xs_ref, w1_ref, b1_ref, w2_ref, b2_ref, y_ref,
                    w1b, w2b, act_scr, prev_scr, *, dff, dm):
    i = pl.program_id(0)
    nu = nused_ref[0]
    e = bexp_ref[jnp.minimum(i, nu - 1)]

    @pl.when(i == 0)
    def _():
        prev_scr[0] = -1

    @pl.when(i >= nu)
    def _():
        y_ref[...] = jnp.zeros(y_ref.shape, F32)

    @pl.when(i < nu)
    def _():
        @pl.when(e != prev_scr[0])
        def _():
            rows = 128

            def cast1(c, carry):
                r0 = pl.multiple_of(c * rows, rows)
                w1b[pl.ds(r0, rows), :] = w1_ref[pl.ds(r0, rows), :].astype(BF16)
                return carry

            def cast2(c, carry):
                r0 = pl.multiple_of(c * rows, rows)
                w2b[pl.ds(r0, rows), :] = w2_ref[pl.ds(r0, rows), :].astype(BF16)
                return carry

            lax.fori_loop(0, dm // rows, cast1, 0)
            lax.fori_loop(0, dff // rows, cast2, 0)
            prev_scr[0] = e

        x = xs_ref[...].astype(BF16)
        cw = 256
        for c in range(dff // cw):
            gt = _dot(x, w1b[:, c * cw:(c + 1) * cw]) + b1_ref[:, c * cw:(c + 1) * cw]
            up = _dot(x, w1b[:, dff + c * cw:dff + (c + 1) * cw]) + b1_ref[:, dff + c * cw:dff + (c + 1) * cw]
            gt = jnp.minimum(gt, SWIGLU_LIMIT)
            up = jnp.clip(up, -SWIGLU_LIMIT, SWIGLU_LIMIT)
            act = gt * jax.nn.sigmoid(gt * SWIGLU_ALPHA) * (up + 1.0)
            act_scr[:, c * cw:(c + 1) * cw] = act.astype(BF16)
        a = act_scr[...]
        for c in range(dm // cw):
            y_ref[:, c * cw:(c + 1) * cw] = _dot(a, w2b[:, c * cw:(c + 1) * cw]) + b2_ref[:, c * cw:(c + 1) * cw]


def _experts(bexp, nused, xs, w1, b1, w2, b2, l):
    rows, dm = xs.shape
    dff = w2.shape[2]
    nb = rows // MOE_BLOCK
    blk = lambda i, be, nu: (jnp.minimum(i, nu[0] - 1), 0)
    wsel = lambda i, be, nu: (l, be[jnp.minimum(i, nu[0] - 1)], 0, 0)
    grid_spec = pltpu.PrefetchScalarGridSpec(
        num_scalar_prefetch=2,
        grid=(nb,),
        in_specs=[
            pl.BlockSpec((MOE_BLOCK, dm), blk),
            pl.BlockSpec((None, None, dm, 2 * dff), wsel),
            pl.BlockSpec((None, None, 1, 2 * dff), wsel),
            pl.BlockSpec((None, None, dff, dm), wsel),
            pl.BlockSpec((None, None, 1, dm), wsel),
        ],
        out_specs=pl.BlockSpec((MOE_BLOCK, dm), lambda i, be, nu: (i, 0)),
        scratch_shapes=[pltpu.VMEM((dm, 2 * dff), BF16), pltpu.VMEM((dff, dm), BF16),
                        pltpu.VMEM((MOE_BLOCK, dff), BF16), pltpu.SMEM((1,), I32)],
    )
    return pl.pallas_call(
        functools.partial(_experts_kernel, dff=dff, dm=dm),
        grid_spec=grid_spec,
        out_shape=jax.ShapeDtypeStruct((rows, dm), F32),
        compiler_params=_cparams(("arbitrary",), VMEM_LIMIT),
        name="experts",
    )(bexp, nused, xs, w1, b1, w2, b2)


def _combine_kernel(dest_ref, dnext_ref, gate_ref, x_ref, g_ref, y_ref, o_ref, buf, sem, *, tm, group):
    i = pl.program_id(0)
    slot = lax.rem(i, 2)

    def gather(idx_ref, s):
        def issue(it, carry):
            for u in range(ISSUE_UNROLL):
                r = it * ISSUE_UNROLL + u
                for k in range(TOP_K):
                    d = idx_ref[r * TOP_K + k]
                    pltpu.make_async_copy(y_ref.at[pl.ds(d, 1)], buf.at[s, k, pl.ds(r, 1)], sem.at[s]).start()
            return carry

        lax.fori_loop(0, tm // ISSUE_UNROLL, issue, 0)

    @pl.when(i == 0)
    def _():
        gather(dest_ref, 0)

    @pl.when(i + 1 < pl.num_programs(0))
    def _():
        gather(dnext_ref, 1 - slot)

    for k in range(TOP_K):
        pltpu.make_async_copy(y_ref.at[pl.ds(0, tm)], buf.at[slot, k], sem.at[slot]).wait()
    gate = gate_ref[...]
    moe = gate[:, 0:1] * buf[slot, 0]
    for k in range(1, TOP_K):
        moe = moe + gate[:, k:k + 1] * buf[slot, k]
    o_ref[...] = x_ref[...] + _group_affine(moe, g_ref[...], None, group)


def _combine(dest1d, gate, x, modg, l, y, group):
    n, d = x.shape
    tm = _pick_tile(n, 512)
    tg = tm // group
    nt = n // tm
    return pl.pallas_call(
        functools.partial(_combine_kernel, tm=tm, group=group),
        grid=(nt,),
        in_specs=[pl.BlockSpec((tm * TOP_K,), lambda i: (i,), memory_space=pltpu.SMEM),
                  pl.BlockSpec((tm * TOP_K,), lambda i: (jnp.minimum(i + 1, nt - 1),), memory_space=pltpu.SMEM),
                  pl.BlockSpec((tm, LANE), lambda i: (i, 0)),
                  pl.BlockSpec((tm, d), lambda i: (i, 0)),
                  pl.BlockSpec((None, None, tg, d), lambda i: (l, 5, i, 0)),
                  pl.BlockSpec(memory_space=pl.ANY)],
        out_specs=pl.BlockSpec((tm, d), lambda i: (i, 0)),
        out_shape=jax.ShapeDtypeStruct((n, d), F32),
        scratch_shapes=[pltpu.VMEM((2, TOP_K, tm, d), F32), pltpu.SemaphoreType.DMA((2,))],
        compiler_params=_cparams(("arbitrary",), VMEM_LIMIT),
        name="combine",
    )(dest1d, dest1d, gate, x, modg, y)


def _final_norm_kernel(x_ref, g_ref, o_ref):
    o_ref[...] = _rms(x_ref[...], g_ref[...])


def _final_norm(x, g, row0, nrows):
    d = x.shape[1]
    tm = _pick_tile(math.gcd(row0, nrows) if row0 else nrows, 512)
    blk0 = row0 // tm
    return pl.pallas_call(
        _final_norm_kernel,
        grid=(nrows // tm,),
        in_specs=[pl.BlockSpec((tm, d), lambda i: (blk0 + i, 0)), pl.BlockSpec((1, d), lambda i: (0, 0))],
        out_specs=pl.BlockSpec((tm, d), lambda i: (i, 0)),
        out_shape=jax.ShapeDtypeStruct((nrows, d), F32),
        compiler_params=_cparams(("arbitrary",)),
        name="final_norm",
    )(x, g.reshape(1, d))


def _pad_heads(w, heads, dk):
    lead = w.shape[:-1]
    w = w.reshape(*lead, heads, dk)
    w = jnp.pad(w, [(0, 0)] * len(lead) + [(0, 0), (0, LANE - dk)])
    return w.reshape(*lead, heads * LANE)


def _pad_cols(w, width):
    return jnp.pad(w, [(0, 0)] * (w.ndim - 1) + [(0, width - w.shape[-1])])


EVEN_WIDTHS = (H_A * LANE, H_A * LANE, H_A * DV_A, H_A * DV_A, LANE, Q_LORA, KV_LORA, LANE)


def _even_weight(w):
    a_qk, a_v = H_A * DK_A, H_A * DV_A
    c = [0, a_qk, 2 * a_qk, 2 * a_qk + a_v, 2 * a_qk + 2 * a_v, 2 * a_qk + 2 * a_v + GATE_RANK]
    c.append(c[-1] + Q_LORA)
    c.append(c[-1] + KV_LORA)
    c.append(c[-1] + ROPE)
    parts = [
        _pad_heads(w[:, c[0]:c[1]], H_A, DK_A),
        _pad_heads(w[:, c[1]:c[2]], H_A, DK_A),
        w[:, c[2]:c[3]],
        w[:, c[3]:c[4]],
        _pad_cols(w[:, c[4]:c[5]], LANE),
        w[:, c[5]:c[6]],
        w[:, c[6]:c[7]],
        _pad_cols(w[:, c[7]:c[8]], LANE),
    ]
    return jnp.concatenate(parts, axis=1).astype(BF16)


def _uq_weight(w):
    w = w.reshape(Q_LORA, H_B, NOPE + ROPE)
    half = ROPE // 2
    nope = w[:, :, :NOPE].reshape(Q_LORA, H_B * NOPE)
    r1 = w[:, :, NOPE:NOPE + half].reshape(Q_LORA, H_B * half)
    r2 = w[:, :, NOPE + half:].reshape(Q_LORA, H_B * half)
    return jnp.concatenate([nope, r1, r2], axis=1).astype(BF16)


def _rope_perm():
    half = ROPE // 2
    r = jnp.arange(2 * LANE)
    second = r // LANE
    h = (r % LANE) // half
    i = r % half
    col = h * LANE + second * half + i
    return (col[:, None] == jnp.arange(H_B * LANE)[None, :]).astype(BF16)


def _rope_tables(segments):
    half = ROPE // 2
    inv_freq = jnp.exp(-math.log(ROPE_BASE) * jnp.arange(half, dtype=F32) / half)
    cos, sin = [], []
    for pos0, length, reps in segments:
        ang = (jnp.arange(length) + pos0).astype(F32)[:, None] * inv_freq[None, :]
        cos.append(jnp.tile(jnp.cos(ang), (reps, 1)))
        sin.append(jnp.tile(jnp.sin(ang), (reps, 1)))
    cos, sin = jnp.concatenate(cos, axis=0), jnp.concatenate(sin, axis=0)
    z = jnp.zeros((cos.shape[0], LANE - ROPE), F32)
    cosq, sinq = jnp.tile(cos, (1, LANE // half)), jnp.tile(sin, (1, LANE // half))
    cosk = jnp.concatenate([cos, cos, z], axis=1)
    sink = jnp.concatenate([sin, sin, z], axis=1)
    return cosq, sinq, cosk, sink


def kernel(x_prompt, x_sample, cache_mla_latent, cache_mla_krope, state_gla, state_hgrn, c_prompt, c_sample,
           w_ada, b_ada, norm1_g, norm2_g, w_in_even, w_gla_a2, b_gla_a, gla_norm_g, mla_q_norm_g, w_mla_uq,
           mla_kv_norm_g, w_mla_uk, w_mla_uv, w_out_even, w_in_odd, hgrn_lb, hgrn_norm_g, w_out_odd,
           w_router, b_router, w_e1, b_e1, w_e2, b_e2, final_norm_g):
    bp, tp, d = x_prompt.shape
    bs, ts, _ = x_sample.shape
    past = cache_mla_latent.shape[2]
    depth = w_ada.shape[0]
    nexp = w_router.shape[2]
    n_p, n_s = bp * tp, bs * ts
    n = n_p + n_s
    group = ts
    assert tp % group == 0 and group % 8 == 0 and tp % CHUNK == 0 and ts <= CHUNK
    assert (n * TOP_K) % MOE_BLOCK == 0
    n_even, n_odd = (depth + 1) // 2, depth // 2

    x = jnp.concatenate([x_prompt.reshape(n_p, d), x_sample.reshape(n_s, d)], axis=0)

    mod = _ada(jnp.concatenate([c_prompt, c_sample], axis=0), w_ada, b_ada)
    mod = mod.reshape(depth, bp + bs, 6, d).transpose(0, 2, 1, 3)
    modg = jnp.concatenate([jnp.repeat(mod[:, :, :bp], tp // group, axis=2), mod[:, :, bp:]], axis=2)

    tabs = _rope_tables([(0, tp, bp), (past, ts, bs)])
    perm = _rope_perm()

    lb_soft = jax.nn.softmax(hgrn_lb.astype(F32), axis=0)
    lb_all = jnp.cumsum(lb_soft, axis=0) - lb_soft[0]
    gla_s0 = jnp.pad(state_gla, ((0, 0), (0, 0), (0, 0), (0, LANE - DK_A), (0, 0)))

    lat_p, kpe_p, gla_p, hgrn_p, lat_s, kpe_s, gla_s, hgrn_s = [], [], [], [], [], [], [], []
    rows = (-(-(n * TOP_K) // MOE_BLOCK)) * MOE_BLOCK + nexp * MOE_BLOCK
    nblocks = rows // MOE_BLOCK

    for l in range(depth):
        j = l // 2
        wr = _pad_cols(w_router[l], LANE)
        wr_hi = wr.astype(BF16)
        wr_lo = (wr - wr_hi.astype(F32)).astype(BF16)
        br = _pad_cols(b_router[l].reshape(1, -1), LANE)
        route = (norm2_g[l], wr_hi, wr_lo, br, group, nexp)
        if l % 2 == 0:
            zq, zk, zv, zr, za, zcq, zckv, zkpe = _inproj(
                x, norm1_g[l], modg, l, _even_weight(w_in_even[j]), EVEN_WIDTHS, group)
            wa = _pad_heads(jnp.pad(w_gla_a2[j], ((0, LANE - GATE_RANK), (0, 0))), H_A, DK_A)
            wa_hi = wa.astype(BF16)
            wa_lo = (wa - wa_hi.astype(F32)).astype(BF16)
            ab = _pad_heads(b_gla_a[j].reshape(1, -1), H_A, DK_A)
            gn = gla_norm_g[j].reshape(1, DV_A)
            extras = (za, wa_hi, wa_lo, ab)
            oa_p, sp = _recurrence("gla", zq, zk, zv, zr, extras, gn, None, heads=H_A, nseq=bp,
                                   seqlen=tp, chunk=CHUNK, row0=0)
            oa_s, ss = _recurrence("gla", zq, zk, zv, zr, extras, gn, (gla_s0, j), heads=H_A, nseq=bs,
                                   seqlen=ts, chunk=ts, row0=n_p)
            gla_p.append(sp[:, :, :DK_A, :])
            gla_s.append(ss[:, :, :DK_A, :])

            wuk = w_mla_uk[j].transpose(1, 2, 0).astype(BF16)
            wuv = w_mla_uv[j].transpose(1, 0, 2).astype(BF16)
            wuvt = w_mla_uv[j].transpose(1, 2, 0).astype(BF16)
            qcat, kcat, lat, kpe, qt, latt = _mla_pre(
                zcq, zckv, zkpe, tabs, mla_q_norm_g[j].reshape(1, -1), mla_kv_norm_g[j].reshape(1, -1),
                _uq_weight(w_mla_uq[j]), wuk, perm)
            ob_p = _attn_prompt(qt, kcat, latt, wuvt, bp, tp)
            ob_s = _attn_sample(qcat, kcat, cache_mla_latent, cache_mla_krope, j, wuv, n_p)
            lat_p.append(lat[:n_p].reshape(bp, tp, KV_LORA))
            lat_s.append(lat[n_p:].reshape(bs, ts, KV_LORA))
            kpe_p.append(kpe[:n_p].reshape(bp, tp, ROPE))
            kpe_s.append(kpe[n_p:].reshape(bs, ts, ROPE))
            wo = w_out_even[j].astype(BF16)
            x, h2, topi, gate = _outproj_router(x, modg, l, [oa_p, ob_p], [oa_s, ob_s],
                                                [wo[:H_A * DV_A], wo[H_A * DV_A:]], *route)
        else:
            zq, zf, zi, zg = _inproj(x, norm1_g[l], modg, l, w_in_odd[j].astype(BF16),
                                     (H_C * DK_C,) * 2 + (H_C * DV_C,) * 2, group)
            extras = (lb_all[l].reshape(1, -1),)
            gn = hgrn_norm_g[j].reshape(1, DV_C)
            oc_p, sp = _recurrence("hgrn", zq, zf, zi, zg, extras, gn, None, heads=H_C, nseq=bp,
                                   seqlen=tp, chunk=CHUNK, row0=0)
            oc_s, ss = _recurrence("hgrn", zq, zf, zi, zg, extras, gn, (state_hgrn, j), heads=H_C, nseq=bs,
                                   seqlen=ts, chunk=ts, row0=n_p)
            hgrn_p.append(sp)
            hgrn_s.append(ss)
            x, h2, topi, gate = _outproj_router(x, modg, l, [oc_p], [oc_s], [w_out_odd[j].astype(BF16)],
                                                *route)

        rank, cnt = _rank(topi)
        counts = cnt[0, :nexp].astype(I32)
        padded = (counts + MOE_BLOCK - 1) // MOE_BLOCK * MOE_BLOCK
        pends = jnp.cumsum(padded)
        pstarts = _pad_cols((pends - padded).astype(F32).reshape(1, -1), LANE)
        dest = _dest(topi, rank, pstarts)
        dest1d = dest[:, :TOP_K].reshape(n * TOP_K)
        bexp = jnp.clip(jnp.sum(pends[None, :] <= (jnp.arange(nblocks) * MOE_BLOCK)[:, None], axis=1),
                        0, nexp - 1).astype(I32)
        nused = (pends[-1:] // MOE_BLOCK).astype(I32)
        padinfo = jnp.stack([_pad_cols(pends, LANE), _pad_cols(padded - counts, LANE),
                             _pad_cols(nused, LANE)]).astype(I32)
        xs = _dispatch(dest1d, padinfo, h2, rows, nexp)
        y = _experts(bexp, nused, xs, w_e1, b_e1.reshape(depth, nexp, 1, -1), w_e2,
                     b_e2.reshape(depth, nexp, 1, -1), l)
        x = _combine(dest1d, gate, x, modg, l, y, group)

    y_p = _final_norm(x, final_norm_g, 0, n_p)
    y_s = _final_norm(x, final_norm_g, n_p, n_s)
    return (y_p.reshape(bp, tp, d), y_s.reshape(bs, ts, d),
            jnp.stack(lat_p), jnp.stack(kpe_p), jnp.stack(gla_p), jnp.stack(hgrn_p),
            jnp.stack(lat_s), jnp.stack(kpe_s), jnp.stack(gla_s), jnp.stack(hgrn_s))
```

```python
import functools
import math

import jax
import jax.numpy as jnp
from jax import lax
from jax.experimental import pallas as pl
from jax.experimental.pallas import tpu as pltpu

F32 = jnp.float32
BF16 = jnp.bfloat16
I32 = jnp.int32

EPS = 1e-6
CHUNK = 64
LANE = 128

H_A, DK_A, DV_A, GATE_RANK, GATE_TAU = 4, 64, 128, 16, 16.0
H_B, Q_LORA, KV_LORA, NOPE, ROPE, V_B = 4, 384, 256, 128, 64, 128
ROPE_BASE = 10000.0
MLA_SCALE = (NOPE + ROPE) ** -0.5
KCAT = KV_LORA + LANE
H_C, DK_C, DV_C = 8, 128, 128
TOP_K = 4
SWIGLU_LIMIT = 7.0
SWIGLU_ALPHA = 1.702
MOE_BLOCK = 512
ISSUE_UNROLL = 8
EXP_CLAMP = 80.0

VMEM_LIMIT = 56 * 1024 * 1024


def _cparams(sem, vmem=None):
    return pltpu.CompilerParams(dimension_semantics=sem, vmem_limit_bytes=vmem)


def _pick_tile(n, pref):
    t = pref
    while n % t:
        t //= 2
    return t


def _rms(x, g):
    return x * lax.rsqrt(jnp.mean(x * x, axis=-1, keepdims=True) + EPS) * g


def _group_affine(x, scale, shift, group):
    tm = x.shape[0]
    pieces = []
    for gi in range(tm // group):
        xg = x[gi * group:(gi + 1) * group, :]
        if scale is not None:
            xg = xg * scale[gi:gi + 1, :]
        if shift is not None:
            xg = xg + shift[gi:gi + 1, :]
        pieces.append(xg)
    return jnp.concatenate(pieces, axis=0)


def _cumsum_rows(x):
    c, w = x.shape
    row = lax.broadcasted_iota(I32, (c, 1), 0)
    s = 1
    while s < c:
        if s % 8 == 0:
            shifted = jnp.concatenate([jnp.zeros((s, w), x.dtype), x[:c - s, :]], axis=0)
        else:
            shifted = jnp.where(row >= s, pltpu.roll(x, s, 0), 0.0)
        x = x + shifted
        s *= 2
    return x


def _dot(a, b):
    return jnp.dot(a, b, preferred_element_type=F32)


def _dot_nt(a, b):
    return lax.dot_general(a, b, (((1,), (1,)), ((), ())), preferred_element_type=F32)


def _dot_tn(a, b):
    return lax.dot_general(a, b, (((0,), (0,)), ((), ())), preferred_element_type=F32)


def _ada_kernel(c_ref, w_ref, b_ref, o_ref):
    c = c_ref[...]
    a = (c * jax.nn.sigmoid(c)).astype(BF16)
    o_ref[...] = _dot(a, w_ref[...].astype(BF16)) + b_ref[...]


def _ada(c_all, w_ada, b_ada):
    depth, d, n6 = w_ada.shape
    s = c_all.shape[0]
    tn = _pick_tile(n6, 1536)
    return pl.pallas_call(
        _ada_kernel,
        grid=(depth, n6 // tn),
        in_specs=[
            pl.BlockSpec((s, d), lambda l, j: (0, 0)),
            pl.BlockSpec((None, d, tn), lambda l, j: (l, 0, j)),
            pl.BlockSpec((None, 1, tn), lambda l, j: (l, 0, j)),
        ],
        out_specs=pl.BlockSpec((None, s, tn), lambda l, j: (l, 0, j)),
        out_shape=jax.ShapeDtypeStruct((depth, s, n6), F32),
        compiler_params=_cparams(("arbitrary", "arbitrary")),
        name="ada",
    )(c_all, w_ada, b_ada.reshape(depth, 1, n6))


def _inproj_kernel(x_ref, g_ref, sc_ref, sh_ref, w_ref, *out_refs, splits, group):
    h = _rms(x_ref[...], g_ref[...])
    h = _group_affine(h, 1.0 + sc_ref[...], sh_ref[...], group).astype(BF16)
    for o_ref, (c0, c1) in zip(out_refs, splits):
        o_ref[...] = _dot(h, w_ref[:, c0:c1]).astype(o_ref.dtype)


def _inproj(x, gain, modg, l, w, widths, group):
    n, d = x.shape
    tm = _pick_tile(n, 512)
    tg = tm // group
    splits, c = [], 0
    for wd in widths:
        splits.append((c, c + wd))
        c += wd
    mod_spec = lambda comp: pl.BlockSpec((None, None, tg, d), lambda i: (l, comp, i, 0))
    return pl.pallas_call(
        functools.partial(_inproj_kernel, splits=tuple(splits), group=group),
        grid=(n // tm,),
        in_specs=[
            pl.BlockSpec((tm, d), lambda i: (i, 0)),
            pl.BlockSpec((1, d), lambda i: (0, 0)),
            mod_spec(1),
            mod_spec(0),
            pl.BlockSpec((d, c), lambda i: (0, 0)),
        ],
        out_specs=[pl.BlockSpec((tm, wd), lambda i: (i, 0)) for wd in widths],
        out_shape=[jax.ShapeDtypeStruct((n, wd), F32) for wd in widths],
        compiler_params=_cparams(("arbitrary",), VMEM_LIMIT),
        name="inproj",
    )(x, gain.reshape(1, d), modg, modg, w)


REC_SEQS = 8


def _rec_kernel(*refs, mode, heads, chunk, nchunks, zero_init, nseqs):
    refs = list(refs)
    ntok = 5 if mode == "gla" else 4
    tok_refs = [refs[s * ntok:(s + 1) * ntok] for s in range(nseqs)]
    refs = refs[nseqs * ntok:]
    if mode == "gla":
        wah_ref, wal_ref, ab_ref, gn_ref = refs[:4]
        refs = refs[4:]
    else:
        lb_ref, gn_ref = refs[:2]
        refs = refs[2:]
    s0_ref = None
    if not zero_init:
        s0_ref = refs.pop(0)
    o_ref, sout_ref = refs[:2]
    st_scr = refs[2:]
    ci = pl.program_id(1)

    @pl.when(ci == 0)
    def _():
        for s in range(nseqs):
            for h in range(heads):
                if zero_init:
                    st_scr[s * heads + h][...] = jnp.zeros((LANE, LANE), F32)
                else:
                    st_scr[s * heads + h][...] = s0_ref[s, h].T

    row = lax.broadcasted_iota(I32, (chunk, chunk), 0)
    col = lax.broadcasted_iota(I32, (chunk, chunk), 1)
    causal = row >= col
    mid = chunk // 2 - 1

    for s in range(nseqs):
        if mode == "gla":
            q_ref, k_ref, v_ref, r_ref, a_ref = tok_refs[s]
            a = a_ref[...]
            a_hi = a.astype(BF16)
            a_lo = (a - a_hi.astype(F32)).astype(BF16)
            alog_all = (_dot(a_hi, wah_ref[...]) + _dot(a_lo, wah_ref[...]) + _dot(a_hi, wal_ref[...])
                        + ab_ref[...])
            g_all = jax.nn.log_sigmoid(alog_all) * (1.0 / GATE_TAU)
            fg_all = None
        else:
            q_ref, k_ref, v_ref, r_ref = tok_refs[s]
            lb = lb_ref[...]
            fg_all = lb + (1.0 - lb) * jax.nn.sigmoid(k_ref[...])
            g_all = jnp.log(fg_all)
        b_all = _cumsum_rows(g_all)

        for h in range(heads):
            sl = slice(h * LANE, (h + 1) * LANE)
            if mode == "gla":
                q = q_ref[:, sl] * (DK_A ** -0.5)
                k = k_ref[:, sl]
            else:
                qr = q_ref[:, sl]
                q = qr * jax.nn.sigmoid(qr)
                k = 1.0 - fg_all[:, sl]
            b = b_all[:, sl]
            b_last = b[chunk - 1:chunk, :]
            b_ref_row = b[mid:mid + 1, :]
            qt = (q * jnp.exp(jnp.minimum(b - b_ref_row, EXP_CLAMP))).astype(BF16)
            kt = (k * jnp.exp(jnp.minimum(b_ref_row - b, EXP_CLAMP))).astype(BF16)
            qs = (q * jnp.exp(b)).astype(BF16)
            ks = (k * jnp.exp(b_last - b)).astype(BF16)
            v = v_ref[:, sl].astype(BF16)
            att = jnp.where(causal, _dot_nt(qt, kt), 0.0).astype(BF16)
            st_ref = st_scr[s * heads + h]
            st = st_ref[...]
            o = _dot(att, v) + _dot_nt(qs, st.astype(BF16))
            st_ref[...] = st * jnp.exp(b_last) + _dot_tn(v, ks)
            rg = r_ref[:, sl]
            o = _rms(o, gn_ref[...]) * (rg * jax.nn.sigmoid(rg))
            o_ref[s, :, sl] = o.astype(o_ref.dtype)

    @pl.when(ci == nchunks - 1)
    def _():
        for s in range(nseqs):
            for h in range(heads):
                sout_ref[s, h] = st_scr[s * heads + h][...].T


def _recurrence(mode, q, k, v, r, extras, gnorm, s0, *, heads, nseq, seqlen, chunk, row0):
    cols = q.shape[1]
    nchunks = seqlen // chunk
    blk0 = row0 // chunk
    ns = REC_SEQS if nseq % REC_SEQS == 0 else 1
    full = lambda a: pl.BlockSpec(a.shape, lambda b, c: (0,) * a.ndim)
    in_specs, args = [], []
    for s in range(ns):
        tok = lambda b, c, s=s: (blk0 + (b * ns + s) * nchunks + c, 0)
        toks = [q, k, v, r] + ([extras[0]] if mode == "gla" else [])
        in_specs += [pl.BlockSpec((chunk, a.shape[1]), tok) for a in toks]
        args += toks
    consts = list(extras[1:]) if mode == "gla" else list(extras)
    consts.append(gnorm)
    in_specs += [full(a) for a in consts]
    args += consts
    zero_init = s0 is None
    if not zero_init:
        s0, s0_layer = s0
        in_specs.append(pl.BlockSpec((None, ns, heads, LANE, LANE), lambda b, c: (s0_layer, b, 0, 0, 0)))
        args.append(s0)
    o, st = pl.pallas_call(
        functools.partial(_rec_kernel, mode=mode, heads=heads, chunk=chunk, nchunks=nchunks,
                          zero_init=zero_init, nseqs=ns),
        grid=(nseq // ns, nchunks),
        in_specs=in_specs,
        out_specs=[pl.BlockSpec((ns, chunk, cols), lambda b, c: (b, c, 0)),
                   pl.BlockSpec((ns, heads, LANE, LANE), lambda b, c: (b, 0, 0, 0))],
        out_shape=[jax.ShapeDtypeStruct((nseq, seqlen, cols), BF16),
                   jax.ShapeDtypeStruct((nseq, heads, LANE, LANE), F32)],
        scratch_shapes=[pltpu.VMEM((LANE, LANE), F32) for _ in range(ns * heads)],
        compiler_params=_cparams(("arbitrary", "arbitrary")),
        name="recurrence_" + mode,
    )(*args)
    return o.reshape(nseq * seqlen, cols), st


def _mla_pre_kernel(cq_ref, ckv_ref, kpe_ref, cosq_ref, sinq_ref, cosk_ref, sink_ref, qg_ref, kvg_ref,
                    wuq_ref, wuk_ref, perm_ref, qcat_ref, kcat_ref, lat_ref, kpeo_ref, qt_ref, latt_ref, *, tq):
    tm = cq_ref.shape[0]
    cqn = _rms(cq_ref[...], qg_ref[...]).astype(BF16)
    qf = _dot(cqn, wuq_ref[...])
    off = H_B * NOPE
    x1 = qf[:, off:off + LANE]
    x2 = qf[:, off + LANE:off + 2 * LANE]
    cq, sq = cosq_ref[...], sinq_ref[...]
    o1 = (x1 * cq - x2 * sq) * MLA_SCALE
    o2 = (x2 * cq + x1 * sq) * MLA_SCALE
    pe = _dot(o1.astype(BF16), perm_ref[0:LANE, :]) + _dot(o2.astype(BF16), perm_ref[LANE:2 * LANE, :])
    for h in range(H_B):
        ql = _dot(qf[:, h * NOPE:(h + 1) * NOPE].astype(BF16), wuk_ref[h]) * MLA_SCALE
        peh = pe[:, h * LANE:(h + 1) * LANE]
        qcat_ref[h, :, 0:KV_LORA] = ql.astype(BF16)
        qcat_ref[h, :, KV_LORA:KCAT] = peh.astype(BF16)
        for jb in range(tm // tq):
            rows = slice(jb * tq, (jb + 1) * tq)
            cols = slice(h * tq, (h + 1) * tq)
            qt_ref[jb, 0:KV_LORA, cols] = ql[rows, :].T.astype(BF16)
            qt_ref[jb, KV_LORA:KCAT, cols] = peh[rows, :].T.astype(BF16)
    latn = _rms(ckv_ref[...], kvg_ref[...])
    lat_ref[...] = latn
    latt_ref[...] = latn.T.astype(BF16)
    x = kpe_ref[...]
    half = ROPE // 2
    lane = lax.broadcasted_iota(I32, x.shape, 1)
    rot = jnp.where(lane < half, -pltpu.roll(x, LANE - half, 1), pltpu.roll(x, half, 1))
    kro = x * cosk_ref[...] + rot * sink_ref[...]
    kpeo_ref[...] = kro[:, 0:ROPE]
    kcat_ref[:, 0:KV_LORA] = latn.astype(BF16)
    kcat_ref[:, KV_LORA:KCAT] = kro.astype(BF16)


ATTN_TQ = 8 * CHUNK


def _mla_pre(zcq, zckv, zkpe, tabs, qg, kvg, wuq, wuk, perm):
    n = zcq.shape[0]
    tm = _pick_tile(n, 512)
    tq = ATTN_TQ
    assert tm % tq == 0
    tokspec = lambda wd: pl.BlockSpec((tm, wd), lambda i: (i, 0))
    full = lambda a: pl.BlockSpec(a.shape, lambda i: (0,) * a.ndim)
    return pl.pallas_call(
        functools.partial(_mla_pre_kernel, tq=tq),
        grid=(n // tm,),
        in_specs=[tokspec(Q_LORA), tokspec(KV_LORA), tokspec(LANE)] + [tokspec(LANE)] * 4
                 + [full(qg), full(kvg), full(wuq), full(wuk), full(perm)],
        out_specs=[pl.BlockSpec((H_B, tm, KCAT), lambda i: (0, i, 0)), tokspec(KCAT),
                   tokspec(KV_LORA), tokspec(ROPE),
                   pl.BlockSpec((tm // tq, KCAT, H_B * tq), lambda i: (i, 0, 0)),
                   pl.BlockSpec((KV_LORA, tm), lambda i: (0, i))],
        out_shape=[jax.ShapeDtypeStruct((H_B, n, KCAT), BF16), jax.ShapeDtypeStruct((n, KCAT), BF16),
                   jax.ShapeDtypeStruct((n, KV_LORA), F32), jax.ShapeDtypeStruct((n, ROPE), F32),
                   jax.ShapeDtypeStruct((n // tq, KCAT, H_B * tq), BF16),
                   jax.ShapeDtypeStruct((KV_LORA, n), BF16)],
        compiler_params=_cparams(("arbitrary",)),
        name="mla_pre",
    )(zcq, zckv, zkpe, *tabs, qg, kvg, wuq, wuk, perm)


def _softmax_update(s, vals, m_scr, l_scr, acc_scr):
    m_prev = m_scr[...]
    m_new = jnp.maximum(m_prev, jnp.max(s, axis=1, keepdims=True))
    alpha = jnp.exp(m_prev - m_new)
    p = jnp.exp(s - m_new)
    l_scr[...] = alpha * l_scr[...] + jnp.sum(p, axis=1, keepdims=True)
    acc_scr[...] = alpha * acc_scr[...] + _dot(p.astype(BF16), vals)
    m_scr[...] = m_new


def _softmax_init(m_scr, l_scr, acc_scr):
    m_scr[...] = jnp.full(m_scr.shape, -jnp.inf, F32)
    l_scr[...] = jnp.zeros(l_scr.shape, F32)
    acc_scr[...] = jnp.zeros(acc_scr.shape, F32)


def _attn_finish(o_ref, wuv_ref, l_scr, acc_scr, tq):
    inv = 1.0 / l_scr[...]
    for h in range(H_B):
        rows = slice(h * tq, (h + 1) * tq)
        oh = (acc_scr[rows, :] * inv[rows, :]).astype(BF16)
        o_ref[:, h * V_B:(h + 1) * V_B] = _dot(oh, wuv_ref[h]).astype(o_ref.dtype)


def _attn_prompt_kernel(qt_ref, k_ref, latt_ref, wuvt_ref, o_ref, m_scr, l_scr, acc_scr, p_scr, *, tq, tk):
    qi = pl.program_id(1)
    cols = H_B * tq
    qt = qt_ref[...]
    m_scr[...] = jnp.full(m_scr.shape, -jnp.inf, F32)
    l_scr[...] = jnp.zeros(l_scr.shape, F32)
    acc_scr[...] = jnp.zeros(acc_scr.shape, F32)

    def block(start, masked):
        s = _dot(k_ref[pl.ds(start, tk), :], qt)
        if masked:
            tok = qi * tq + (lax.broadcasted_iota(I32, (1, cols), 1) & (tq - 1))
            limit = (tok // CHUNK + 1) * CHUNK
            key = start + lax.broadcasted_iota(I32, (tk, 1), 0)
            s = jnp.where(key < limit, s, -jnp.inf)
        m_prev = m_scr[...]
        m_new = jnp.maximum(m_prev, jnp.max(s, axis=0, keepdims=True))
        alpha = jnp.exp(m_prev - m_new)
        p = jnp.exp(s - m_new)
        l_scr[...] = alpha * l_scr[...] + jnp.sum(p, axis=0, keepdims=True)
        m_scr[...] = m_new
        p_scr[...] = p.astype(BF16)
        acc_scr[...] = alpha * acc_scr[...] + _dot(latt_ref[:, pl.ds(start, tk)], p_scr[...])

    nfull = (qi * tq) // tk

    def full(j, carry):
        block(pl.multiple_of(j * tk, tk), False)
        return carry

    lax.fori_loop(0, nfull, full, 0)
    block(pl.multiple_of(nfull * tk, tk), True)

    ot =(acc_scr[...] * (1.0 / l_scr[...])).astype(BF16)
    for h in range(H_B):
        oh = _dot(wuvt_ref[h], ot[:, h * tq:(h + 1) * tq])
        o_ref[:, h * V_B:(h + 1) * V_B] = oh.T.astype(o_ref.dtype)


def _attn_prompt(qt, kcat, latt, wuvt, nseq, seqlen):
    n = nseq * seqlen
    tq = ATTN_TQ
    tk = 512
    assert seqlen % tk == 0 and tk % tq == 0
    nq = seqlen // tq
    return pl.pallas_call(
        functools.partial(_attn_prompt_kernel, tq=tq, tk=tk),
        grid=(nseq, nq),
        in_specs=[
            pl.BlockSpec((None, KCAT, H_B * tq), lambda b, qi: (b * nq + qi, 0, 0)),
            pl.BlockSpec((seqlen, KCAT), lambda b, qi: (b, 0)),
            pl.BlockSpec((KV_LORA, seqlen), lambda b, qi: (0, b)),
            pl.BlockSpec(wuvt.shape, lambda b, qi: (0, 0, 0)),
        ],
        out_specs=pl.BlockSpec((tq, H_B * V_B), lambda b, qi: (b * nq + qi, 0)),
        out_shape=jax.ShapeDtypeStruct((n, H_B * V_B), BF16),
        scratch_shapes=[pltpu.VMEM((1, H_B * tq), F32), pltpu.VMEM((1, H_B * tq), F32),
                        pltpu.VMEM((KV_LORA, H_B * tq), F32), pltpu.VMEM((tk, H_B * tq), BF16)],
        compiler_params=_cparams(("arbitrary", "arbitrary")),
        name="attn_prompt",
    )(qt, kcat, latt, wuvt)


def _attn_sample_kernel(q_ref, plat_ref, pkpe_ref, knew_ref, wuv_ref, o_ref,
                        m_scr, l_scr, acc_scr, *, tq, nkp):
    ki = pl.program_id(1)

    @pl.when(ki == 0)
    def _():
        _softmax_init(m_scr, l_scr, acc_scr)

    q = q_ref[...].reshape(H_B * tq, KCAT)
    lat = plat_ref[...].astype(BF16)
    kpe = pkpe_ref[...].astype(BF16)
    s = _dot_nt(q[:, 0:KV_LORA], lat) + _dot_nt(q[:, KV_LORA:KV_LORA + ROPE], kpe)
    _softmax_update(s, lat, m_scr, l_scr, acc_scr)

    @pl.when(ki == nkp - 1)
    def _():
        kn = knew_ref[...]
        _softmax_update(_dot_nt(q, kn), kn[:, 0:KV_LORA], m_scr, l_scr, acc_scr)
        _attn_finish(o_ref, wuv_ref, l_scr, acc_scr, tq)


def _attn_sample(qcat, kcat, past_lat, past_kpe, layer, wuv, row0):
    _, nseq, past, _ = past_lat.shape
    tq = (kcat.shape[0] - row0) // nseq
    tkp = _pick_tile(past, 2048)
    nkp = past // tkp
    blk0 = row0 // tq
    return pl.pallas_call(
        functools.partial(_attn_sample_kernel, tq=tq, nkp=nkp),
        grid=(nseq, nkp),
        in_specs=[
            pl.BlockSpec((H_B, tq, KCAT), lambda b, ki: (0, blk0 + b, 0)),
            pl.BlockSpec((None, None, tkp, KV_LORA), lambda b, ki: (layer, b, ki, 0)),
            pl.BlockSpec((None, None, tkp, ROPE), lambda b, ki: (layer, b, ki, 0)),
            pl.BlockSpec((tq, KCAT), lambda b, ki: (blk0 + b, 0)),
            pl.BlockSpec(wuv.shape, lambda b, ki: (0, 0, 0)),
        ],
        out_specs=pl.BlockSpec((tq, H_B * V_B), lambda b, ki: (b, 0)),
        out_shape=jax.ShapeDtypeStruct((nseq * tq, H_B * V_B), BF16),
        scratch_shapes=[pltpu.VMEM((H_B * tq, 1), F32), pltpu.VMEM((H_B * tq, 1), F32),
                        pltpu.VMEM((H_B * tq, KV_LORA), F32)],
        compiler_params=_cparams(("arbitrary", "arbitrary")),
        name="attn_sample",
    )(qcat, past_lat, past_kpe, kcat, wuv)


def _outproj_router_kernel(*refs, nlhs, group, ptiles, nexp):
    x_ref, g_ref = refs[0], refs[1]
    lhs_p = refs[2:2 + nlhs]
    lhs_s = refs[2 + nlhs:2 + 2 * nlhs]
    ws = refs[2 + 2 * nlhs:2 + 3 * nlhs]
    n2_ref, sc_ref, sh_ref, wh_ref, wl_ref, br_ref, o_ref, h_ref, ti_ref, gt_ref = refs[2 + 3 * nlhs:]
    is_prompt = pl.program_id(0) < ptiles
    acc = None
    for ap, asm, w in zip(lhs_p, lhs_s, ws):
        a = jnp.where(is_prompt, ap[...], asm[...])
        t = _dot(a, w[...])
        acc = t if acc is None else acc + t
    x = x_ref[...] + _group_affine(acc, g_ref[...], None, group)
    o_ref[...] = x
    _route(x, n2_ref, sc_ref, sh_ref, wh_ref, wl_ref, br_ref, h_ref, ti_ref, gt_ref, group, nexp)


def _outproj_router(x, modg, l, lhs_p, lhs_s, ws, gain2, wr_hi, wr_lo, br, group, nexp):
    n, d = x.shape
    n_p, n_s = lhs_p[0].shape[0], lhs_s[0].shape[0]
    tm = _pick_tile(math.gcd(n_p, n_s), 512)
    tg = tm // group
    ptiles = n_p // tm
    mod_spec = lambda comp: pl.BlockSpec((None, None, tg, d), lambda i: (l, comp, i, 0))
    full = lambda a: pl.BlockSpec(a.shape, lambda i: (0,) * a.ndim)
    tok = lambda wd: pl.BlockSpec((tm, wd), lambda i: (i, 0))
    return pl.pallas_call(
        functools.partial(_outproj_router_kernel, nlhs=len(ws), group=group, ptiles=ptiles, nexp=nexp),
        grid=(n // tm,),
        in_specs=[tok(d), mod_spec(2)]
                 + [pl.BlockSpec((tm, a.shape[1]), lambda i: (jnp.minimum(i, ptiles - 1), 0)) for a in lhs_p]
                 + [pl.BlockSpec((tm, a.shape[1]), lambda i: (jnp.maximum(i - ptiles, 0), 0)) for a in lhs_s]
                 + [full(w) for w in ws]
                 + [pl.BlockSpec((1, d), lambda i: (0, 0)), mod_spec(4), mod_spec(3),
                    full(wr_hi), full(wr_lo), full(br)],
        out_specs=[tok(d), tok(d), tok(LANE), tok(LANE)],
        out_shape=[jax.ShapeDtypeStruct((n, d), F32), jax.ShapeDtypeStruct((n, d), F32),
                   jax.ShapeDtypeStruct((n, LANE), I32), jax.ShapeDtypeStruct((n, LANE), F32)],
        compiler_params=_cparams(("arbitrary",), VMEM_LIMIT),
        name="outproj_router",
    )(x, modg, *lhs_p, *lhs_s, *ws, gain2.reshape(1, d), modg, modg, wr_hi, wr_lo, br)


def _route(x, g_ref, sc_ref, sh_ref, wh_ref, wl_ref, br_ref, h_ref, ti_ref, gt_ref, group, nexp):
    h = _rms(x, g_ref[...])
    h = _group_affine(h, 1.0 + sc_ref[...], sh_ref[...], group)
    h_ref[...] = h
    h_hi = h.astype(BF16)
    h_lo = (h - h_hi.astype(F32)).astype(BF16)
    logits = _dot(h_hi, wh_ref[...]) + _dot(h_lo, wh_ref[...]) + _dot(h_hi, wl_ref[...]) + br_ref[...]
    lane = lax.broadcasted_iota(I32, logits.shape, 1)
    lane_f = lane.astype(F32)
    cur = jnp.where(lane < nexp, logits, -jnp.inf)
    tops, idxs = [], []
    for _ in range(TOP_K):
        m = jnp.max(cur, axis=1, keepdims=True)
        i = jnp.min(jnp.where(cur == m, lane_f, float(LANE)), axis=1, keepdims=True)
        cur = jnp.where(lane_f == i, -jnp.inf, cur)
        tops.append(m)
        idxs.append(i.astype(I32))
    es = [jnp.exp(t - tops[0]) for t in tops]
    inv = 1.0 / (es[0] + es[1] + es[2] + es[3])
    ti = jnp.zeros(logits.shape, I32)
    gt = jnp.zeros(logits.shape, F32)
    for k in range(TOP_K):
        ti = jnp.where(lane == k, idxs[k], ti)
        gt = jnp.where(lane == k, es[k] * inv, gt)
    ti_ref[...] = ti
    gt_ref[...] = gt


def _rank_kernel(ti_ref, rank_ref, cnt_ref, carry_scr, *, tm):
    i = pl.program_id(0)

    @pl.when(i == 0)
    def _():
        carry_scr[...] = jnp.zeros(carry_scr.shape, F32)

    ti = ti_ref[...]
    lane = lax.broadcasted_iota(I32, ti.shape, 1)
    sel = [lane == ti[:, k:k + 1] for k in range(TOP_K)]
    hot = jnp.zeros(ti.shape, F32)
    for s in sel:
        hot = hot + jnp.where(s, 1.0, 0.0)
    row = lax.broadcasted_iota(I32, (tm, tm), 0)
    col = lax.broadcasted_iota(I32, (tm, tm), 1)
    strict = jnp.where(row > col, 1.0, 0.0).astype(BF16)
    before = _dot(strict, hot.astype(BF16)) + carry_scr[0:1, :]
    rank = jnp.zeros(ti.shape, F32)
    for k in range(TOP_K):
        rk = jnp.sum(jnp.where(sel[k], before, 0.0), axis=1, keepdims=True)
        rank = jnp.where(lane == k, rk, rank)
    rank_ref[...] = rank.astype(I32)
    carry_scr[...] = carry_scr[...] + jnp.sum(hot, axis=0, keepdims=True)
    cnt_ref[...] = carry_scr[...]


def _rank(topi):
    n = topi.shape[0]
    tm = _pick_tile(n, 512)
    return pl.pallas_call(
        functools.partial(_rank_kernel, tm=tm),
        grid=(n // tm,),
        in_specs=[pl.BlockSpec((tm, LANE), lambda i: (i, 0))],
        out_specs=[pl.BlockSpec((tm, LANE), lambda i: (i, 0)), pl.BlockSpec((8, LANE), lambda i: (0, 0))],
        out_shape=[jax.ShapeDtypeStruct((n, LANE), I32), jax.ShapeDtypeStruct((8, LANE), F32)],
        scratch_shapes=[pltpu.VMEM((8, LANE), F32)],
        compiler_params=_cparams(("arbitrary",)),
        name="rank",
    )(topi)


def _dest_kernel(ti_ref, rank_ref, ps_ref, d_ref):
    ti = ti_ref[...]
    lane = lax.broadcasted_iota(I32, ti.shape, 1)
    ps = ps_ref[...]
    dest = rank_ref[...]
    for k in range(TOP_K):
        base = jnp.sum(jnp.where(lane == ti[:, k:k + 1], ps, 0.0), axis=1, keepdims=True).astype(I32)
        dest = dest + jnp.where(lane == k, base, 0)
    d_ref[...] = dest


def _dest(topi, rank, pstarts_row):
    n = topi.shape[0]
    tm = _pick_tile(n, 1024)
    tok = pl.BlockSpec((tm, LANE), lambda i: (i, 0))
    return pl.pallas_call(
        _dest_kernel,
        grid=(n // tm,),
        in_specs=[tok, tok, pl.BlockSpec((1, LANE), lambda i: (0, 0))],
        out_specs=tok,
        out_shape=jax.ShapeDtypeStruct((n, LANE), I32),
        compiler_params=_cparams(("arbitrary",)),
        name="dest",
    )(topi, rank, pstarts_row)


def _dispatch_kernel(dest_ref, pad_ref, h_ref, xs_ref, hbuf, zbuf, sem, zsem, *, tm, nexp, nblocks):
    i = pl.program_id(0)
    slot = lax.rem(i, 2)
    hbuf[slot] = h_ref[...]

    def issue(it, carry):
        for u in range(ISSUE_UNROLL):
            r = it * ISSUE_UNROLL + u
            for k in range(TOP_K):
                d = dest_ref[r * TOP_K + k]
                pltpu.make_async_copy(hbuf.at[slot, pl.ds(r, 1)], xs_ref.at[pl.ds(d, 1)],
                                      sem.at[slot]).start(priority=k % 2)
        return carry

    lax.fori_loop(0, tm // ISSUE_UNROLL, issue, 0)

    def retire(s):
        for _ in range(TOP_K):
            pltpu.make_async_copy(hbuf.at[s], xs_ref.at[pl.ds(0, tm)], sem.at[s]).wait()

    @pl.when(i >= 1)
    def _():
        retire(1 - slot)

    @pl.when(i == pl.num_programs(0) - 1)
    def _():
        retire(slot)
        zbuf[...] = jnp.zeros(zbuf.shape, F32)

        def fill_expert(e, carry):
            end = pad_ref[0, e]
            npad = pad_ref[1, e]
            p = MOE_BLOCK // 2
            while p >= 1:
                bit = npad & p
                end = end - bit

                @pl.when(bit != 0)
                def _(end=end, p=p):
                    if p >= 8:
                        start = pl.multiple_of(end, 8)
                        pltpu.make_async_copy(zbuf.at[pl.ds(0, p)], xs_ref.at[pl.ds(start, p)], zsem).start()
                    else:
                        for q in range(p):
                            pltpu.make_async_copy(zbuf.at[pl.ds(0, 1)], xs_ref.at[pl.ds(end + q, 1)],
                                                  zsem).start()

                p //= 2
            return carry

        lax.fori_loop(0, nexp, fill_expert, 0)
        nused = pad_ref[2, 0]

        def fill_tail(b, carry):
            @pl.when(b >= nused)
            def _():
                start = pl.multiple_of(b * MOE_BLOCK, MOE_BLOCK)
                pltpu.make_async_copy(zbuf, xs_ref.at[pl.ds(start, MOE_BLOCK)], zsem).start()

            return carry

        lax.fori_loop(0, nblocks, fill_tail, 0)
        for _ in range(nexp):
            pltpu.make_async_copy(zbuf, xs_ref.at[pl.ds(0, MOE_BLOCK)], zsem).wait()


def _dispatch(dest1d, padinfo, h, rows, nexp):
    n, d = h.shape
    tm = _pick_tile(n, 512)
    return pl.pallas_call(
        functools.partial(_dispatch_kernel, tm=tm, nexp=nexp, nblocks=rows // MOE_BLOCK),
        grid=(n // tm,),
        in_specs=[pl.BlockSpec((tm * TOP_K,), lambda i: (i,), memory_space=pltpu.SMEM),
                  pl.BlockSpec(memory_space=pltpu.SMEM),
                  pl.BlockSpec((tm, d), lambda i: (i, 0))],
        out_specs=pl.BlockSpec(memory_space=pl.ANY),
        out_shape=jax.ShapeDtypeStruct((rows, d), F32),
        scratch_shapes=[pltpu.VMEM((2, tm, d), F32), pltpu.VMEM((MOE_BLOCK, d), F32),
                        pltpu.SemaphoreType.DMA((2,)), pltpu.SemaphoreType.DMA(())],
        compiler_params=pltpu.CompilerParams(dimension_semantics=("arbitrary",), has_side_effects=True),
        name="dispatch",
    )(dest1d, padinfo, h)


def _experts_kernel(bexp_ref, nused_ref, xs_ref, w1_ref, b1_ref, w2_ref, b2_ref, y_ref,
                    w1b, w2b, act_scr, prev_scr, *, dff, dm):
    i = pl.program_id(0)
    nu = nused_ref[0]
    e = bexp_ref[jnp.minimum(i, nu - 1)]

    @pl.when(i == 0)
    def _():
        prev_scr[0] = -1

    @pl.when(i >= nu)
    def _():
        y_ref[...] = jnp.zeros(y_ref.shape, F32)

    @pl.when(i < nu)
    def _():
        @pl.when(e != prev_scr[0])
        def _():
            rows = 128

            def cast1(c, carry):
                r0 = pl.multiple_of(c * rows, rows)
                w1b[pl.ds(r0, rows), :] = w1_ref[pl.ds(r0, rows), :].astype(BF16)
                return carry

            def cast2(c, carry):
                r0 = pl.multiple_of(c * rows, rows)
                w2b[pl.ds(r0, rows), :] = w2_ref[pl.ds(r0, rows), :].astype(BF16)
                return carry

            lax.fori_loop(0, dm // rows, cast1, 0)
            lax.fori_loop(0, dff // rows, cast2, 0)
            prev_scr[0] = e

        x = xs_ref[...].astype(BF16)
        cw = 256
        for c in range(dff // cw):
            gt = _dot(x, w1b[:, c * cw:(c + 1) * cw]) + b1_ref[:, c * cw:(c + 1) * cw]
            up = _dot(x, w1b[:, dff + c * cw:dff + (c + 1) * cw]) + b1_ref[:, dff + c * cw:dff + (c + 1) * cw]
            gt = jnp.minimum(gt, SWIGLU_LIMIT)
            up = jnp.clip(up, -SWIGLU_LIMIT, SWIGLU_LIMIT)
            act = gt * jax.nn.sigmoid(gt * SWIGLU_ALPHA) * (up + 1.0)
            act_scr[:, c * cw:(c + 1) * cw] = act.astype(BF16)
        a = act_scr[...]
        for c in range(dm // cw):
            y_ref[:, c * cw:(c + 1) * cw] = _dot(a, w2b[:, c * cw:(c + 1) * cw]) + b2_ref[:, c * cw:(c + 1) * cw]


def _experts(bexp, nused, xs, w1, b1, w2, b2, l):
    rows, dm = xs.shape
    dff = w2.shape[2]
    nb = rows // MOE_BLOCK
    blk = lambda i, be, nu: (jnp.minimum(i, nu[0] - 1), 0)
    wsel = lambda i, be, nu: (l, be[jnp.minimum(i, nu[0] - 1)], 0, 0)
    grid_spec = pltpu.PrefetchScalarGridSpec(
        num_scalar_prefetch=2,
        grid=(nb,),
        in_specs=[
            pl.BlockSpec((MOE_BLOCK, dm), blk),
            pl.BlockSpec((None, None, dm, 2 * dff), wsel),
            pl.BlockSpec((None, None, 1, 2 * dff), wsel),
            pl.BlockSpec((None, None, dff, dm), wsel),
            pl.BlockSpec((None, None, 1, dm), wsel),
        ],
        out_specs=pl.BlockSpec((MOE_BLOCK, dm), lambda i, be, nu: (i, 0)),
        scratch_shapes=[pltpu.VMEM((dm, 2 * dff), BF16), pltpu.VMEM((dff, dm), BF16),
                        pltpu.VMEM((MOE_BLOCK, dff), BF16), pltpu.SMEM((1,), I32)],
    )
    return pl.pallas_call(
        functools.partial(_experts_kernel, dff=dff, dm=dm),
        grid_spec=grid_spec,
        out_shape=jax.ShapeDtypeStruct((rows, dm), F32),
        compiler_params=_cparams(("arbitrary",), VMEM_LIMIT),
        name="experts",
    )(bexp, nused, xs, w1, b1, w2, b2)


def _combine_kernel(dest_ref, dnext_ref, gate_ref, x_ref, g_ref, y_ref, o_ref, buf, sem, *, tm, group):
    i = pl.program_id(0)
    slot = lax.rem(i, 2)

    def gather(idx_ref, s):
        def issue(it, carry):
            for u in range(ISSUE_UNROLL):
                r = it * ISSUE_UNROLL + u
                for k in range(TOP_K):
                    d = idx_ref[r * TOP_K + k]
                    pltpu.make_async_copy(y_ref.at[pl.ds(d, 1)], buf.at[s, k, pl.ds(r, 1)],
                                          sem.at[s]).start(priority=k % 2)
            return carry

        lax.fori_loop(0, tm // ISSUE_UNROLL, issue, 0)

    @pl.when(i == 0)
    def _():
        gather(dest_ref, 0)

    @pl.when(i + 1 < pl.num_programs(0))
    def _():
        gather(dnext_ref, 1 - slot)

    for k in range(TOP_K):
        pltpu.make_async_copy(y_ref.at[pl.ds(0, tm)], buf.at[slot, k], sem.at[slot]).wait()
    gate = gate_ref[...]
    moe = gate[:, 0:1] * buf[slot, 0]
    for k in range(1, TOP_K):
        moe = moe + gate[:, k:k + 1] * buf[slot, k]
    o_ref[...] = x_ref[...] + _group_affine(moe, g_ref[...], None, group)


def _combine(dest1d, gate, x, modg, l, y, group):
    n, d = x.shape
    tm = _pick_tile(n, 512)
    tg = tm // group
    nt = n // tm
    return pl.pallas_call(
        functools.partial(_combine_kernel, tm=tm, group=group),
        grid=(nt,),
        in_specs=[pl.BlockSpec((tm * TOP_K,), lambda i: (i,), memory_space=pltpu.SMEM),
                  pl.BlockSpec((tm * TOP_K,), lambda i: (jnp.minimum(i + 1, nt - 1),), memory_space=pltpu.SMEM),
                  pl.BlockSpec((tm, LANE), lambda i: (i, 0)),
                  pl.BlockSpec((tm, d), lambda i: (i, 0)),
                  pl.BlockSpec((None, None, tg, d), lambda i: (l, 5, i, 0)),
                  pl.BlockSpec(memory_space=pl.ANY)],
        out_specs=pl.BlockSpec((tm, d), lambda i: (i, 0)),
        out_shape=jax.ShapeDtypeStruct((n, d), F32),
        scratch_shapes=[pltpu.VMEM((2, TOP_K, tm, d), F32), pltpu.SemaphoreType.DMA((2,))],
        compiler_params=_cparams(("arbitrary",), VMEM_LIMIT),
        name="combine",
    )(dest1d, dest1d, gate, x, modg, y)


def _final_norm_kernel(x_ref, g_ref, o_ref):
    o_ref[...] = _rms(x_ref[...], g_ref[...])


def _final_norm(x, g, row0, nrows):
    d = x.shape[1]
    tm = _pick_tile(math.gcd(row0, nrows) if row0 else nrows, 512)
    blk0 = row0 // tm
    return pl.pallas_call(
        _final_norm_kernel,
        grid=(nrows // tm,),
        in_specs=[pl.BlockSpec((tm, d), lambda i: (blk0 + i, 0)), pl.BlockSpec((1, d), lambda i: (0, 0))],
        out_specs=pl.BlockSpec((tm, d), lambda i: (i, 0)),
        out_shape=jax.ShapeDtypeStruct((nrows, d), F32),
        compiler_params=_cparams(("arbitrary",)),
        name="final_norm",
    )(x, g.reshape(1, d))


def _pad_heads(w, heads, dk):
    lead = w.shape[:-1]
    w = w.reshape(*lead, heads, dk)
    w = jnp.pad(w, [(0, 0)] * len(lead) + [(0, 0), (0, LANE - dk)])
    return w.reshape(*lead, heads * LANE)


def _pad_cols(w, width):
    return jnp.pad(w, [(0, 0)] * (w.ndim - 1) + [(0, width - w.shape[-1])])


EVEN_WIDTHS = (H_A * LANE, H_A * LANE, H_A * DV_A, H_A * DV_A, LANE, Q_LORA, KV_LORA, LANE)


def _even_weight(w):
    a_qk, a_v = H_A * DK_A, H_A * DV_A
    c = [0, a_qk, 2 * a_qk, 2 * a_qk + a_v, 2 * a_qk + 2 * a_v, 2 * a_qk + 2 * a_v + GATE_RANK]
    c.append(c[-1] + Q_LORA)
    c.append(c[-1] + KV_LORA)
    c.append(c[-1] + ROPE)
    parts = [
        _pad_heads(w[:, c[0]:c[1]], H_A, DK_A),
        _pad_heads(w[:, c[1]:c[2]], H_A, DK_A),
        w[:, c[2]:c[3]],
        w[:, c[3]:c[4]],
        _pad_cols(w[:, c[4]:c[5]], LANE),
        w[:, c[5]:c[6]],
        w[:, c[6]:c[7]],
        _pad_cols(w[:, c[7]:c[8]], LANE),
    ]
    return jnp.concatenate(parts, axis=1).astype(BF16)


def _uq_weight(w):
    w = w.reshape(Q_LORA, H_B, NOPE + ROPE)
    half = ROPE // 2
    nope = w[:, :, :NOPE].reshape(Q_LORA, H_B * NOPE)
    r1 = w[:, :, NOPE:NOPE + half].reshape(Q_LORA, H_B * half)
    r2 = w[:, :, NOPE + half:].reshape(Q_LORA, H_B * half)
    return jnp.concatenate([nope, r1, r2], axis=1).astype(BF16)


def _rope_perm():
    half = ROPE // 2
    r = jnp.arange(2 * LANE)
    second = r // LANE
    h = (r % LANE) // half
    i = r % half
    col = h * LANE + second * half + i
    return (col[:, None] == jnp.arange(H_B * LANE)[None, :]).astype(BF16)


def _rope_tables(segments):
    half = ROPE // 2
    inv_freq = jnp.exp(-math.log(ROPE_BASE) * jnp.arange(half, dtype=F32) / half)
    cos, sin = [], []
    for pos0, length, reps in segments:
        ang = (jnp.arange(length) + pos0).astype(F32)[:, None] * inv_freq[None, :]
        cos.append(jnp.tile(jnp.cos(ang), (reps, 1)))
        sin.append(jnp.tile(jnp.sin(ang), (reps, 1)))
    cos, sin = jnp.concatenate(cos, axis=0), jnp.concatenate(sin, axis=0)
    z = jnp.zeros((cos.shape[0], LANE - ROPE), F32)
    cosq, sinq = jnp.tile(cos, (1, LANE // half)), jnp.tile(sin, (1, LANE // half))
    cosk = jnp.concatenate([cos, cos, z], axis=1)
    sink = jnp.concatenate([sin, sin, z], axis=1)
    return cosq, sinq, cosk, sink


def kernel(x_prompt, x_sample, cache_mla_latent, cache_mla_krope, state_gla, state_hgrn, c_prompt, c_sample,
           w_ada, b_ada, norm1_g, norm2_g, w_in_even, w_gla_a2, b_gla_a, gla_norm_g, mla_q_norm_g, w_mla_uq,
           mla_kv_norm_g, w_mla_uk, w_mla_uv, w_out_even, w_in_odd, hgrn_lb, hgrn_norm_g, w_out_odd,
           w_router, b_router, w_e1, b_e1, w_e2, b_e2, final_norm_g):
    bp, tp, d = x_prompt.shape
    bs, ts, _ = x_sample.shape
    past = cache_mla_latent.shape[2]
    depth = w_ada.shape[0]
    nexp = w_router.shape[2]
    n_p, n_s = bp * tp, bs * ts
    n = n_p + n_s
    group = ts
    assert tp % group == 0 and group % 8 == 0 and tp % CHUNK == 0 and ts <= CHUNK
    assert (n * TOP_K) % MOE_BLOCK == 0
    n_even, n_odd = (depth + 1) // 2, depth // 2

    x = jnp.concatenate([x_prompt.reshape(n_p, d), x_sample.reshape(n_s, d)], axis=0)

    mod = _ada(jnp.concatenate([c_prompt, c_sample], axis=0), w_ada, b_ada)
    mod = mod.reshape(depth, bp + bs, 6, d).transpose(0, 2, 1, 3)
    modg = jnp.concatenate([jnp.repeat(mod[:, :, :bp], tp // group, axis=2), mod[:, :, bp:]], axis=2)

    tabs = _rope_tables([(0, tp, bp), (past, ts, bs)])
    perm = _rope_perm()

    lb_soft = jax.nn.softmax(hgrn_lb.astype(F32), axis=0)
    lb_all = jnp.cumsum(lb_soft, axis=0) - lb_soft[0]
    gla_s0 = jnp.pad(state_gla, ((0, 0), (0, 0), (0, 0), (0, LANE - DK_A), (0, 0)))

    lat_p, kpe_p, gla_p, hgrn_p, lat_s, kpe_s, gla_s, hgrn_s = [], [], [], [], [], [], [], []
    rows = (-(-(n * TOP_K) // MOE_BLOCK)) * MOE_BLOCK + nexp * MOE_BLOCK
    nblocks = rows // MOE_BLOCK

    for l in range(depth):
        j = l // 2
        wr = _pad_cols(w_router[l], LANE)
        wr_hi = wr.astype(BF16)
        wr_lo = (wr - wr_hi.astype(F32)).astype(BF16)
        br = _pad_cols(b_router[l].reshape(1, -1), LANE)
        route = (norm2_g[l], wr_hi, wr_lo, br, group, nexp)
        if l % 2 == 0:
            zq, zk, zv, zr, za, zcq, zckv, zkpe = _inproj(
                x, norm1_g[l], modg, l, _even_weight(w_in_even[j]), EVEN_WIDTHS, group)
            wa = _pad_heads(jnp.pad(w_gla_a2[j], ((0, LANE - GATE_RANK), (0, 0))), H_A, DK_A)
            wa_hi = wa.astype(BF16)
            wa_lo = (wa - wa_hi.astype(F32)).astype(BF16)
            ab = _pad_heads(b_gla_a[j].reshape(1, -1), H_A, DK_A)
            gn = gla_norm_g[j].reshape(1, DV_A)
            extras = (za, wa_hi, wa_lo, ab)
            oa_p, sp = _recurrence("gla", zq, zk, zv, zr, extras, gn, None, heads=H_A, nseq=bp,
                                   seqlen=tp, chunk=CHUNK, row0=0)
            oa_s, ss = _recurrence("gla", zq, zk, zv, zr, extras, gn, (gla_s0, j), heads=H_A, nseq=bs,
                                   seqlen=ts, chunk=ts, row0=n_p)
            gla_p.append(sp[:, :, :DK_A, :])
            gla_s.append(ss[:, :, :DK_A, :])

            wuk = w_mla_uk[j].transpose(1, 2, 0).astype(BF16)
            wuv = w_mla_uv[j].transpose(1, 0, 2).astype(BF16)
            wuvt = w_mla_uv[j].transpose(1, 2, 0).astype(BF16)
            qcat, kcat, lat, kpe, qt, latt = _mla_pre(
                zcq, zckv, zkpe, tabs, mla_q_norm_g[j].reshape(1, -1), mla_kv_norm_g[j].reshape(1, -1),
                _uq_weight(w_mla_uq[j]), wuk, perm)
            ob_p = _attn_prompt(qt, kcat, latt, wuvt, bp, tp)
            ob_s = _attn_sample(qcat, kcat, cache_mla_latent, cache_mla_krope, j, wuv, n_p)
            lat_p.append(lat[:n_p].reshape(bp, tp, KV_LORA))
            lat_s.append(lat[n_p:].reshape(bs, ts, KV_LORA))
            kpe_p.append(kpe[:n_p].reshape(bp, tp, ROPE))
            kpe_s.append(kpe[n_p:].reshape(bs, ts, ROPE))
            wo = w_out_even[j].astype(BF16)
            x, h2, topi, gate = _outproj_router(x, modg, l, [oa_p, ob_p], [oa_s, ob_s],
                                                [wo[:H_A * DV_A], wo[H_A * DV_A:]], *route)
        else:
            zq, zf, zi, zg = _inproj(x, norm1_g[l], modg, l, w_in_odd[j].astype(BF16),
                                     (H_C * DK_C,) * 2 + (H_C * DV_C,) * 2, group)
            extras = (lb_all[l].reshape(1, -1),)
            gn = hgrn_norm_g[j].reshape(1, DV_C)
            oc_p, sp = _recurrence("hgrn", zq, zf, zi, zg, extras, gn, None, heads=H_C, nseq=bp,
                                   seqlen=tp, chunk=CHUNK, row0=0)
            oc_s, ss = _recurrence("hgrn", zq, zf, zi, zg, extras, gn, (state_hgrn, j), heads=H_C, nseq=bs,
                                   seqlen=ts, chunk=ts, row0=n_p)
            hgrn_p.append(sp)
            hgrn_s.append(ss)
            x, h2, topi, gate = _outproj_router(x, modg, l, [oc_p], [oc_s], [w_out_odd[j].astype(BF16)],
                                                *route)

        rank, cnt = _rank(topi)
        counts = cnt[0, :nexp].astype(I32)
        padded = (counts + MOE_BLOCK - 1) // MOE_BLOCK * MOE_BLOCK
        pends = jnp.cumsum(padded)
        pstarts = _pad_cols((pends - padded).astype(F32).reshape(1, -1), LANE)
        dest = _dest(topi, rank, pstarts)
        dest1d = dest[:, :TOP_K].reshape(n * TOP_K)
        bexp = jnp.clip(jnp.sum(pends[None, :] <= (jnp.arange(nblocks) * MOE_BLOCK)[:, None], axis=1),
                        0, nexp - 1).astype(I32)
        nused = (pends[-1:] // MOE_BLOCK).astype(I32)
        padinfo = jnp.stack([_pad_cols(pends, LANE), _pad_cols(padded - counts, LANE),
                             _pad_cols(nused, LANE)]).astype(I32)
        xs = _dispatch(dest1d, padinfo, h2, rows, nexp)
        y = _experts(bexp, nused, xs, w_e1, b_e1.reshape(depth, nexp, 1, -1), w_e2,
                     b_e2.reshape(depth, nexp, 1, -1), l)
        x = _combine(dest1d, gate, x, modg, l, y, group)

    y_p = _final_norm(x, final_norm_g, 0, n_p)
    y_s = _final_norm(x, final_norm_g, n_p, n_s)
    return (y_p.reshape(bp, tp, d), y_s.reshape(bs, ts, d),
            jnp.stack(lat_p), jnp.stack(kpe_p), jnp.stack(gla_p), jnp.stack(hgrn_p),
            jnp.stack(lat_s), jnp.stack(kpe_s), jnp.stack(gla_s), jnp.stack(hgrn_s))
```
